```python
import math
import jax, jax.numpy as jnp
from jax import lax
import numpy as np

D_MODEL = 2048
BATCH = 4
SEQ = 2048
DEPTH = 1

MEM_LEN = 256
EPS = 1e-6
DA_HEADS = 8
DA_HEAD_DIM = D_MODEL // 32
DA_V_DIM = 2 * DA_HEAD_DIM
DA_WIDTH = DA_HEADS * DA_V_DIM
Q_BLOCK = 128
SSD_HEAD_DIM = 64
SSD_WIDTH = D_MODEL
SSD_HEADS = SSD_WIDTH // SSD_HEAD_DIM
SSD_GROUPS = 4
SSD_STATE = 128
SSD_CONV = 4
SSD_CHUNK = 128
CONV_DIM = SSD_WIDTH + 2 * SSD_GROUPS * SSD_STATE
MIX_WIDTH = DA_WIDTH + SSD_WIDTH
SPLITS = (DA_WIDTH, 2 * DA_WIDTH, 3 * DA_WIDTH,
          3 * DA_WIDTH + SSD_WIDTH, 3 * DA_WIDTH + SSD_WIDTH + CONV_DIM)
IN_COLS = 3 * DA_WIDTH + SSD_WIDTH + CONV_DIM + SSD_HEADS
X_HEADS = 4
X_HEAD_DIM = D_MODEL // 16
X_WIDTH = X_HEADS * X_HEAD_DIM
D_FF = -(-8 * D_MODEL // (3 * 256)) * 256

kernel_name = "hybrid_diffattn_ssd_xattn_swiglu"


def rms_norm(x, w, eps=EPS):
    xf = x.astype(jnp.float32)
    y = xf * lax.rsqrt(jnp.mean(xf * xf, axis=-1, keepdims=True) + eps)
    return (y * w.astype(jnp.float32)).astype(x.dtype)


def diff_attention(q, k, v, q_norm_w, k_norm_w, lam, subln_w, lambda_init):
    b, s = q.shape[:2]
    q = rms_norm(q, q_norm_w) * (DA_HEAD_DIM ** -0.5)
    k = rms_norm(k, k_norm_w)
    nb = s // Q_BLOCK
    qb = q.reshape(b, nb, Q_BLOCK, DA_HEADS, 2, DA_HEAD_DIM).transpose(1, 0, 2, 3, 4, 5)
    k_pos = jnp.arange(s)

    def block(args):
        q_blk, i = args
        sc = jnp.einsum('bqhmd,bkhmd->bhmqk', q_blk, k,
                        preferred_element_type=jnp.float32)
        q_pos = i * Q_BLOCK + jnp.arange(Q_BLOCK)
        mask = k_pos[None, :] <= q_pos[:, None]
        p = jax.nn.softmax(jnp.where(mask, sc, -jnp.inf), axis=-1)
        p = p[:, :, 0] - lam * p[:, :, 1]
        return jnp.einsum('bhqk,bkhv->bqhv', p.astype(v.dtype), v)

    o = lax.map(block, (qb, jnp.arange(nb)))
    o = o.transpose(1, 0, 2, 3, 4).reshape(b, s, DA_HEADS, DA_V_DIM)
    o = rms_norm(o, subln_w) * (1.0 - lambda_init)
    return o.reshape(b, s, DA_WIDTH)


def ssd_mixer(z, xbc, dt_raw, conv_w, conv_b, dt_bias, a_log, d_skip, norm_w):
    b, s = z.shape[:2]
    G, E, P, N, L = SSD_GROUPS, SSD_HEADS // SSD_GROUPS, SSD_HEAD_DIM, SSD_STATE, SSD_CHUNK
    nc = s // L
    xbc = lax.conv_general_dilated(xbc, conv_w[:, None, :], window_strides=(1,),
                                   padding=[(SSD_CONV - 1, 0)],
                                   dimension_numbers=('NWC', 'WIO', 'NWC'),
                                   feature_group_count=CONV_DIM) + conv_b
    xbc = jax.nn.silu(xbc)
    xs, bm, cm = jnp.split(xbc, [SSD_WIDTH, SSD_WIDTH + G * N], axis=-1)
    dt = jax.nn.softplus(dt_raw.astype(jnp.float32) + dt_bias.astype(jnp.float32))
    a = -jnp.exp(a_log.astype(jnp.float32))
    x = xs.reshape(b, nc, L, G, E, P)
    bc = bm.reshape(b, nc, L, G, N)
    cc = cm.reshape(b, nc, L, G, N)
    dtc = dt.reshape(b, nc, L, G, E)
    xdt = x * dtc[..., None]
    acs = jnp.cumsum((dtc * a.reshape(G, E)).transpose(0, 1, 3, 4, 2), axis=-1)
    causal = jnp.tril(jnp.ones((L, L), dtype=bool))
    seg = acs[..., :, None] - acs[..., None, :]
    decay = jnp.exp(jnp.where(causal, seg, -jnp.inf))
    cb = jnp.einsum('bclgn,bcsgn->bcgls', cc, bc, preferred_element_type=jnp.float32)
    y_diag = jnp.einsum('bcgls,bcgels,bcsgep->bclgep', cb, decay, xdt)
    decay_states = jnp.exp(acs[..., -1:] - acs)
    states = jnp.einsum('bclgn,bcgel,bclgep->bcgepn', bc, decay_states, xdt)
    chunk_decay = jnp.exp(acs[..., -1])

    def step(carry, inp):
        st, dec = inp
        return carry * dec[..., None, None] + st, carry

    init = jnp.zeros((b, G, E, P, N), dtype=states.dtype)
    _, prev = lax.scan(step, init, (states.transpose(1, 0, 2, 3, 4, 5),
                                    chunk_decay.transpose(1, 0, 2, 3)))
    prev = prev.transpose(1, 0, 2, 3, 4, 5)
    y_off = jnp.einsum('bclgn,bcgepn,bcgel->bclgep', cc, prev, jnp.exp(acs))
    y = y_diag + y_off + x * d_skip.reshape(G, E)[..., None]
    y = y.reshape(b, s, SSD_WIDTH).astype(z.dtype) * jax.nn.silu(z)
    y = rms_norm(y.reshape(b, s, G, SSD_WIDTH // G), norm_w.reshape(G, SSD_WIDTH // G))
    return y.reshape(b, s, SSD_WIDTH)


def cross_attention(hn, mem_n, w_q, w_kv, q_norm_w, k_norm_w, w_o):
    b, s = hn.shape[:2]
    m = mem_n.shape[1]
    q = (hn @ w_q).reshape(b, s, X_HEADS, X_HEAD_DIM)
    k, v = jnp.split(mem_n @ w_kv, 2, axis=-1)
    k = k.reshape(b, m, X_HEADS, X_HEAD_DIM)
    v = v.reshape(b, m, X_HEADS, X_HEAD_DIM)
    q = rms_norm(q, q_norm_w) * (X_HEAD_DIM ** -0.5)
    k = rms_norm(k, k_norm_w)
    sc = jnp.einsum('bqhd,bkhd->bhqk', q, k, preferred_element_type=jnp.float32)
    p = jax.nn.softmax(sc, axis=-1)
    o = jnp.einsum('bhqk,bkhd->bqhd', p.astype(v.dtype), v).reshape(b, s, X_WIDTH)
    return o @ w_o


def setup_inputs(seed: int = 0) -> dict:
    key = jax.random.key(seed)
    ks = jax.random.split(key, 32)
    f32 = jnp.float32
    nrm = lambda k, shape, scale: jax.random.normal(k, shape, f32) * scale
    gain = lambda k, shape: 1.0 + 0.02 * jax.random.normal(k, shape, f32)
    dt0 = jnp.exp(jax.random.uniform(ks[12], (DEPTH, SSD_HEADS), f32, math.log(1e-3), math.log(1e-1)))
    return {
        "x": nrm(ks[0], (BATCH, SEQ, D_MODEL), 1.0),
        "mem": nrm(ks[1], (BATCH, MEM_LEN, D_MODEL), 1.0),
        "mix_norm_w": gain(ks[2], (DEPTH, D_MODEL)),
        "w_in": nrm(ks[3], (DEPTH, D_MODEL, IN_COLS), D_MODEL ** -0.5),
        "da_q_norm_w": gain(ks[4], (DEPTH, DA_HEAD_DIM)),
        "da_k_norm_w": gain(ks[5], (DEPTH, DA_HEAD_DIM)),
        "da_lambda_q1": nrm(ks[6], (DEPTH, DA_HEAD_DIM), 0.1),
        "da_lambda_k1": nrm(ks[7], (DEPTH, DA_HEAD_DIM), 0.1),
        "da_lambda_q2": nrm(ks[8], (DEPTH, DA_HEAD_DIM), 0.1),
        "da_lambda_k2": nrm(ks[9], (DEPTH, DA_HEAD_DIM), 0.1),
        "da_subln_w": gain(ks[10], (DEPTH, DA_V_DIM)),
        "ssd_conv_w": nrm(ks[11], (DEPTH, SSD_CONV, CONV_DIM), SSD_CONV ** -0.5),
        "ssd_conv_b": nrm(ks[13], (DEPTH, CONV_DIM), 0.02),
        "ssd_dt_bias": dt0 + jnp.log(-jnp.expm1(-dt0)),
        "ssd_a_log": jnp.log(jax.random.uniform(ks[14], (DEPTH, SSD_HEADS), f32, 1.0, 16.0)),
        "ssd_d": gain(ks[15], (DEPTH, SSD_HEADS)),
        "ssd_norm_w": gain(ks[16], (DEPTH, SSD_WIDTH)),
        "w_out": nrm(ks[17], (DEPTH, MIX_WIDTH, D_MODEL), MIX_WIDTH ** -0.5),
        "xattn_norm_w": gain(ks[18], (DEPTH, D_MODEL)),
        "mem_norm_w": gain(ks[19], (DEPTH, D_MODEL)),
        "xattn_w_q": nrm(ks[20], (DEPTH, D_MODEL, X_WIDTH), D_MODEL ** -0.5),
        "xattn_w_kv": nrm(ks[21], (DEPTH, D_MODEL, 2 * X_WIDTH), D_MODEL ** -0.5),
        "xattn_q_norm_w": gain(ks[22], (DEPTH, X_HEAD_DIM)),
        "xattn_k_norm_w": gain(ks[23], (DEPTH, X_HEAD_DIM)),
        "xattn_w_o": nrm(ks[24], (DEPTH, X_WIDTH, D_MODEL), X_WIDTH ** -0.5),
        "ffn_norm_w": gain(ks[25], (DEPTH, D_MODEL)),
        "ffn_w_gate": nrm(ks[26], (DEPTH, D_MODEL, D_FF), D_MODEL ** -0.5),
        "ffn_w_up": nrm(ks[27], (DEPTH, D_MODEL, D_FF), D_MODEL ** -0.5),
        "ffn_w_down": nrm(ks[28], (DEPTH, D_FF, D_MODEL), D_FF ** -0.5),
    }


def reference(x, mem, mix_norm_w, w_in, da_q_norm_w, da_k_norm_w, da_lambda_q1, da_lambda_k1,
              da_lambda_q2, da_lambda_k2, da_subln_w, ssd_conv_w, ssd_conv_b, ssd_dt_bias,
              ssd_a_log, ssd_d, ssd_norm_w, w_out, xattn_norm_w, mem_norm_w, xattn_w_q,
              xattn_w_kv, xattn_q_norm_w, xattn_k_norm_w, xattn_w_o, ffn_norm_w, ffn_w_gate,
              ffn_w_up, ffn_w_down):
    b, s, _ = x.shape
    h = x
    for i in range(DEPTH):
        lambda_init = 0.8 - 0.6 * math.exp(-0.3 * i)
        hn = rms_norm(h, mix_norm_w[i])
        proj = hn @ w_in[i]
        q, k, v, z, xbc, dt_raw = jnp.split(proj, SPLITS, axis=-1)
        q = q.reshape(b, s, DA_HEADS, 2, DA_HEAD_DIM)
        k = k.reshape(b, s, DA_HEADS, 2, DA_HEAD_DIM)
        v = v.reshape(b, s, DA_HEADS, DA_V_DIM)
        lam = (jnp.exp(jnp.sum(da_lambda_q1[i].astype(jnp.float32) * da_lambda_k1[i].astype(jnp.float32)))
               - jnp.exp(jnp.sum(da_lambda_q2[i].astype(jnp.float32) * da_lambda_k2[i].astype(jnp.float32)))
               + lambda_init)
        a_out = diff_attention(q, k, v, da_q_norm_w[i], da_k_norm_w[i], lam, da_subln_w[i], lambda_init)
        s_out = ssd_mixer(z, xbc, dt_raw, ssd_conv_w[i], ssd_conv_b[i], ssd_dt_bias[i],
                          ssd_a_log[i], ssd_d[i], ssd_norm_w[i])
        h = h + jnp.concatenate([a_out, s_out], axis=-1) @ w_out[i]
        mem_n = rms_norm(mem, mem_norm_w[i])
        h = h + cross_attention(rms_norm(h, xattn_norm_w[i]), mem_n, xattn_w_q[i], xattn_w_kv[i],
                                xattn_q_norm_w[i], xattn_k_norm_w[i], xattn_w_o[i])
        hn = rms_norm(h, ffn_norm_w[i])
        h = h + (jax.nn.silu(hn @ ffn_w_gate[i]) * (hn @ ffn_w_up[i])) @ ffn_w_down[i]
    return h
```

```python
import functools
import math

import jax
import jax.numpy as jnp
import numpy as np
from jax import lax
from jax.experimental import pallas as pl
from jax.experimental.pallas import tpu as pltpu

EPS = 1e-6
LANES = 128
SUBLANES = 8
VMEM_LIMIT = 48 * 1024 * 1024

DA_HEADS = 8
DA_HEAD_DIM = 64
DA_V_DIM = 128
SSD_HEAD_DIM = 64
SSD_GROUPS = 4
SSD_STATE = 128
SSD_CONV = 4
SSD_CHUNK = 128
X_HEADS = 4
X_HEAD_DIM = 128

F32 = jnp.float32
BF16 = jnp.bfloat16


def _rms_scale(xf, eps=EPS):
    return lax.rsqrt(jnp.mean(xf * xf, axis=-1, keepdims=True) + eps)


def _silu(x):
    return x / (1.0 + jnp.exp(-x))


def _split_bf16(x, parts):
    out = []
    r = x
    for _ in range(parts):
        p = r.astype(BF16)
        out.append(p)
        r = r - p.astype(F32)
    return out


def _dot(a, b):
    return jnp.dot(a, b, preferred_element_type=F32)


def _dot_nt(a, b):
    return lax.dot_general(a, b, (((1,), (1,)), ((), ())), preferred_element_type=F32)


def _in_proj_kernel(x_ref, nw_ref, w_ref, wdt_ref, o_ref, dt_ref, hn_ref):
    n = pl.program_id(1)

    @pl.when(n == 0)
    def _():
        xf = x_ref[...]
        hn = (xf * _rms_scale(xf) * nw_ref[...]).astype(BF16)
        hn_ref[...] = hn
        dt_ref[...] = _dot(hn, wdt_ref[...])

    o_ref[...] = _dot(hn_ref[...], w_ref[...]).astype(o_ref.dtype)


def _in_proj(x2, nw, w, wdt, tm=512, tn=1024):
    m, k = x2.shape
    n = w.shape[1]
    return pl.pallas_call(
        _in_proj_kernel,
        out_shape=(jax.ShapeDtypeStruct((m, n), BF16), jax.ShapeDtypeStruct((m, LANES), F32)),
        grid=(m // tm, n // tn),
        in_specs=[
            pl.BlockSpec((tm, k), lambda i, j: (i, 0)),
            pl.BlockSpec((1, k), lambda i, j: (0, 0)),
            pl.BlockSpec((k, tn), lambda i, j: (0, j)),
            pl.BlockSpec((k, LANES), lambda i, j: (0, 0)),
        ],
        out_specs=(
            pl.BlockSpec((tm, tn), lambda i, j: (i, j)),
            pl.BlockSpec((tm, LANES), lambda i, j: (i, 0)),
        ),
        scratch_shapes=[pltpu.VMEM((tm, k), BF16)],
        compiler_params=pltpu.CompilerParams(
            dimension_semantics=("parallel", "arbitrary"), vmem_limit_bytes=VMEM_LIMIT),
        name="in_proj",
    )(x2, nw, w, wdt)


def _norm_matmul_kernel(x_ref, nw_ref, w_ref, o_ref):
    xf = x_ref[...]
    hn = (xf * _rms_scale(xf) * nw_ref[...]).astype(BF16)
    o_ref[...] = _dot(hn, w_ref[...]).astype(o_ref.dtype)


def _norm_matmul(x2, nw, w, tm, name):
    m, k = x2.shape
    n = w.shape[1]
    return pl.pallas_call(
        _norm_matmul_kernel,
        out_shape=jax.ShapeDtypeStruct((m, n), BF16),
        grid=(m // tm,),
        in_specs=[
            pl.BlockSpec((tm, k), lambda i: (i, 0)),
            pl.BlockSpec((1, k), lambda i: (0, 0)),
            pl.BlockSpec((k, n), lambda i: (0, 0)),
        ],
        out_specs=pl.BlockSpec((tm, n), lambda i: (i, 0)),
        compiler_params=pltpu.CompilerParams(
            dimension_semantics=("parallel",), vmem_limit_bytes=VMEM_LIMIT),
        name=name,
    )(x2, nw, w)


def _diff_attn_kernel(q_ref, k_ref, v_ref, qw_ref, kw_ref, lam_ref, sw_ref, o_ref,
                      kn_ref, m_ref, l_ref, acc_ref, *, tq, lambda_init):
    i = pl.program_id(2)
    lane = lax.broadcasted_iota(jnp.int32, (1, DA_V_DIM), 1)
    first = lane < DA_HEAD_DIM

    def sub_head_norm(xf):
        sq = xf * xf
        s1 = jnp.sum(jnp.where(first, sq, 0.0), axis=-1, keepdims=True)
        s2 = jnp.sum(jnp.where(first, 0.0, sq), axis=-1, keepdims=True)
        r1 = lax.rsqrt(s1 * (1.0 / DA_HEAD_DIM) + EPS)
        r2 = lax.rsqrt(s2 * (1.0 / DA_HEAD_DIM) + EPS)
        return xf * jnp.where(first, r1, r2)

    @pl.when(i == 0)
    def _():
        kn_ref[...] = (sub_head_norm(k_ref[...].astype(F32)) * kw_ref[...]).astype(BF16)

    qn = sub_head_norm(q_ref[...].astype(F32)) * qw_ref[...] * (DA_HEAD_DIM ** -0.5)
    qs = (jnp.where(first, qn, 0.0).astype(BF16), jnp.where(first, 0.0, qn).astype(BF16))

    m_ref[...] = jnp.full(m_ref.shape, -jnp.inf, F32)
    l_ref[...] = jnp.zeros(l_ref.shape, F32)
    acc_ref[...] = jnp.zeros(acc_ref.shape, F32)

    row = lax.broadcasted_iota(jnp.int32, (tq, tq), 0)
    col = lax.broadcasted_iota(jnp.int32, (tq, tq), 1)
    causal = col <= row

    def tile(j, masked):
        start = pl.multiple_of(j * tq, tq)
        k = kn_ref[pl.ds(start, tq), :]
        v = v_ref[pl.ds(start, tq), :]
        for sub in range(2):
            s = _dot_nt(qs[sub], k)
            if masked:
                s = jnp.where(causal, s, -jnp.inf)
            m_prev = m_ref[sub]
            m_new = jnp.maximum(m_prev, jnp.max(s, axis=-1, keepdims=True))
            alpha = jnp.exp(m_prev - m_new)
            p = jnp.exp(s - m_new)
            l_ref[sub] = alpha * l_ref[sub] + jnp.sum(p, axis=-1, keepdims=True)
            acc_ref[sub] = alpha * acc_ref[sub] + _dot(p.astype(BF16), v)
            m_ref[sub] = m_new

    def body(j, carry):
        tile(j, False)
        return carry

    lax.fori_loop(0, i, body, 0)
    tile(i, True)

    lam_p = lam_ref[...]
    lam = (jnp.exp(jnp.sum(lam_p[0:1] * lam_p[1:2], axis=-1, keepdims=True))
           - jnp.exp(jnp.sum(lam_p[2:3] * lam_p[3:4], axis=-1, keepdims=True)) + lambda_init)
    o = acc_ref[0] / l_ref[0] - lam * (acc_ref[1] / l_ref[1])
    o = o * _rms_scale(o) * sw_ref[...] * (1.0 - lambda_init)
    o_ref[...] = o.astype(o_ref.dtype)


def _diff_attn(proj3, qw2, kw2, lam_p, sw, lambda_init, q_col, k_col, v_col, tq=256):
    b, s, _ = proj3.shape
    kern = functools.partial(_diff_attn_kernel, tq=tq, lambda_init=lambda_init)
    vec = lambda shape: pl.BlockSpec(shape, lambda bi, h, i: (0, 0))
    return pl.pallas_call(
        kern,
        out_shape=jax.ShapeDtypeStruct((b, s, DA_HEADS * DA_V_DIM), BF16),
        grid=(b, DA_HEADS, s // tq),
        in_specs=[
            pl.BlockSpec((None, tq, DA_V_DIM), lambda bi, h, i: (bi, i, q_col + h)),
            pl.BlockSpec((None, s, DA_V_DIM), lambda bi, h, i: (bi, 0, k_col + h)),
            pl.BlockSpec((None, s, DA_V_DIM), lambda bi, h, i: (bi, 0, v_col + h)),
            vec((1, DA_V_DIM)), vec((1, DA_V_DIM)), vec((4, DA_HEAD_DIM)), vec((1, DA_V_DIM)),
        ],
        out_specs=pl.BlockSpec((None, tq, DA_V_DIM), lambda bi, h, i: (bi, i, h)),
        scratch_shapes=[
            pltpu.VMEM((s, DA_V_DIM), BF16),
            pltpu.VMEM((2, tq, 1), F32),
            pltpu.VMEM((2, tq, 1), F32),
            pltpu.VMEM((2, tq, DA_V_DIM), F32),
        ],
        compiler_params=pltpu.CompilerParams(
            dimension_semantics=("parallel", "parallel", "arbitrary"), vmem_limit_bytes=VMEM_LIMIT),
        name="diff_attn",
    )(proj3, proj3, proj3, qw2, kw2, lam_p, sw)


def _ssd_kernel(z_ref, xs_ref, bc_ref, dt_ref, cwx_ref, cbx_ref, cwbc_ref, cbbc_ref,
                dtb_ref, alog_ref, dx_ref, nw_ref, e_ref, o_ref,
                state_ref, tailx_ref, tailbc_ref, y_ref):
    L, N, G = SSD_CHUNK, SSD_STATE, SSD_GROUPS
    gw = o_ref.shape[-1] // G
    c = pl.program_id(1)

    @pl.when(c == 0)
    def _():
        state_ref[...] = jnp.zeros(state_ref.shape, F32)
        tailx_ref[...] = jnp.zeros(tailx_ref.shape, F32)
        tailbc_ref[...] = jnp.zeros(tailbc_ref.shape, F32)

    row8 = lax.broadcasted_iota(jnp.int32, (SUBLANES, 1), 0)

    def conv_silu(cur, tail, w_ref, b_ref):
        acc = cur * w_ref[SSD_CONV - 1:SSD_CONV, :] + b_ref[...]
        for k in range(1, SSD_CONV):
            rolled = pltpu.roll(cur, k, 0)
            head = jnp.where(row8 < k, pltpu.roll(tail, k, 0), rolled[0:SUBLANES])
            shifted = jnp.concatenate([head, rolled[SUBLANES:]], axis=0)
            acc = acc + shifted * w_ref[SSD_CONV - 1 - k:SSD_CONV - k, :]
        return _silu(acc)

    xs_raw = xs_ref[...].astype(F32)
    bc_raw = bc_ref[...].astype(F32)
    xs_c = conv_silu(xs_raw, tailx_ref[...], cwx_ref, cbx_ref)
    bc_c = conv_silu(bc_raw, tailbc_ref[...], cwbc_ref, cbbc_ref)
    tailx_ref[...] = xs_raw[L - SUBLANES:, :]
    tailbc_ref[...] = bc_raw[L - SUBLANES:, :]

    dtx = dt_ref[...] + dtb_ref[...]
    dt = jnp.maximum(dtx, 0.0) + jnp.log1p(jnp.exp(-jnp.abs(dtx)))
    a = -jnp.exp(alog_ref[...])
    da = dt * a
    r_i = lax.broadcasted_iota(jnp.int32, (L, L), 0)
    c_i = lax.broadcasted_iota(jnp.int32, (L, L), 1)
    causal = c_i <= r_i
    tril = jnp.where(causal, 1.0, 0.0).astype(BF16)
    acs = sum(_dot(tril, p) for p in _split_bf16(da, 3))
    acs_t = acs.T

    e = e_ref[...]
    dt_x = sum(_dot(p, e) for p in _split_bf16(dt, 2))
    acs_x = sum(_dot(p, e) for p in _split_bf16(acs, 3))
    last_x = acs_x[L - 1:L, :]
    xdt_f = xs_c * dt_x
    xdt = xdt_f.astype(BF16)
    xds = (xdt_f * jnp.exp(last_x - acs_x)).astype(BF16)
    expacs_x = jnp.exp(acs_x)
    cdecay_x = jnp.exp(last_x)

    lane = lax.broadcasted_iota(jnp.int32, (1, LANES), 1)
    first = lane < SSD_HEAD_DIM
    heads_per_group = gw // SSD_HEAD_DIM

    for g in range(G):
        cols = slice(g * gw, (g + 1) * gw)
        b_g = bc_c[:, g * N:(g + 1) * N]
        c_g = bc_c[:, (G + g) * N:(G + g + 1) * N].astype(BF16)
        cb = _dot_nt(c_g, b_g.astype(BF16))
        state_old = state_ref[:, cols]
        y_off = _dot(c_g, state_old.astype(BF16)) * expacs_x[:, cols]
        state_ref[:, cols] = state_old * cdecay_x[:, cols] + _dot(b_g.T.astype(BF16), xds[:, cols])
        for pair in range(heads_per_group // 2):
            h0 = g * heads_per_group + 2 * pair
            pc = slice(g * gw + pair * LANES, g * gw + (pair + 1) * LANES)
            xp = xdt[:, pc]
            res = []
            for h in (h0, h0 + 1):
                seg = acs[:, h:h + 1] - acs_t[h:h + 1, :]
                dec = jnp.exp(jnp.where(causal, seg, -jnp.inf))
                res.append(_dot((cb * dec).astype(BF16), xp))
            y_ref[:, pc] = jnp.where(first, res[0], res[1])
        y = y_ref[:, cols] + y_off + xs_c[:, cols] * dx_ref[:, cols]
        y = y * _silu(z_ref[:, cols].astype(F32))
        o_ref[:, cols] = (y * _rms_scale(y) * nw_ref[:, cols]).astype(o_ref.dtype)


def _ssd(proj3, dt3, cwx, cbx, cwbc, cbbc, dtb, alog, dx, nw, e, z_col, xs_col, bc_col):
    b, s, _ = proj3.shape
    w = e.shape[1]
    bcw = cwbc.shape[1]
    L = SSD_CHUNK
    full = lambda shape: pl.BlockSpec(shape, lambda bi, c: (0, 0))
    return pl.pallas_call(
        _ssd_kernel,
        out_shape=jax.ShapeDtypeStruct((b, s, w), BF16),
        grid=(b, s // L),
        in_specs=[
            pl.BlockSpec((None, L, w), lambda bi, c: (bi, c, z_col)),
            pl.BlockSpec((None, L, w), lambda bi, c: (bi, c, xs_col)),
            pl.BlockSpec((None, L, bcw), lambda bi, c: (bi, c, bc_col)),
            pl.BlockSpec((None, L, LANES), lambda bi, c: (bi, c, 0)),
            full((SSD_CONV, w)), full((1, w)), full((SSD_CONV, bcw)), full((1, bcw)),
            full((1, LANES)), full((1, LANES)), full((1, w)), full((1, w)), full((LANES, w)),
        ],
        out_specs=pl.BlockSpec((None, L, w), lambda bi, c: (bi, c, 0)),
        scratch_shapes=[
            pltpu.VMEM((SSD_STATE, w), F32),
            pltpu.VMEM((SUBLANES, w), F32),
            pltpu.VMEM((SUBLANES, bcw), F32),
            pltpu.VMEM((L, w), F32),
        ],
        compiler_params=pltpu.CompilerParams(
            dimension_semantics=("parallel", "arbitrary"), vmem_limit_bytes=VMEM_LIMIT),
        name="ssd",
    )(proj3, proj3, proj3, dt3, cwx, cbx, cwbc, cbbc, dtb, alog, dx, nw, e)


def _out_proj_kernel(a_ref, s_ref, x_ref, wa_ref, ws_ref, o_ref):
    o_ref[...] = x_ref[...] + _dot(a_ref[...], wa_ref[...]) + _dot(s_ref[...], ws_ref[...])


def _out_proj(a2, s2, x2, wa, ws, tm=512, tn=1024):
    m, ka = a2.shape
    ks = s2.shape[1]
    n = wa.shape[1]
    return pl.pallas_call(
        _out_proj_kernel,
        out_shape=jax.ShapeDtypeStruct((m, n), F32),
        grid=(m // tm, n // tn),
        in_specs=[
            pl.BlockSpec((tm, ka), lambda i, j: (i, 0)),
            pl.BlockSpec((tm, ks), lambda i, j: (i, 0)),
            pl.BlockSpec((tm, tn), lambda i, j: (i, j)),
            pl.BlockSpec((ka, tn), lambda i, j: (0, j)),
            pl.BlockSpec((ks, tn), lambda i, j: (0, j)),
        ],
        out_specs=pl.BlockSpec((tm, tn), lambda i, j: (i, j)),
        compiler_params=pltpu.CompilerParams(
            dimension_semantics=("parallel", "parallel"), vmem_limit_bytes=VMEM_LIMIT),
        name="out_proj",
    )(a2, s2, x2, wa, ws)


def _xattn_kernel(q_ref, kv_ref, h_ref, qw_ref, kw_ref, wo_ref, o_ref):
    d = X_HEAD_DIM
    outs = []
    for hh in range(X_HEADS):
        qf = q_ref[:, hh * d:(hh + 1) * d].astype(F32)
        kf = kv_ref[:, hh * d:(hh + 1) * d].astype(F32)
        v = kv_ref[:, (X_HEADS + hh) * d:(X_HEADS + hh + 1) * d]
        qn = (qf * _rms_scale(qf) * qw_ref[...] * (d ** -0.5)).astype(BF16)
        kn = (kf * _rms_scale(kf) * kw_ref[...]).astype(BF16)
        s = _dot_nt(qn, kn)
        p = jnp.exp(s - jnp.max(s, axis=-1, keepdims=True))
        o = _dot(p.astype(BF16), v) / jnp.sum(p, axis=-1, keepdims=True)
        outs.append(o.astype(BF16))
    o_all = jnp.concatenate(outs, axis=-1)
    o_ref[...] = h_ref[...] + _dot(o_all, wo_ref[...])


def _xattn(q3, kv3, h3, qw, kw, wo, tq=512):
    b, s, xw = q3.shape
    mlen = kv3.shape[1]
    dm = h3.shape[-1]
    return pl.pallas_call(
        _xattn_kernel,
        out_shape=jax.ShapeDtypeStruct((b, s, dm), F32),
        grid=(b, s // tq),
        in_specs=[
            pl.BlockSpec((None, tq, xw), lambda bi, i: (bi, i, 0)),
            pl.BlockSpec((None, mlen, 2 * xw), lambda bi, i: (bi, 0, 0)),
            pl.BlockSpec((None, tq, dm), lambda bi, i: (bi, i, 0)),
            pl.BlockSpec((1, X_HEAD_DIM), lambda bi, i: (0, 0)),
            pl.BlockSpec((1, X_HEAD_DIM), lambda bi, i: (0, 0)),
            pl.BlockSpec((xw, dm), lambda bi, i: (0, 0)),
        ],
        out_specs=pl.BlockSpec((None, tq, dm), lambda bi, i: (bi, i, 0)),
        compiler_params=pltpu.CompilerParams(
            dimension_semantics=("parallel", "parallel"), vmem_limit_bytes=VMEM_LIMIT),
        name="xattn",
    )(q3, kv3, h3, qw, kw, wo)


def _ffn_kernel(h_ref, nw_ref, wg_ref, wu_ref, wd_ref, o_ref, hn_ref):
    f = pl.program_id(1)

    @pl.when(f == 0)
    def _():
        hf = h_ref[...]
        hn_ref[...] = (hf * _rms_scale(hf) * nw_ref[...]).astype(BF16)
        o_ref[...] = hf

    hn = hn_ref[...]
    act = (_silu(_dot(hn, wg_ref[...])) * _dot(hn, wu_ref[...])).astype(BF16)
    o_ref[...] += _dot(act, wd_ref[...])


def _ffn(h2, nw, wg, wu, wd, tm=512, tf=512):
    m, k = h2.shape
    dff = wg.shape[1]
    return pl.pallas_call(
        _ffn_kernel,
        out_shape=jax.ShapeDtypeStruct((m, k), F32),
        grid=(m // tm, dff // tf),
        in_specs=[
            pl.BlockSpec((tm, k), lambda i, f: (i, 0)),
            pl.BlockSpec((1, k), lambda i, f: (0, 0)),
            pl.BlockSpec((k, tf), lambda i, f: (0, f)),
            pl.BlockSpec((k, tf), lambda i, f: (0, f)),
            pl.BlockSpec((tf, k), lambda i, f: (f, 0)),
        ],
        out_specs=pl.BlockSpec((tm, k), lambda i, f: (i, 0)),
        scratch_shapes=[pltpu.VMEM((tm, k), BF16)],
        compiler_params=pltpu.CompilerParams(
            dimension_semantics=("parallel", "arbitrary"), vmem_limit_bytes=VMEM_LIMIT),
        name="ffn",
    )(h2, nw, wg, wu, wd)


def _pad_lanes(v, fill=0.0):
    return jnp.pad(v, ((0, 0), (0, LANES - v.shape[1])), constant_values=fill)


def kernel(x, mem, mix_norm_w, w_in, da_q_norm_w, da_k_norm_w, da_lambda_q1, da_lambda_k1, da_lambda_q2, da_lambda_k2, da_subln_w, ssd_conv_w, ssd_conv_b, ssd_dt_bias, ssd_a_log, ssd_d, ssd_norm_w, w_out, xattn_norm_w, mem_norm_w, xattn_w_q, xattn_w_kv, xattn_q_norm_w, xattn_k_norm_w, xattn_w_o, ffn_norm_w, ffn_w_gate, ffn_w_up, ffn_w_down):
    b, s, dm = x.shape
    mlen = mem.shape[1]
    depth = w_in.shape[0]
    da_w = DA_HEADS * DA_V_DIM
    ssd_w = ssd_norm_w.shape[1]
    ssd_heads = ssd_w // SSD_HEAD_DIM
    gn = SSD_GROUPS * SSD_STATE
    xw = X_HEADS * X_HEAD_DIM
    o_q, o_k, o_v, o_z = 0, da_w, 2 * da_w, 3 * da_w
    o_xs = o_z + ssd_w
    o_bc = o_xs + ssd_w
    o_dt = o_bc + 2 * gn
    z_col, xs_col = 0, 1
    bc_col = (2 * ssd_w) // (2 * gn)
    q_col = (2 * ssd_w + 2 * gn) // DA_V_DIM
    k_col = q_col + DA_HEADS
    v_col = k_col + DA_HEADS
    assert (2 * ssd_w) % (2 * gn) == 0 and ssd_heads <= LANES

    expand = np.zeros((LANES, ssd_w), np.float32)
    for h in range(ssd_heads):
        expand[h, h * SSD_HEAD_DIM:(h + 1) * SSD_HEAD_DIM] = 1.0
    expand = jnp.asarray(expand, BF16)

    h = x.reshape(b * s, dm)
    mem2 = mem.reshape(b * mlen, dm)
    for i in range(depth):
        lambda_init = 0.8 - 0.6 * math.exp(-0.3 * i)
        wi = w_in[i]
        w_main = jnp.concatenate(
            [wi[:, o_z:o_xs], wi[:, o_xs:o_bc], wi[:, o_bc:o_dt], wi[:, o_q:o_k], wi[:, o_k:o_v], wi[:, o_v:o_z]],
            axis=1).astype(BF16)
        w_dt = _pad_lanes(wi[:, o_dt:]).astype(BF16)
        proj, dt_raw = _in_proj(h, mix_norm_w[i][None], w_main, w_dt)
        proj3 = proj.reshape(b, s, -1)

        lam_p = jnp.stack([da_lambda_q1[i], da_lambda_k1[i], da_lambda_q2[i], da_lambda_k2[i]])
        a_out = _diff_attn(proj3, jnp.tile(da_q_norm_w[i], 2)[None], jnp.tile(da_k_norm_w[i], 2)[None],
                           lam_p, da_subln_w[i][None], lambda_init, q_col, k_col, v_col)

        cw, cb = ssd_conv_w[i], ssd_conv_b[i][None]
        s_out = _ssd(proj3, dt_raw.reshape(b, s, LANES),
                     cw[:, :ssd_w], cb[:, :ssd_w], cw[:, ssd_w:], cb[:, ssd_w:],
                     _pad_lanes(ssd_dt_bias[i][None]), _pad_lanes(ssd_a_log[i][None]),
                     jnp.repeat(ssd_d[i], SSD_HEAD_DIM)[None], ssd_norm_w[i][None], expand,
                     z_col, xs_col, bc_col)

        wo = w_out[i].astype(BF16)
        h1 = _out_proj(a_out.reshape(b * s, da_w), s_out.reshape(b * s, ssd_w), h, wo[:da_w], wo[da_w:])

        xq = _norm_matmul(h1, xattn_norm_w[i][None], xattn_w_q[i].astype(BF16), 512, "xq_proj")
        xkv = _norm_matmul(mem2, mem_norm_w[i][None], xattn_w_kv[i].astype(BF16), 256, "xkv_proj")
        h2 = _xattn(xq.reshape(b, s, xw), xkv.reshape(b, mlen, 2 * xw), h1.reshape(b, s, dm),
                    xattn_q_norm_w[i][None], xattn_k_norm_w[i][None], xattn_w_o[i].astype(BF16))

        h = _ffn(h2.reshape(b * s, dm), ffn_norm_w[i][None], ffn_w_gate[i].astype(BF16),
                 ffn_w_up[i].astype(BF16), ffn_w_down[i].astype(BF16))
    return h.reshape(b, s, dm)
```

```python
import functools
import math

import jax
import jax.numpy as jnp
import numpy as np
from jax import lax
from jax.experimental import pallas as pl
from jax.experimental.pallas import tpu as pltpu

EPS = 1e-6
LANES = 128
SUBLANES = 8
VMEM_LIMIT = 48 * 1024 * 1024

DA_HEADS = 8
DA_HEAD_DIM = 64
DA_V_DIM = 128
SSD_HEAD_DIM = 64
SSD_GROUPS = 4
SSD_STATE = 128
SSD_CONV = 4
SSD_CHUNK = 128
X_HEADS = 4
X_HEAD_DIM = 128

F32 = jnp.float32
BF16 = jnp.bfloat16


def _rms_scale(xf, eps=EPS):
    return lax.rsqrt(jnp.mean(xf * xf, axis=-1, keepdims=True) + eps)


def _silu(x):
    return x / (1.0 + jnp.exp(-x))


def _split_bf16(x, parts):
    out = []
    r = x
    for _ in range(parts):
        p = r.astype(BF16)
        out.append(p)
        r = r - p.astype(F32)
    return out


def _dot(a, b):
    return jnp.dot(a, b, preferred_element_type=F32)


def _dot_nt(a, b):
    return lax.dot_general(a, b, (((1,), (1,)), ((), ())), preferred_element_type=F32)


def _in_proj_kernel(x_ref, nw_ref, w_ref, wdt_ref, o_ref, dt_ref, hn_ref):
    n = pl.program_id(1)

    @pl.when(n == 0)
    def _():
        xf = x_ref[...]
        hn = (xf * _rms_scale(xf) * nw_ref[...]).astype(BF16)
        hn_ref[...] = hn
        dt_ref[...] = _dot(hn, wdt_ref[...])

    o_ref[...] = _dot(hn_ref[...], w_ref[...]).astype(o_ref.dtype)


def _in_proj(x2, nw, w, wdt, tm=512, tn=1024):
    m, k = x2.shape
    n = w.shape[1]
    return pl.pallas_call(
        _in_proj_kernel,
        out_shape=(jax.ShapeDtypeStruct((m, n), BF16), jax.ShapeDtypeStruct((m, LANES), F32)),
        grid=(m // tm, n // tn),
        in_specs=[
            pl.BlockSpec((tm, k), lambda i, j: (i, 0)),
            pl.BlockSpec((1, k), lambda i, j: (0, 0)),
            pl.BlockSpec((k, tn), lambda i, j: (0, j)),
            pl.BlockSpec((k, LANES), lambda i, j: (0, 0)),
        ],
        out_specs=(
            pl.BlockSpec((tm, tn), lambda i, j: (i, j)),
            pl.BlockSpec((tm, LANES), lambda i, j: (i, 0)),
        ),
        scratch_shapes=[pltpu.VMEM((tm, k), BF16)],
        compiler_params=pltpu.CompilerParams(
            dimension_semantics=("parallel", "arbitrary"), vmem_limit_bytes=VMEM_LIMIT),
        name="in_proj",
    )(x2, nw, w, wdt)


def _norm_matmul_kernel(x_ref, nw_ref, w_ref, o_ref):
    xf = x_ref[...]
    hn = (xf * _rms_scale(xf) * nw_ref[...]).astype(BF16)
    o_ref[...] = _dot(hn, w_ref[...]).astype(o_ref.dtype)


def _norm_matmul(x2, nw, w, tm, name):
    m, k = x2.shape
    n = w.shape[1]
    return pl.pallas_call(
        _norm_matmul_kernel,
        out_shape=jax.ShapeDtypeStruct((m, n), BF16),
        grid=(m // tm,),
        in_specs=[
            pl.BlockSpec((tm, k), lambda i: (i, 0)),
            pl.BlockSpec((1, k), lambda i: (0, 0)),
            pl.BlockSpec((k, n), lambda i: (0, 0)),
        ],
        out_specs=pl.BlockSpec((tm, n), lambda i: (i, 0)),
        compiler_params=pltpu.CompilerParams(
            dimension_semantics=("parallel",), vmem_limit_bytes=VMEM_LIMIT),
        name=name,
    )(x2, nw, w)


def _diff_attn_kernel(q_ref, k_ref, v_ref, qw_ref, kw_ref, lam_ref, sw_ref, o_ref,
                      kn_ref, vt_ref, qcat_ref, m_ref, l_ref, acc_ref, s_ref, *, tq, hb, lambda_init):
    i = pl.program_id(2)
    n_tiles = vt_ref.shape[1]
    d = DA_V_DIM
    lane = lax.broadcasted_iota(jnp.int32, (1, d), 1)
    first = lane < DA_HEAD_DIM

    def sub_head_norm(xf):
        sq = xf * xf
        s1 = jnp.sum(jnp.where(first, sq, 0.0), axis=-1, keepdims=True)
        s2 = jnp.sum(jnp.where(first, 0.0, sq), axis=-1, keepdims=True)
        r1 = lax.rsqrt(s1 * (1.0 / DA_HEAD_DIM) + EPS)
        r2 = lax.rsqrt(s2 * (1.0 / DA_HEAD_DIM) + EPS)
        return xf * jnp.where(first, r1, r2)

    @pl.when(i == 0)
    def _():
        for hh in range(hb):
            hs = slice(hh * d, (hh + 1) * d)
            kn_ref[hh] = (sub_head_norm(k_ref[:, hs].astype(F32)) * kw_ref[...]).astype(BF16)
            for t in range(n_tiles):
                vt_ref[hh, t] = v_ref[t * tq:(t + 1) * tq, hs].astype(F32).T.astype(BF16)

    for hh in range(hb):
        qn = sub_head_norm(q_ref[:, hh * d:(hh + 1) * d].astype(F32)) * qw_ref[...] * (DA_HEAD_DIM ** -0.5)
        qcat_ref[hh] = jnp.concatenate(
            [jnp.where(first, qn, 0.0), jnp.where(first, 0.0, qn)], axis=0).astype(BF16)
    m_ref[...] = jnp.full(m_ref.shape, -jnp.inf, F32)
    l_ref[...] = jnp.zeros(l_ref.shape, F32)
    acc_ref[...] = jnp.zeros(acc_ref.shape, F32)

    def tile(j, masked):
        start = pl.multiple_of(j * tq, tq)
        if masked:
            key = lax.broadcasted_iota(jnp.int32, (tq, 2 * tq), 0)
            qry = lax.broadcasted_iota(jnp.int32, (tq, 2 * tq), 1)
            causal = key <= jnp.where(qry >= tq, qry - tq, qry)
        for hh in range(hb):
            s_ref[hh] = _dot_nt(kn_ref[hh, pl.ds(start, tq), :], qcat_ref[hh])
        for hh in range(hb):
            s = s_ref[hh]
            if masked:
                s = jnp.where(causal, s, -jnp.inf)
            m_prev = m_ref[hh]
            m_new = jnp.maximum(m_prev, jnp.max(s, axis=0, keepdims=True))
            alpha = jnp.exp(m_prev - m_new)
            p = jnp.exp(s - m_new)
            l_ref[hh] = alpha * l_ref[hh] + jnp.sum(p, axis=0, keepdims=True)
            acc_ref[hh] = alpha * acc_ref[hh] + _dot(vt_ref[hh, j], p.astype(BF16))
            m_ref[hh] = m_new

    def body(j, carry):
        tile(j, False)
        return carry

    lax.fori_loop(0, i, body, 0)
    tile(i, True)

    lam_p = lam_ref[...]
    lam = (jnp.exp(jnp.sum(lam_p[0:1] * lam_p[1:2], axis=-1, keepdims=True))
           - jnp.exp(jnp.sum(lam_p[2:3] * lam_p[3:4], axis=-1, keepdims=True)) + lambda_init)
    for hh in range(hb):
        ot = acc_ref[hh] / l_ref[hh]
        o = (ot[:, :tq] - lam * ot[:, tq:]).T
        o = o * _rms_scale(o) * sw_ref[...] * (1.0 - lambda_init)
        o_ref[:, hh * d:(hh + 1) * d] = o.astype(o_ref.dtype)


def _diff_attn(proj3, qw2, kw2, lam_p, sw, lambda_init, q_col, k_col, v_col, tq=256, hb=4):
    b, s, _ = proj3.shape
    d = DA_V_DIM
    kern = functools.partial(_diff_attn_kernel, tq=tq, hb=hb, lambda_init=lambda_init)
    vec = lambda shape: pl.BlockSpec(shape, lambda bi, h, i: (0, 0))
    return pl.pallas_call(
        kern,
        out_shape=jax.ShapeDtypeStruct((b, s, DA_HEADS * d), BF16),
        grid=(b, DA_HEADS // hb, s // tq),
        in_specs=[
            pl.BlockSpec((None, tq, hb * d), lambda bi, h, i: (bi, i, q_col // hb + h)),
            pl.BlockSpec((None, s, hb * d), lambda bi, h, i: (bi, 0, k_col // hb + h)),
            pl.BlockSpec((None, s, hb * d), lambda bi, h, i: (bi, 0, v_col // hb + h)),
            vec((1, d)), vec((1, d)), vec((4, DA_HEAD_DIM)), vec((1, d)),
        ],
        out_specs=pl.BlockSpec((None, tq, hb * d), lambda bi, h, i: (bi, i, h)),
        scratch_shapes=[
            pltpu.VMEM((hb, s, d), BF16),
            pltpu.VMEM((hb, s // tq, d, tq), BF16),
            pltpu.VMEM((hb, 2 * tq, d), BF16),
            pltpu.VMEM((hb, 1, 2 * tq), F32),
            pltpu.VMEM((hb, 1, 2 * tq), F32),
            pltpu.VMEM((hb, d, 2 * tq), F32),
            pltpu.VMEM((hb, tq, 2 * tq), F32),
        ],
        compiler_params=pltpu.CompilerParams(
            dimension_semantics=("parallel", "parallel", "arbitrary"), vmem_limit_bytes=VMEM_LIMIT),
        name="diff_attn",
    )(proj3, proj3, proj3, qw2, kw2, lam_p, sw)


def _ssd_kernel(z_ref, xs_ref, bc_ref, dt_ref, cwx_ref, cbx_ref, cwbc_ref, cbbc_ref,
                dtb_ref, alog_ref, dx_ref, nw_ref, e_ref, o_ref,
                state_ref, tailx_ref, tailbc_ref, y_ref):
    L, N, G = SSD_CHUNK, SSD_STATE, SSD_GROUPS
    gw = o_ref.shape[-1] // G
    c = pl.program_id(1)

    @pl.when(c == 0)
    def _():
        state_ref[...] = jnp.zeros(state_ref.shape, F32)
        tailx_ref[...] = jnp.zeros(tailx_ref.shape, F32)
        tailbc_ref[...] = jnp.zeros(tailbc_ref.shape, F32)

    row8 = lax.broadcasted_iota(jnp.int32, (SUBLANES, 1), 0)

    def conv_silu(cur, tail, w_ref, b_ref):
        acc = cur * w_ref[SSD_CONV - 1:SSD_CONV, :] + b_ref[...]
        for k in range(1, SSD_CONV):
            rolled = pltpu.roll(cur, k, 0)
            head = jnp.where(row8 < k, pltpu.roll(tail, k, 0), rolled[0:SUBLANES])
            shifted = jnp.concatenate([head, rolled[SUBLANES:]], axis=0)
            acc = acc + shifted * w_ref[SSD_CONV - 1 - k:SSD_CONV - k, :]
        return _silu(acc)

    xs_raw = xs_ref[...].astype(F32)
    bc_raw = bc_ref[...].astype(F32)
    xs_c = conv_silu(xs_raw, tailx_ref[...], cwx_ref, cbx_ref)
    bc_c = conv_silu(bc_raw, tailbc_ref[...], cwbc_ref, cbbc_ref)
    tailx_ref[...] = xs_raw[L - SUBLANES:, :]
    tailbc_ref[...] = bc_raw[L - SUBLANES:, :]

    dtx = dt_ref[...] + dtb_ref[...]
    dt = jnp.maximum(dtx, 0.0) + jnp.log1p(jnp.exp(-jnp.abs(dtx)))
    a = -jnp.exp(alog_ref[...])
    da = dt * a
    r_i = lax.broadcasted_iota(jnp.int32, (L, L), 0)
    c_i = lax.broadcasted_iota(jnp.int32, (L, L), 1)
    causal = c_i <= r_i
    tril = jnp.where(causal, 1.0, 0.0).astype(BF16)
    acs = sum(_dot(tril, p) for p in _split_bf16(da, 3))
    acs_t = acs.T

    e = e_ref[...]
    dt_x = sum(_dot(p, e) for p in _split_bf16(dt, 2))
    acs_x = sum(_dot(p, e) for p in _split_bf16(acs, 3))
    last_x = acs_x[L - 1:L, :]
    xdt_f = xs_c * dt_x
    xdt = xdt_f.astype(BF16)
    xds = (xdt_f * jnp.exp(last_x - acs_x)).astype(BF16)
    expacs_x = jnp.exp(acs_x)
    cdecay_x = jnp.exp(last_x)

    lane = lax.broadcasted_iota(jnp.int32, (1, LANES), 1)
    first = lane < SSD_HEAD_DIM
    heads_per_group = gw // SSD_HEAD_DIM

    for g in range(G):
        cols = slice(g * gw, (g + 1) * gw)
        b_g = bc_c[:, g * N:(g + 1) * N]
        c_g = bc_c[:, (G + g) * N:(G + g + 1) * N].astype(BF16)
        cb = _dot_nt(c_g, b_g.astype(BF16))
        state_old = state_ref[:, cols]
        y_off = _dot(c_g, state_old.astype(BF16)) * expacs_x[:, cols]
        state_ref[:, cols] = state_old * cdecay_x[:, cols] + _dot(b_g.T.astype(BF16), xds[:, cols])
        for pair in range(heads_per_group // 2):
            h0 = g * heads_per_group + 2 * pair
            pc = slice(g * gw + pair * LANES, g * gw + (pair + 1) * LANES)
            xp = xdt[:, pc]
            res = []
            for h in (h0, h0 + 1):
                seg = acs[:, h:h + 1] - acs_t[h:h + 1, :]
                dec = jnp.exp(jnp.where(causal, seg, -jnp.inf))
                res.append(_dot((cb * dec).astype(BF16), xp))
            y_ref[:, pc] = jnp.where(first, res[0], res[1])
        y = y_ref[:, cols] + y_off + xs_c[:, cols] * dx_ref[:, cols]
        y = y * _silu(z_ref[:, cols].astype(F32))
        o_ref[:, cols] = (y * _rms_scale(y) * nw_ref[:, cols]).astype(o_ref.dtype)


def _ssd(proj3, dt3, cwx, cbx, cwbc, cbbc, dtb, alog, dx, nw, e, z_col, xs_col, bc_col):
    b, s, _ = proj3.shape
    w = e.shape[1]
    bcw = cwbc.shape[1]
    L = SSD_CHUNK
    full = lambda shape: pl.BlockSpec(shape, lambda bi, c: (0, 0))
    return pl.pallas_call(
        _ssd_kernel,
        out_shape=jax.ShapeDtypeStruct((b, s, w), BF16),
        grid=(b, s // L),
        in_specs=[
            pl.BlockSpec((None, L, w), lambda bi, c: (bi, c, z_col)),
            pl.BlockSpec((None, L, w), lambda bi, c: (bi, c, xs_col)),
            pl.BlockSpec((None, L, bcw), lambda bi, c: (bi, c, bc_col)),
            pl.BlockSpec((None, L, LANES), lambda bi, c: (bi, c, 0)),
            full((SSD_CONV, w)), full((1, w)), full((SSD_CONV, bcw)), full((1, bcw)),
            full((1, LANES)), full((1, LANES)), full((1, w)), full((1, w)), full((LANES, w)),
        ],
        out_specs=pl.BlockSpec((None, L, w), lambda bi, c: (bi, c, 0)),
        scratch_shapes=[
            pltpu.VMEM((SSD_STATE, w), F32),
            pltpu.VMEM((SUBLANES, w), F32),
            pltpu.VMEM((SUBLANES, bcw), F32),
            pltpu.VMEM((L, w), F32),
        ],
        compiler_params=pltpu.CompilerParams(
            dimension_semantics=("parallel", "arbitrary"), vmem_limit_bytes=VMEM_LIMIT),
        name="ssd",
    )(proj3, proj3, proj3, dt3, cwx, cbx, cwbc, cbbc, dtb, alog, dx, nw, e)


def _out_proj_kernel(a_ref, s_ref, x_ref, wa_ref, ws_ref, o_ref):
    o_ref[...] = x_ref[...] + _dot(a_ref[...], wa_ref[...]) + _dot(s_ref[...], ws_ref[...])


def _out_proj(a2, s2, x2, wa, ws, tm=512, tn=1024):
    m, ka = a2.shape
    ks = s2.shape[1]
    n = wa.shape[1]
    return pl.pallas_call(
        _out_proj_kernel,
        out_shape=jax.ShapeDtypeStruct((m, n), F32),
        grid=(m // tm, n // tn),
        in_specs=[
            pl.BlockSpec((tm, ka), lambda i, j: (i, 0)),
            pl.BlockSpec((tm, ks), lambda i, j: (i, 0)),
            pl.BlockSpec((tm, tn), lambda i, j: (i, j)),
            pl.BlockSpec((ka, tn), lambda i, j: (0, j)),
            pl.BlockSpec((ks, tn), lambda i, j: (0, j)),
        ],
        out_specs=pl.BlockSpec((tm, tn), lambda i, j: (i, j)),
        compiler_params=pltpu.CompilerParams(
            dimension_semantics=("parallel", "parallel"), vmem_limit_bytes=VMEM_LIMIT),
        name="out_proj",
    )(a2, s2, x2, wa, ws)


def _xattn_kernel(q_ref, kv_ref, h_ref, qw_ref, kw_ref, wo_ref, o_ref):
    d = X_HEAD_DIM
    outs = []
    for hh in range(X_HEADS):
        qf = q_ref[:, hh * d:(hh + 1) * d].astype(F32)
        kf = kv_ref[:, hh * d:(hh + 1) * d].astype(F32)
        v = kv_ref[:, (X_HEADS + hh) * d:(X_HEADS + hh + 1) * d]
        qn = (qf * _rms_scale(qf) * qw_ref[...] * (d ** -0.5)).astype(BF16)
        kn = (kf * _rms_scale(kf) * kw_ref[...]).astype(BF16)
        s = _dot_nt(qn, kn)
        p = jnp.exp(s - jnp.max(s, axis=-1, keepdims=True))
        o = _dot(p.astype(BF16), v) / jnp.sum(p, axis=-1, keepdims=True)
        outs.append(o.astype(BF16))
    o_all = jnp.concatenate(outs, axis=-1)
    o_ref[...] = h_ref[...] + _dot(o_all, wo_ref[...])


def _xattn(q3, kv3, h3, qw, kw, wo, tq=512):
    b, s, xw = q3.shape
    mlen = kv3.shape[1]
    dm = h3.shape[-1]
    return pl.pallas_call(
        _xattn_kernel,
        out_shape=jax.ShapeDtypeStruct((b, s, dm), F32),
        grid=(b, s // tq),
        in_specs=[
            pl.BlockSpec((None, tq, xw), lambda bi, i: (bi, i, 0)),
            pl.BlockSpec((None, mlen, 2 * xw), lambda bi, i: (bi, 0, 0)),
            pl.BlockSpec((None, tq, dm), lambda bi, i: (bi, i, 0)),
            pl.BlockSpec((1, X_HEAD_DIM), lambda bi, i: (0, 0)),
            pl.BlockSpec((1, X_HEAD_DIM), lambda bi, i: (0, 0)),
            pl.BlockSpec((xw, dm), lambda bi, i: (0, 0)),
        ],
        out_specs=pl.BlockSpec((None, tq, dm), lambda bi, i: (bi, i, 0)),
        compiler_params=pltpu.CompilerParams(
            dimension_semantics=("parallel", "parallel"), vmem_limit_bytes=VMEM_LIMIT),
        name="xattn",
    )(q3, kv3, h3, qw, kw, wo)


def _ffn_kernel(h_ref, nw_ref, wg_ref, wu_ref, wd_ref, o_ref, hn_ref):
    f = pl.program_id(1)

    @pl.when(f == 0)
    def _():
        hf = h_ref[...]
        hn_ref[...] = (hf * _rms_scale(hf) * nw_ref[...]).astype(BF16)
        o_ref[...] = hf

    hn = hn_ref[...]
    act = (_silu(_dot(hn, wg_ref[...])) * _dot(hn, wu_ref[...])).astype(BF16)
    o_ref[...] += _dot(act, wd_ref[...])


def _ffn(h2, nw, wg, wu, wd, tm=512, tf=512):
    m, k = h2.shape
    dff = wg.shape[1]
    return pl.pallas_call(
        _ffn_kernel,
        out_shape=jax.ShapeDtypeStruct((m, k), F32),
        grid=(m // tm, dff // tf),
        in_specs=[
            pl.BlockSpec((tm, k), lambda i, f: (i, 0)),
            pl.BlockSpec((1, k), lambda i, f: (0, 0)),
            pl.BlockSpec((k, tf), lambda i, f: (0, f)),
            pl.BlockSpec((k, tf), lambda i, f: (0, f)),
            pl.BlockSpec((tf, k), lambda i, f: (f, 0)),
        ],
        out_specs=pl.BlockSpec((tm, k), lambda i, f: (i, 0)),
        scratch_shapes=[pltpu.VMEM((tm, k), BF16)],
        compiler_params=pltpu.CompilerParams(
            dimension_semantics=("parallel", "arbitrary"), vmem_limit_bytes=VMEM_LIMIT),
        name="ffn",
    )(h2, nw, wg, wu, wd)


def _pad_lanes(v, fill=0.0):
    return jnp.pad(v, ((0, 0), (0, LANES - v.shape[1])), constant_values=fill)


def kernel(x, mem, mix_norm_w, w_in, da_q_norm_w, da_k_norm_w, da_lambda_q1, da_lambda_k1, da_lambda_q2, da_lambda_k2, da_subln_w, ssd_conv_w, ssd_conv_b, ssd_dt_bias, ssd_a_log, ssd_d, ssd_norm_w, w_out, xattn_norm_w, mem_norm_w, xattn_w_q, xattn_w_kv, xattn_q_norm_w, xattn_k_norm_w, xattn_w_o, ffn_norm_w, ffn_w_gate, ffn_w_up, ffn_w_down):
    b, s, dm = x.shape
    mlen = mem.shape[1]
    depth = w_in.shape[0]
    da_w = DA_HEADS * DA_V_DIM
    ssd_w = ssd_norm_w.shape[1]
    ssd_heads = ssd_w // SSD_HEAD_DIM
    gn = SSD_GROUPS * SSD_STATE
    xw = X_HEADS * X_HEAD_DIM
    o_q, o_k, o_v, o_z = 0, da_w, 2 * da_w, 3 * da_w
    o_xs = o_z + ssd_w
    o_bc = o_xs + ssd_w
    o_dt = o_bc + 2 * gn
    z_col, xs_col = 0, 1
    bc_col = (2 * ssd_w) // (2 * gn)
    q_col = (2 * ssd_w + 2 * gn) // DA_V_DIM
    k_col = q_col + DA_HEADS
    v_col = k_col + DA_HEADS
    assert (2 * ssd_w) % (2 * gn) == 0 and ssd_heads <= LANES

    expand = np.zeros((LANES, ssd_w), np.float32)
    for h in range(ssd_heads):
        expand[h, h * SSD_HEAD_DIM:(h + 1) * SSD_HEAD_DIM] = 1.0
    expand = jnp.asarray(expand, BF16)

    h = x.reshape(b * s, dm)
    mem2 = mem.reshape(b * mlen, dm)
    for i in range(depth):
        lambda_init = 0.8 - 0.6 * math.exp(-0.3 * i)
        wi = w_in[i]
        w_main = jnp.concatenate(
            [wi[:, o_z:o_xs], wi[:, o_xs:o_bc], wi[:, o_bc:o_dt], wi[:, o_q:o_k], wi[:, o_k:o_v], wi[:, o_v:o_z]],
            axis=1).astype(BF16)
        w_dt = _pad_lanes(wi[:, o_dt:]).astype(BF16)
        proj, dt_raw = _in_proj(h, mix_norm_w[i][None], w_main, w_dt)
        proj3 = proj.reshape(b, s, -1)

        lam_p = jnp.stack([da_lambda_q1[i], da_lambda_k1[i], da_lambda_q2[i], da_lambda_k2[i]])
        a_out = _diff_attn(proj3, jnp.tile(da_q_norm_w[i], 2)[None], jnp.tile(da_k_norm_w[i], 2)[None],
                           lam_p, da_subln_w[i][None], lambda_init, q_col, k_col, v_col)

        cw, cb = ssd_conv_w[i], ssd_conv_b[i][None]
        s_out = _ssd(proj3, dt_raw.reshape(b, s, LANES),
                     cw[:, :ssd_w], cb[:, :ssd_w], cw[:, ssd_w:], cb[:, ssd_w:],
                     _pad_lanes(ssd_dt_bias[i][None]), _pad_lanes(ssd_a_log[i][None]),
                     jnp.repeat(ssd_d[i], SSD_HEAD_DIM)[None], ssd_norm_w[i][None], expand,
                     z_col, xs_col, bc_col)

        wo = w_out[i].astype(BF16)
        h1 = _out_proj(a_out.reshape(b * s, da_w), s_out.reshape(b * s, ssd_w), h, wo[:da_w], wo[da_w:])

        xq = _norm_matmul(h1, xattn_norm_w[i][None], xattn_w_q[i].astype(BF16), 512, "xq_proj")
        xkv = _norm_matmul(mem2, mem_norm_w[i][None], xattn_w_kv[i].astype(BF16), 256, "xkv_proj")
        h2 = _xattn(xq.reshape(b, s, xw), xkv.reshape(b, mlen, 2 * xw), h1.reshape(b, s, dm),
                    xattn_q_norm_w[i][None], xattn_k_norm_w[i][None], xattn_w_o[i].astype(BF16))

        h = _ffn(h2.reshape(b * s, dm), ffn_norm_w[i][None], ffn_w_gate[i].astype(BF16),
                 ffn_w_up[i].astype(BF16), ffn_w_down[i].astype(BF16))
    return h.reshape(b, s, dm)
```

```python
import functools
import math

import jax
import jax.numpy as jnp
import numpy as np
from jax import lax
from jax.experimental import pallas as pl
from jax.experimental.pallas import tpu as pltpu

EPS = 1e-6
LANES = 128
SUBLANES = 8
VMEM_LIMIT = 48 * 1024 * 1024
FFN_VMEM_LIMIT = 56 * 1024 * 1024
LOG2E = math.log2(math.e)
DEN_ROWS = 16

DA_HEADS = 8
DA_HEAD_DIM = 64
DA_V_DIM = 128
SSD_HEAD_DIM = 64
SSD_GROUPS = 4
SSD_STATE = 128
SSD_CONV = 4
SSD_CHUNK = 128
X_HEADS = 4
X_HEAD_DIM = 128

F32 = jnp.float32
BF16 = jnp.bfloat16


def _rms_scale(xf, eps=EPS):
    return lax.rsqrt(jnp.mean(xf * xf, axis=-1, keepdims=True) + eps)


def _silu(x):
    h = 0.5 * x
    return h + h * jnp.tanh(h)


def _split_bf16(x, parts):
    out = []
    r = x
    for _ in range(parts):
        p = r.astype(BF16)
        out.append(p)
        r = r - p.astype(F32)
    return out


def _dot(a, b):
    return jnp.dot(a, b, preferred_element_type=F32)


def _dot_nt(a, b):
    return lax.dot_general(a, b, (((1,), (1,)), ((), ())), preferred_element_type=F32)


def _in_proj_kernel(x_ref, nw_ref, w_ref, wdt_ref, o_ref, dt_ref, hn_ref):
    n = pl.program_id(1)

    @pl.when(n == 0)
    def _():
        xf = x_ref[...]
        hn = (xf * _rms_scale(xf) * nw_ref[...]).astype(BF16)
        hn_ref[...] = hn
        dt_ref[...] = _dot(hn, wdt_ref[...])

    o_ref[...] = _dot(hn_ref[...], w_ref[...].astype(BF16)).astype(o_ref.dtype)


def _in_proj(x2, nw, w_all, layer, wdt, n, col_shift, tm=1024, tn=1024):
    m, k = x2.shape
    shift, nblk = col_shift // tn, n // tn
    assert col_shift % tn == 0 and n % tn == 0
    return pl.pallas_call(
        _in_proj_kernel,
        out_shape=(jax.ShapeDtypeStruct((m, n), BF16), jax.ShapeDtypeStruct((m, LANES), F32)),
        grid=(m // tm, nblk),
        in_specs=[
            pl.BlockSpec((tm, k), lambda i, j: (i, 0)),
            pl.BlockSpec((1, k), lambda i, j: (0, 0)),
            pl.BlockSpec((None, k, tn), lambda i, j: (layer, 0, (j + shift) % nblk)),
            pl.BlockSpec((k, LANES), lambda i, j: (0, 0)),
        ],
        out_specs=(
            pl.BlockSpec((tm, tn), lambda i, j: (i, j)),
            pl.BlockSpec((tm, LANES), lambda i, j: (i, 0)),
        ),
        scratch_shapes=[pltpu.VMEM((tm, k), BF16)],
        compiler_params=pltpu.CompilerParams(
            dimension_semantics=("parallel", "arbitrary"), vmem_limit_bytes=VMEM_LIMIT),
        name="in_proj",
    )(x2, nw, w_all, wdt)


def _norm_matmul_kernel(x_ref, nw_ref, w_ref, o_ref):
    xf = x_ref[...]
    hn = (xf * _rms_scale(xf) * nw_ref[...]).astype(BF16)
    o_ref[...] = _dot(hn, w_ref[...].astype(BF16)).astype(o_ref.dtype)


def _norm_matmul(x2, nw, w_all, layer, tm, name):
    m, k = x2.shape
    n = w_all.shape[2]
    return pl.pallas_call(
        _norm_matmul_kernel,
        out_shape=jax.ShapeDtypeStruct((m, n), BF16),
        grid=(m // tm,),
        in_specs=[
            pl.BlockSpec((tm, k), lambda i: (i, 0)),
            pl.BlockSpec((1, k), lambda i: (0, 0)),
            pl.BlockSpec((None, k, n), lambda i: (layer, 0, 0)),
        ],
        out_specs=pl.BlockSpec((tm, n), lambda i: (i, 0)),
        compiler_params=pltpu.CompilerParams(
            dimension_semantics=("parallel",), vmem_limit_bytes=VMEM_LIMIT),
        name=name,
    )(x2, nw, w_all)


def _diff_attn_kernel(q_ref, k_ref, v_ref, qw_ref, kw_ref, lam_ref, sw_ref, o_ref,
                      kn_ref, vt_ref, qcat_ref, m_ref, acc_ref, s_ref, *, tq, hb, lambda_init):
    i = pl.program_id(2)
    n_tiles = vt_ref.shape[1]
    d = DA_V_DIM
    lane = lax.broadcasted_iota(jnp.int32, (1, d), 1)
    first = lane < DA_HEAD_DIM
    ones_row = lax.broadcasted_iota(jnp.int32, (DEN_ROWS, tq), 0) == 0

    def sub_head_norm(xf):
        sq = xf * xf
        s1 = jnp.sum(jnp.where(first, sq, 0.0), axis=-1, keepdims=True)
        s2 = jnp.sum(jnp.where(first, 0.0, sq), axis=-1, keepdims=True)
        r1 = lax.rsqrt(s1 * (1.0 / DA_HEAD_DIM) + EPS)
        r2 = lax.rsqrt(s2 * (1.0 / DA_HEAD_DIM) + EPS)
        return xf * jnp.where(first, r1, r2)

    @pl.when(i == 0)
    def _():
        for hh in range(hb):
            hs = slice(hh * d, (hh + 1) * d)
            kn_ref[hh] = (sub_head_norm(k_ref[:, hs].astype(F32)) * kw_ref[...]).astype(BF16)
            for t in range(n_tiles):
                vt_ref[hh, t, :d] = v_ref[t * tq:(t + 1) * tq, hs].astype(F32).T.astype(BF16)
                vt_ref[hh, t, d:] = jnp.where(ones_row, 1.0, 0.0).astype(BF16)

    for hh in range(hb):
        qn = sub_head_norm(q_ref[:, hh * d:(hh + 1) * d].astype(F32)) * qw_ref[...] * (DA_HEAD_DIM ** -0.5 * LOG2E)
        qcat_ref[hh] = jnp.concatenate(
            [jnp.where(first, qn, 0.0), jnp.where(first, 0.0, qn)], axis=0).astype(BF16)
    m_ref[...] = jnp.full(m_ref.shape, -jnp.inf, F32)
    acc_ref[...] = jnp.zeros(acc_ref.shape, F32)

    def tile(j, masked):
        start = pl.multiple_of(j * tq, tq)
        if masked:
            key = lax.broadcasted_iota(jnp.int32, (tq, 2 * tq), 0)
            qry = lax.broadcasted_iota(jnp.int32, (tq, 2 * tq), 1)
            causal = key <= jnp.where(qry >= tq, qry - tq, qry)
        for hh in range(hb):
            s_ref[hh] = _dot_nt(kn_ref[hh, pl.ds(start, tq), :], qcat_ref[hh])
        for hh in range(hb):
            s = s_ref[hh]
            if masked:
                s = jnp.where(causal, s, -jnp.inf)
            m_prev = m_ref[hh]
            m_new = jnp.maximum(m_prev, jnp.max(s, axis=0, keepdims=True))
            alpha = jnp.exp2(m_prev - m_new)
            p = jnp.exp2(s - m_new)
            acc_ref[hh] = alpha * acc_ref[hh] + _dot(vt_ref[hh, j], p.astype(BF16))
            m_ref[hh] = m_new

    def body(j, carry):
        tile(j, False)
        return carry

    lax.fori_loop(0, i, body, 0)
    tile(i, True)

    lam_p = lam_ref[...]
    lam = (jnp.exp(jnp.sum(lam_p[0:1] * lam_p[1:2], axis=-1, keepdims=True))
           - jnp.exp(jnp.sum(lam_p[2:3] * lam_p[3:4], axis=-1, keepdims=True)) + lambda_init)
    for hh in range(hb):
        ot = acc_ref[hh, :d] / acc_ref[hh, d:d + 1]
        o = (ot[:, :tq] - lam * ot[:, tq:]).T
        o = o * _rms_scale(o) * sw_ref[...] * (1.0 - lambda_init)
        o_ref[:, hh * d:(hh + 1) * d] = o.astype(o_ref.dtype)


def _diff_attn(proj3, qw2, kw2, lam_p, sw, lambda_init, q_col, k_col, v_col, tq=256, hb=8):
    b, s, _ = proj3.shape
    d = DA_V_DIM
    kern = functools.partial(_diff_attn_kernel, tq=tq, hb=hb, lambda_init=lambda_init)
    vec = lambda shape: pl.BlockSpec(shape, lambda bi, h, i: (0, 0))
    return pl.pallas_call(
        kern,
        out_shape=jax.ShapeDtypeStruct((b, s, DA_HEADS * d), BF16),
        grid=(b, DA_HEADS // hb, s // tq),
        in_specs=[
            pl.BlockSpec((None, tq, hb * d), lambda bi, h, i: (bi, i, q_col // hb + h)),
            pl.BlockSpec((None, s, hb * d), lambda bi, h, i: (bi, 0, k_col // hb + h)),
            pl.BlockSpec((None, s, hb * d), lambda bi, h, i: (bi, 0, v_col // hb + h)),
            vec((1, d)), vec((1, d)), vec((4, DA_HEAD_DIM)), vec((1, d)),
        ],
        out_specs=pl.BlockSpec((None, tq, hb * d), lambda bi, h, i: (bi, i, h)),
        scratch_shapes=[
            pltpu.VMEM((hb, s, d), BF16),
            pltpu.VMEM((hb, s // tq, d + DEN_ROWS, tq), BF16),
            pltpu.VMEM((hb, 2 * tq, d), BF16),
            pltpu.VMEM((hb, 1, 2 * tq), F32),
            pltpu.VMEM((hb, d + DEN_ROWS, 2 * tq), F32),
            pltpu.VMEM((hb, tq, 2 * tq), F32),
        ],
        compiler_params=pltpu.CompilerParams(
            dimension_semantics=("parallel", "parallel", "arbitrary"), vmem_limit_bytes=VMEM_LIMIT),
        name="diff_attn",
    )(proj3, proj3, proj3, qw2, kw2, lam_p, sw)


def _ssd_kernel(z_ref, xs_ref, bc_ref, dt_ref, cwx_ref, cbx_ref, cwbc_ref, cbbc_ref,
                dtb_ref, alog_ref, dx_ref, nw_ref, e_ref, o_ref,
                state_ref, tailx_ref, tailbc_ref, y_ref):
    L, N, G = SSD_CHUNK, SSD_STATE, SSD_GROUPS
    gw = o_ref.shape[-1] // G
    c = pl.program_id(1)

    @pl.when(c == 0)
    def _():
        state_ref[...] = jnp.zeros(state_ref.shape, F32)
        tailx_ref[...] = jnp.zeros(tailx_ref.shape, F32)
        tailbc_ref[...] = jnp.zeros(tailbc_ref.shape, F32)

    row8 = lax.broadcasted_iota(jnp.int32, (SUBLANES, 1), 0)

    def conv_silu(cur, tail, w_ref, b_ref):
        acc = cur * w_ref[SSD_CONV - 1:SSD_CONV, :] + b_ref[...]
        for k in range(1, SSD_CONV):
            rolled = pltpu.roll(cur, k, 0)
            head = jnp.where(row8 < k, pltpu.roll(tail, k, 0), rolled[0:SUBLANES])
            shifted = jnp.concatenate([head, rolled[SUBLANES:]], axis=0)
            acc = acc + shifted * w_ref[SSD_CONV - 1 - k:SSD_CONV - k, :]
        return _silu(acc)

    xs_raw = xs_ref[...].astype(F32)
    bc_raw = bc_ref[...].astype(F32)
    xs_c = conv_silu(xs_raw, tailx_ref[...], cwx_ref, cbx_ref)
    bc_c = conv_silu(bc_raw, tailbc_ref[...], cwbc_ref, cbbc_ref)
    tailx_ref[...] = xs_raw[L - SUBLANES:, :]
    tailbc_ref[...] = bc_raw[L - SUBLANES:, :]

    dtx = dt_ref[...] + dtb_ref[...]
    dt = jnp.maximum(dtx, 0.0) + jnp.log1p(jnp.exp(-jnp.abs(dtx)))
    a = -jnp.exp(alog_ref[...])
    da = dt * a
    r_i = lax.broadcasted_iota(jnp.int32, (L, L), 0)
    c_i = lax.broadcasted_iota(jnp.int32, (L, L), 1)
    causal = c_i <= r_i
    tril = jnp.where(causal, 1.0, 0.0).astype(BF16)
    acs = sum(_dot(tril, p) for p in _split_bf16(da, 3))
    acs_t = acs.T

    e = e_ref[...]
    dt_x = sum(_dot(p, e) for p in _split_bf16(dt, 2))
    acs_x = sum(_dot(p, e) for p in _split_bf16(acs, 3))
    last_x = acs_x[L - 1:L, :]
    xdt_f = xs_c * dt_x
    xdt = xdt_f.astype(BF16)
    xds = (xdt_f * jnp.exp(last_x - acs_x)).astype(BF16)
    expacs_x = jnp.exp(acs_x)
    cdecay_x = jnp.exp(last_x)

    lane = lax.broadcasted_iota(jnp.int32, (1, LANES), 1)
    first = lane < SSD_HEAD_DIM
    heads_per_group = gw // SSD_HEAD_DIM

    for g in range(G):
        cols = slice(g * gw, (g + 1) * gw)
        b_g = bc_c[:, g * N:(g + 1) * N]
        c_g = bc_c[:, (G + g) * N:(G + g + 1) * N].astype(BF16)
        cb = _dot_nt(c_g, b_g.astype(BF16))
        state_old = state_ref[:, cols]
        y_off = _dot(c_g, state_old.astype(BF16)) * expacs_x[:, cols]
        state_ref[:, cols] = state_old * cdecay_x[:, cols] + _dot(b_g.T.astype(BF16), xds[:, cols])
        for pair in range(heads_per_group // 2):
            h0 = g * heads_per_group + 2 * pair
            pc = slice(g * gw + pair * LANES, g * gw + (pair + 1) * LANES)
            xp = xdt[:, pc]
            res = []
            for h in (h0, h0 + 1):
                seg = acs[:, h:h + 1] - acs_t[h:h + 1, :]
                dec = jnp.exp(jnp.where(causal, seg, -jnp.inf))
                res.append(_dot((cb * dec).astype(BF16), xp))
            y_ref[:, pc] = jnp.where(first, res[0], res[1])
        y = y_ref[:, cols] + y_off + xs_c[:, cols] * dx_ref[:, cols]
        y = y * _silu(z_ref[:, cols].astype(F32))
        o_ref[:, cols] = (y * _rms_scale(y) * nw_ref[:, cols]).astype(o_ref.dtype)


def _ssd(proj3, dt3, cwx, cbx, cwbc, cbbc, dtb, alog, dx, nw, e, z_col, xs_col, bc_col):
    b, s, _ = proj3.shape
    w = e.shape[1]
    bcw = cwbc.shape[1]
    L = SSD_CHUNK
    full = lambda shape: pl.BlockSpec(shape, lambda bi, c: (0, 0))
    return pl.pallas_call(
        _ssd_kernel,
        out_shape=jax.ShapeDtypeStruct((b, s, w), BF16),
        grid=(b, s // L),
        in_specs=[
            pl.BlockSpec((None, L, w), lambda bi, c: (bi, c, z_col)),
            pl.BlockSpec((None, L, w), lambda bi, c: (bi, c, xs_col)),
            pl.BlockSpec((None, L, bcw), lambda bi, c: (bi, c, bc_col)),
            pl.BlockSpec((None, L, LANES), lambda bi, c: (bi, c, 0)),
            full((SSD_CONV, w)), full((1, w)), full((SSD_CONV, bcw)), full((1, bcw)),
            full((1, LANES)), full((1, LANES)), full((1, w)), full((1, w)), full((LANES, w)),
        ],
        out_specs=pl.BlockSpec((None, L, w), lambda bi, c: (bi, c, 0)),
        scratch_shapes=[
            pltpu.VMEM((SSD_STATE, w), F32),
            pltpu.VMEM((SUBLANES, w), F32),
            pltpu.VMEM((SUBLANES, bcw), F32),
            pltpu.VMEM((L, w), F32),
        ],
        compiler_params=pltpu.CompilerParams(
            dimension_semantics=("parallel", "arbitrary"), vmem_limit_bytes=VMEM_LIMIT),
        name="ssd",
    )(proj3, proj3, proj3, dt3, cwx, cbx, cwbc, cbbc, dtb, alog, dx, nw, e)


def _out_proj_kernel(a_ref, s_ref, x_ref, *rest):
    w_refs, o_ref = rest[:-1], rest[-1]
    ka = a_ref.shape[1]
    acc = x_ref[...] + _dot(a_ref[...], w_refs[0][...].astype(BF16))
    for c, w_ref in enumerate(w_refs[1:]):
        acc = acc + _dot(s_ref[:, c * ka:(c + 1) * ka], w_ref[...].astype(BF16))
    o_ref[...] = acc


def _out_proj(a2, s2, x2, w_all, layer, tm=1024, tn=512):
    m, ka = a2.shape
    n_s = s2.shape[1] // ka
    n = w_all.shape[2]
    w4 = w_all.reshape(w_all.shape[0], 1 + n_s, ka, n)
    w_specs = [pl.BlockSpec((None, None, ka, tn), lambda i, j, c=c: (layer, c, 0, j)) for c in range(1 + n_s)]
    return pl.pallas_call(
        _out_proj_kernel,
        out_shape=jax.ShapeDtypeStruct((m, n), F32),
        grid=(m // tm, n // tn),
        in_specs=[
            pl.BlockSpec((tm, ka), lambda i, j: (i, 0)),
            pl.BlockSpec((tm, n_s * ka), lambda i, j: (i, 0)),
            pl.BlockSpec((tm, tn), lambda i, j: (i, j)),
        ] + w_specs,
        out_specs=pl.BlockSpec((tm, tn), lambda i, j: (i, j)),
        compiler_params=pltpu.CompilerParams(
            dimension_semantics=("parallel", "parallel"), vmem_limit_bytes=VMEM_LIMIT),
        name="out_proj",
    )(a2, s2, x2, *([w4] * (1 + n_s)))


def _xattn_kernel(q_ref, kv_ref, h_ref, qw_ref, kw_ref, wo_ref, o_ref):
    d = X_HEAD_DIM
    outs = []
    for hh in range(X_HEADS):
        qf = q_ref[:, hh * d:(hh + 1) * d].astype(F32)
        kf = kv_ref[:, hh * d:(hh + 1) * d].astype(F32)
        v = kv_ref[:, (X_HEADS + hh) * d:(X_HEADS + hh + 1) * d]
        qn = (qf * _rms_scale(qf) * qw_ref[...] * (d ** -0.5)).astype(BF16)
        kn = (kf * _rms_scale(kf) * kw_ref[...]).astype(BF16)
        s = _dot_nt(qn, kn)
        p = jnp.exp(s - jnp.max(s, axis=-1, keepdims=True))
        o = _dot(p.astype(BF16), v) / jnp.sum(p, axis=-1, keepdims=True)
        outs.append(o.astype(BF16))
    o_all = jnp.concatenate(outs, axis=-1)
    o_ref[...] = h_ref[...] + _dot(o_all, wo_ref[...].astype(BF16))


def _xattn(q3, kv3, h3, qw, kw, wo_all, layer, tq=512):
    b, s, xw = q3.shape
    mlen = kv3.shape[1]
    dm = h3.shape[-1]
    return pl.pallas_call(
        _xattn_kernel,
        out_shape=jax.ShapeDtypeStruct((b, s, dm), F32),
        grid=(b, s // tq),
        in_specs=[
            pl.BlockSpec((None, tq, xw), lambda bi, i: (bi, i, 0)),
            pl.BlockSpec((None, mlen, 2 * xw), lambda bi, i: (bi, 0, 0)),
            pl.BlockSpec((None, tq, dm), lambda bi, i: (bi, i, 0)),
            pl.BlockSpec((1, X_HEAD_DIM), lambda bi, i: (0, 0)),
            pl.BlockSpec((1, X_HEAD_DIM), lambda bi, i: (0, 0)),
            pl.BlockSpec((None, xw, dm), lambda bi, i: (layer, 0, 0)),
        ],
        out_specs=pl.BlockSpec((None, tq, dm), lambda bi, i: (bi, i, 0)),
        compiler_params=pltpu.CompilerParams(
            dimension_semantics=("parallel", "parallel"), vmem_limit_bytes=VMEM_LIMIT),
        name="xattn",
    )(q3, kv3, h3, qw, kw, wo_all)


def _ffn_kernel(h_ref, nw_ref, wg_ref, wu_ref, wd_ref, o_ref, hn_ref):
    f = pl.program_id(1)

    @pl.when(f == 0)
    def _():
        hf = h_ref[...]
        hn_ref[...] = (hf * _rms_scale(hf) * nw_ref[...]).astype(BF16)
        o_ref[...] = hf

    hn = hn_ref[...]
    gate = _dot(hn, wg_ref[...].astype(BF16))
    up = _dot(hn, wu_ref[...].astype(BF16))
    act = (_silu(gate) * up).astype(BF16)
    o_ref[...] += _dot(act, wd_ref[...].astype(BF16))


def _ffn(h2, nw, wg_all, wu_all, wd_all, layer, tm=1024, tf=256):
    m, k = h2.shape
    dff = wg_all.shape[2]
    return pl.pallas_call(
        _ffn_kernel,
        out_shape=jax.ShapeDtypeStruct((m, k), F32),
        grid=(m // tm, dff // tf),
        in_specs=[
            pl.BlockSpec((tm, k), lambda i, f: (i, 0), pipeline_mode=pl.Buffered(1)),
            pl.BlockSpec((1, k), lambda i, f: (0, 0)),
            pl.BlockSpec((None, k, tf), lambda i, f: (layer, 0, f)),
            pl.BlockSpec((None, k, tf), lambda i, f: (layer, 0, f)),
            pl.BlockSpec((None, tf, k), lambda i, f: (layer, f, 0)),
        ],
        out_specs=pl.BlockSpec((tm, k), lambda i, f: (i, 0)),
        scratch_shapes=[pltpu.VMEM((tm, k), BF16)],
        compiler_params=pltpu.CompilerParams(
            dimension_semantics=("parallel", "arbitrary"), vmem_limit_bytes=FFN_VMEM_LIMIT),
        name="ffn",
    )(h2, nw, wg_all, wu_all, wd_all)


def _pad_lanes(v, fill=0.0):
    return jnp.pad(v, ((0, 0), (0, LANES - v.shape[1])), constant_values=fill)


def kernel(x, mem, mix_norm_w, w_in, da_q_norm_w, da_k_norm_w, da_lambda_q1, da_lambda_k1, da_lambda_q2, da_lambda_k2, da_subln_w, ssd_conv_w, ssd_conv_b, ssd_dt_bias, ssd_a_log, ssd_d, ssd_norm_w, w_out, xattn_norm_w, mem_norm_w, xattn_w_q, xattn_w_kv, xattn_q_norm_w, xattn_k_norm_w, xattn_w_o, ffn_norm_w, ffn_w_gate, ffn_w_up, ffn_w_down):
    b, s, dm = x.shape
    mlen = mem.shape[1]
    depth = w_in.shape[0]
    da_w = DA_HEADS * DA_V_DIM
    ssd_w = ssd_norm_w.shape[1]
    ssd_heads = ssd_w // SSD_HEAD_DIM
    gn = SSD_GROUPS * SSD_STATE
    xw = X_HEADS * X_HEAD_DIM
    o_q, o_k, o_v, o_z = 0, da_w, 2 * da_w, 3 * da_w
    o_xs = o_z + ssd_w
    o_bc = o_xs + ssd_w
    o_dt = o_bc + 2 * gn
    z_col, xs_col = 0, 1
    bc_col = (2 * ssd_w) // (2 * gn)
    q_col = (2 * ssd_w + 2 * gn) // DA_V_DIM
    k_col = q_col + DA_HEADS
    v_col = k_col + DA_HEADS
    assert (2 * ssd_w) % (2 * gn) == 0 and ssd_heads <= LANES

    expand = np.zeros((LANES, ssd_w), np.float32)
    for h in range(ssd_heads):
        expand[h, h * SSD_HEAD_DIM:(h + 1) * SSD_HEAD_DIM] = 1.0
    expand = jnp.asarray(expand, BF16)

    h = x.reshape(b * s, dm)
    mem2 = mem.reshape(b * mlen, dm)
    for i in range(depth):
        lambda_init = 0.8 - 0.6 * math.exp(-0.3 * i)
        w_dt = _pad_lanes(w_in[i][:, o_dt:]).astype(BF16)
        proj, dt_raw = _in_proj(h, mix_norm_w[i][None], w_in, i, w_dt, o_dt, o_z)
        proj3 = proj.reshape(b, s, -1)

        lam_p = jnp.stack([da_lambda_q1[i], da_lambda_k1[i], da_lambda_q2[i], da_lambda_k2[i]])
        a_out = _diff_attn(proj3, jnp.tile(da_q_norm_w[i], 2)[None], jnp.tile(da_k_norm_w[i], 2)[None],
                           lam_p, da_subln_w[i][None], lambda_init, q_col, k_col, v_col)

        cw, cb = ssd_conv_w[i], ssd_conv_b[i][None]
        s_out = _ssd(proj3, dt_raw.reshape(b, s, LANES),
                     cw[:, :ssd_w], cb[:, :ssd_w], cw[:, ssd_w:], cb[:, ssd_w:],
                     _pad_lanes(ssd_dt_bias[i][None]), _pad_lanes(ssd_a_log[i][None]),
                     jnp.repeat(ssd_d[i], SSD_HEAD_DIM)[None], ssd_norm_w[i][None], expand,
                     z_col, xs_col, bc_col)

        h1 = _out_proj(a_out.reshape(b * s, da_w), s_out.reshape(b * s, ssd_w), h, w_out, i)

        xq = _norm_matmul(h1, xattn_norm_w[i][None], xattn_w_q, i, 512, "xq_proj")
        xkv = _norm_matmul(mem2, mem_norm_w[i][None], xattn_w_kv, i, 256, "xkv_proj")
        h2 = _xattn(xq.reshape(b, s, xw), xkv.reshape(b, mlen, 2 * xw), h1.reshape(b, s, dm),
                    xattn_q_norm_w[i][None], xattn_k_norm_w[i][None], xattn_w_o, i)

        h = _ffn(h2.reshape(b * s, dm), ffn_norm_w[i][None], ffn_w_gate, ffn_w_up, ffn_w_down, i)
    return h.reshape(b, s, dm)
```

```python
import functools
import math

import jax
import jax.numpy as jnp
import numpy as np
from jax import lax
from jax.experimental import pallas as pl
from jax.experimental.pallas import tpu as pltpu

EPS = 1e-6
LANES = 128
SUBLANES = 8
VMEM_LIMIT = 48 * 1024 * 1024
BIG_VMEM_LIMIT = 56 * 1024 * 1024
LOG2E = math.log2(math.e)
DEN_ROWS = 16

DA_HEADS = 8
DA_HEAD_DIM = 64
DA_V_DIM = 128
SSD_HEAD_DIM = 64
SSD_GROUPS = 4
SSD_STATE = 128
SSD_CONV = 4
SSD_CHUNK = 128
X_HEADS = 4
X_HEAD_DIM = 128

F32 = jnp.float32
BF16 = jnp.bfloat16


def _rms_scale(xf, eps=EPS):
    return lax.rsqrt(jnp.mean(xf * xf, axis=-1, keepdims=True) + eps)


def _silu(x):
    h = 0.5 * x
    return h + h * jnp.tanh(h)


def _split_bf16(x, parts):
    out = []
    r = x
    for _ in range(parts):
        p = r.astype(BF16)
        out.append(p)
        r = r - p.astype(F32)
    return out


def _dot(a, b):
    return jnp.dot(a, b, preferred_element_type=F32)


def _dot_nt(a, b):
    return lax.dot_general(a, b, (((1,), (1,)), ((), ())), preferred_element_type=F32)


def _in_proj_kernel(x_ref, nw_ref, wt_ref, wdt_ref, o_ref, dt_ref, hn_ref):
    n, mi = pl.program_id(1), pl.program_id(2)

    @pl.when(n == 0)
    def _():
        xf = x_ref[...]
        hn = (xf * _rms_scale(xf) * nw_ref[...]).astype(BF16)
        hn_ref[mi] = hn
        wdt = wdt_ref[...].astype(BF16)
        wdt = jnp.concatenate([wdt, jnp.zeros((LANES - wdt.shape[0], wdt.shape[1]), BF16)], axis=0)
        dt_ref[...] = _dot_nt(hn, wdt)

    o_ref[...] = _dot_nt(hn_ref[mi], wt_ref[...].astype(BF16)).astype(o_ref.dtype)


def _in_proj(x2, nw, wt_all, layer, n, col_shift, tm=1024, tn=1024, m_inner=2):
    m, k = x2.shape
    n_dt = wt_all.shape[1] - n
    shift, nblk = col_shift // tn, n // tn
    assert col_shift % tn == 0 and n % tn == 0 and n % n_dt == 0 and m % (tm * m_inner) == 0
    row = lambda mo, j, mi: mo * m_inner + jnp.where(j == 0, mi, m_inner - 1)
    return pl.pallas_call(
        _in_proj_kernel,
        out_shape=(jax.ShapeDtypeStruct((m, n), BF16), jax.ShapeDtypeStruct((m, LANES), F32)),
        grid=(m // (tm * m_inner), nblk, m_inner),
        in_specs=[
            pl.BlockSpec((tm, k), lambda mo, j, mi: (row(mo, j, mi), 0)),
            pl.BlockSpec((1, k), lambda mo, j, mi: (0, 0)),
            pl.BlockSpec((None, tn, k), lambda mo, j, mi: (layer, (j + shift) % nblk, 0)),
            pl.BlockSpec((None, n_dt, k), lambda mo, j, mi: (layer, n // n_dt, 0)),
        ],
        out_specs=(
            pl.BlockSpec((tm, tn), lambda mo, j, mi: (mo * m_inner + mi, j)),
            pl.BlockSpec((tm, LANES), lambda mo, j, mi: (row(mo, j, mi), 0)),
        ),
        scratch_shapes=[pltpu.VMEM((m_inner, tm, k), BF16)],
        compiler_params=pltpu.CompilerParams(
            dimension_semantics=("parallel", "arbitrary", "arbitrary"), vmem_limit_bytes=BIG_VMEM_LIMIT),
        name="in_proj",
    )(x2, nw, wt_all, wt_all)


def _xkv_kernel(x_ref, nw_ref, w_ref, kw_ref, o_ref):
    d = X_HEAD_DIM
    xf = x_ref[...]
    hn = (xf * _rms_scale(xf) * nw_ref[...]).astype(BF16)
    kv = _dot(hn, w_ref[...].astype(BF16))
    kw = X_HEADS * d
    for hh in range(X_HEADS):
        kf = kv[:, hh * d:(hh + 1) * d]
        o_ref[:, hh * d:(hh + 1) * d] = (kf * _rms_scale(kf) * kw_ref[...]).astype(o_ref.dtype)
    o_ref[:, kw:] = kv[:, kw:].astype(o_ref.dtype)


def _xkv_proj(x2, nw, w_all, layer, kw, tm=256):
    m, k = x2.shape
    n = w_all.shape[2]
    return pl.pallas_call(
        _xkv_kernel,
        out_shape=jax.ShapeDtypeStruct((m, n), BF16),
        grid=(m // tm,),
        in_specs=[
            pl.BlockSpec((tm, k), lambda i: (i, 0)),
            pl.BlockSpec((1, k), lambda i: (0, 0)),
            pl.BlockSpec((None, k, n), lambda i: (layer, 0, 0)),
            pl.BlockSpec((1, X_HEAD_DIM), lambda i: (0, 0)),
        ],
        out_specs=pl.BlockSpec((tm, n), lambda i: (i, 0)),
        compiler_params=pltpu.CompilerParams(
            dimension_semantics=("parallel",), vmem_limit_bytes=VMEM_LIMIT),
        name="xkv_proj",
    )(x2, nw, w_all, kw)


def _diff_attn_kernel(q_ref, k_ref, v_ref, qw_ref, kw_ref, lam_ref, sw_ref, *rest,
                      tq, hb, lambda_init, n_cast):
    cast_in, o_ref, cast_out = rest[:n_cast], rest[n_cast], rest[n_cast + 1:2 * n_cast + 1]
    kn_ref, vt_ref, qcat_ref, m_ref, acc_ref, s_ref = rest[2 * n_cast + 1:]
    for w_ref, wb_ref in zip(cast_in, cast_out):
        wb_ref[...] = w_ref[...].astype(BF16)
    i = pl.program_id(2)
    n_tiles = vt_ref.shape[1]
    d = DA_V_DIM
    lane = lax.broadcasted_iota(jnp.int32, (1, d), 1)
    first = lane < DA_HEAD_DIM
    ones_row = lax.broadcasted_iota(jnp.int32, (DEN_ROWS, tq), 0) == 0

    def sub_head_norm(xf):
        sq = xf * xf
        s1 = jnp.sum(jnp.where(first, sq, 0.0), axis=-1, keepdims=True)
        s2 = jnp.sum(jnp.where(first, 0.0, sq), axis=-1, keepdims=True)
        r1 = lax.rsqrt(s1 * (1.0 / DA_HEAD_DIM) + EPS)
        r2 = lax.rsqrt(s2 * (1.0 / DA_HEAD_DIM) + EPS)
        return xf * jnp.where(first, r1, r2)

    @pl.when(i == 0)
    def _():
        for hh in range(hb):
            hs = slice(hh * d, (hh + 1) * d)
            kn_ref[hh] = (sub_head_norm(k_ref[:, hs].astype(F32)) * kw_ref[...]).astype(BF16)
            for t in range(n_tiles):
                vt_ref[hh, t, :d] = v_ref[t * tq:(t + 1) * tq, hs].astype(F32).T.astype(BF16)
                vt_ref[hh, t, d:] = jnp.where(ones_row, 1.0, 0.0).astype(BF16)

    for hh in range(hb):
        qn = sub_head_norm(q_ref[:, hh * d:(hh + 1) * d].astype(F32)) * qw_ref[...] * (DA_HEAD_DIM ** -0.5 * LOG2E)
        qcat_ref[hh] = jnp.concatenate(
            [jnp.where(first, qn, 0.0), jnp.where(first, 0.0, qn)], axis=0).astype(BF16)
    m_ref[...] = jnp.full(m_ref.shape, -jnp.inf, F32)
    acc_ref[...] = jnp.zeros(acc_ref.shape, F32)

    def tile(j, masked):
        start = pl.multiple_of(j * tq, tq)
        if masked:
            key = lax.broadcasted_iota(jnp.int32, (tq, 2 * tq), 0)
            qry = lax.broadcasted_iota(jnp.int32, (tq, 2 * tq), 1)
            causal = key <= jnp.where(qry >= tq, qry - tq, qry)
        for hh in range(hb):
            s_ref[hh] = _dot_nt(kn_ref[hh, pl.ds(start, tq), :], qcat_ref[hh])
        for hh in range(hb):
            s = s_ref[hh]
            if masked:
                s = jnp.where(causal, s, -jnp.inf)
            m_prev = m_ref[hh]
            m_new = jnp.maximum(m_prev, jnp.max(s, axis=0, keepdims=True))
            alpha = jnp.exp2(m_prev - m_new)
            p = jnp.exp2(s - m_new)
            acc_ref[hh] = alpha * acc_ref[hh] + _dot(vt_ref[hh, j], p.astype(BF16))
            m_ref[hh] = m_new

    def body(j, carry):
        tile(j, False)
        return carry

    lax.fori_loop(0, i, body, 0)
    tile(i, True)

    lam_p = lam_ref[...]
    lam = (jnp.exp(jnp.sum(lam_p[0:1] * lam_p[1:2], axis=-1, keepdims=True))
           - jnp.exp(jnp.sum(lam_p[2:3] * lam_p[3:4], axis=-1, keepdims=True)) + lambda_init)
    for hh in range(hb):
        ot = acc_ref[hh, :d] / acc_ref[hh, d:d + 1]
        o = (ot[:, :tq] - lam * ot[:, tq:]).T
        o = o * _rms_scale(o) * sw_ref[...] * (1.0 - lambda_init)
        o_ref[:, hh * d:(hh + 1) * d] = o.astype(o_ref.dtype)


def _diff_attn(proj3, qw2, kw2, lam_p, sw, lambda_init, q_col, k_col, v_col, cast_ws, layer, tq=256, hb=8):
    b, s, _ = proj3.shape
    d = DA_V_DIM
    nh, nq = DA_HEADS // hb, s // tq
    steps = b * nh * nq
    kern = functools.partial(_diff_attn_kernel, tq=tq, hb=hb, lambda_init=lambda_init, n_cast=len(cast_ws))
    vec = lambda shape: pl.BlockSpec(shape, lambda bi, h, i: (0, 0))
    cast_in, cast_out, cast_shapes = [], [], []
    for w in cast_ws:
        rows, cols = w.shape[1:]
        rb = rows // steps
        assert rows % steps == 0 and rb % (2 * SUBLANES) == 0
        cast_in.append(pl.BlockSpec((None, rb, cols), lambda bi, h, i: (layer, (bi * nh + h) * nq + i, 0)))
        cast_out.append(pl.BlockSpec((rb, cols), lambda bi, h, i: ((bi * nh + h) * nq + i, 0)))
        cast_shapes.append(jax.ShapeDtypeStruct((rows, cols), BF16))
    outs = pl.pallas_call(
        kern,
        out_shape=[jax.ShapeDtypeStruct((b, s, DA_HEADS * d), BF16)] + cast_shapes,
        grid=(b, nh, nq),
        in_specs=[
            pl.BlockSpec((None, tq, hb * d), lambda bi, h, i: (bi, i, q_col // hb + h)),
            pl.BlockSpec((None, s, hb * d), lambda bi, h, i: (bi, 0, k_col // hb + h)),
            pl.BlockSpec((None, s, hb * d), lambda bi, h, i: (bi, 0, v_col // hb + h)),
            vec((1, d)), vec((1, d)), vec((4, DA_HEAD_DIM)), vec((1, d)),
        ] + cast_in,
        out_specs=[pl.BlockSpec((None, tq, hb * d), lambda bi, h, i: (bi, i, h))] + cast_out,
        scratch_shapes=[
            pltpu.VMEM((hb, s, d), BF16),
            pltpu.VMEM((hb, s // tq, d + DEN_ROWS, tq), BF16),
            pltpu.VMEM((hb, 2 * tq, d), BF16),
            pltpu.VMEM((hb, 1, 2 * tq), F32),
            pltpu.VMEM((hb, d + DEN_ROWS, 2 * tq), F32),
            pltpu.VMEM((hb, tq, 2 * tq), F32),
        ],
        compiler_params=pltpu.CompilerParams(
            dimension_semantics=("parallel", "parallel", "arbitrary"), vmem_limit_bytes=BIG_VMEM_LIMIT),
        name="diff_attn",
    )(proj3, proj3, proj3, qw2, kw2, lam_p, sw, *cast_ws)
    return outs[0], outs[1:]


def _ssd_kernel(z_ref, xs_ref, bc_ref, dt_ref, cwx_ref, cbx_ref, cwbc_ref, cbbc_ref,
                dtb_ref, alog_ref, dx_ref, nw_ref, e_ref, o_ref,
                state_ref, tailx_ref, tailbc_ref, y_ref):
    L, N, G = SSD_CHUNK, SSD_STATE, SSD_GROUPS
    gw = o_ref.shape[-1] // G
    c = pl.program_id(1)

    @pl.when(c == 0)
    def _():
        state_ref[...] = jnp.zeros(state_ref.shape, F32)
        tailx_ref[...] = jnp.zeros(tailx_ref.shape, F32)
        tailbc_ref[...] = jnp.zeros(tailbc_ref.shape, F32)

    row8 = lax.broadcasted_iota(jnp.int32, (SUBLANES, 1), 0)

    def conv_silu(cur, tail, w_ref, b_ref):
        acc = cur * w_ref[SSD_CONV - 1:SSD_CONV, :] + b_ref[...]
        for k in range(1, SSD_CONV):
            rolled = pltpu.roll(cur, k, 0)
            head = jnp.where(row8 < k, pltpu.roll(tail, k, 0), rolled[0:SUBLANES])
            shifted = jnp.concatenate([head, rolled[SUBLANES:]], axis=0)
            acc = acc + shifted * w_ref[SSD_CONV - 1 - k:SSD_CONV - k, :]
        return _silu(acc)

    xs_raw = xs_ref[...].astype(F32)
    bc_raw = bc_ref[...].astype(F32)
    xs_c = conv_silu(xs_raw, tailx_ref[...], cwx_ref, cbx_ref)
    bc_c = conv_silu(bc_raw, tailbc_ref[...], cwbc_ref, cbbc_ref)
    tailx_ref[...] = xs_raw[L - SUBLANES:, :]
    tailbc_ref[...] = bc_raw[L - SUBLANES:, :]

    dtx = dt_ref[...] + dtb_ref[...]
    dt = jnp.maximum(dtx, 0.0) + jnp.log1p(jnp.exp(-jnp.abs(dtx)))
    a = -jnp.exp(alog_ref[...])
    da = dt * a
    r_i = lax.broadcasted_iota(jnp.int32, (L, L), 0)
    c_i = lax.broadcasted_iota(jnp.int32, (L, L), 1)
    causal = c_i <= r_i
    tril = jnp.where(causal, 1.0, 0.0).astype(BF16)
    acs = sum(_dot(tril, p) for p in _split_bf16(da, 3))
    acs_t = acs.T

    e = e_ref[...]
    dt_x = sum(_dot(p, e) for p in _split_bf16(dt, 2))
    acs_x = sum(_dot(p, e) for p in _split_bf16(acs, 3))
    last_x = acs_x[L - 1:L, :]
    xdt_f = xs_c * dt_x
    xdt = xdt_f.astype(BF16)
    xds = (xdt_f * jnp.exp(last_x - acs_x)).astype(BF16)
    expacs_x = jnp.exp(acs_x)
    cdecay_x = jnp.exp(last_x)

    lane = lax.broadcasted_iota(jnp.int32, (1, LANES), 1)
    first = lane < SSD_HEAD_DIM
    heads_per_group = gw // SSD_HEAD_DIM

    for g in range(G):
        cols = slice(g * gw, (g + 1) * gw)
        b_g = bc_c[:, g * N:(g + 1) * N]
        c_g = bc_c[:, (G + g) * N:(G + g + 1) * N].astype(BF16)
        cb = _dot_nt(c_g, b_g.astype(BF16))
        state_old = state_ref[:, cols]
        y_off = _dot(c_g, state_old.astype(BF16)) * expacs_x[:, cols]
        state_ref[:, cols] = state_old * cdecay_x[:, cols] + _dot(b_g.T.astype(BF16), xds[:, cols])
        for pair in range(heads_per_group // 2):
            h0 = g * heads_per_group + 2 * pair
            pc = slice(g * gw + pair * LANES, g * gw + (pair + 1) * LANES)
            xp = xdt[:, pc]
            res = []
            for h in (h0, h0 + 1):
                seg = acs[:, h:h + 1] - acs_t[h:h + 1, :]
                dec = jnp.exp(jnp.where(causal, seg, -jnp.inf))
                res.append(_dot((cb * dec).astype(BF16), xp))
            y_ref[:, pc] = jnp.where(first, res[0], res[1])
        y = y_ref[:, cols] + y_off + xs_c[:, cols] * dx_ref[:, cols]
        y = y * _silu(z_ref[:, cols].astype(F32))
        o_ref[:, cols] = (y * _rms_scale(y) * nw_ref[:, cols]).astype(o_ref.dtype)


def _ssd(proj3, dt3, cwx, cbx, cwbc, cbbc, dtb, alog, dx, nw, e, z_col, xs_col, bc_col):
    b, s, _ = proj3.shape
    w = e.shape[1]
    bcw = cwbc.shape[1]
    L = SSD_CHUNK
    full = lambda shape: pl.BlockSpec(shape, lambda bi, c: (0, 0))
    return pl.pallas_call(
        _ssd_kernel,
        out_shape=jax.ShapeDtypeStruct((b, s, w), BF16),
        grid=(b, s // L),
        in_specs=[
            pl.BlockSpec((None, L, w), lambda bi, c: (bi, c, z_col)),
            pl.BlockSpec((None, L, w), lambda bi, c: (bi, c, xs_col)),
            pl.BlockSpec((None, L, bcw), lambda bi, c: (bi, c, bc_col)),
            pl.BlockSpec((None, L, LANES), lambda bi, c: (bi, c, 0)),
            full((SSD_CONV, w)), full((1, w)), full((SSD_CONV, bcw)), full((1, bcw)),
            full((1, LANES)), full((1, LANES)), full((1, w)), full((1, w)), full((LANES, w)),
        ],
        out_specs=pl.BlockSpec((None, L, w), lambda bi, c: (bi, c, 0)),
        scratch_shapes=[
            pltpu.VMEM((SSD_STATE, w), F32),
            pltpu.VMEM((SUBLANES, w), F32),
            pltpu.VMEM((SUBLANES, bcw), F32),
            pltpu.VMEM((L, w), F32),
        ],
        compiler_params=pltpu.CompilerParams(
            dimension_semantics=("parallel", "arbitrary"), vmem_limit_bytes=VMEM_LIMIT),
        name="ssd",
    )(proj3, proj3, proj3, dt3, cwx, cbx, cwbc, cbbc, dtb, alog, dx, nw, e)


def _mix_out_kernel(a_ref, s_ref, x_ref, w_ref, xnw_ref, wq_ref, kv_ref, qw_ref, wo_ref, o_ref):
    d = X_HEAD_DIM
    ka = a_ref.shape[1]
    h1 = x_ref[...] + _dot(a_ref[...], w_ref[0])
    for c in range(1, w_ref.shape[0]):
        h1 = h1 + _dot(s_ref[:, (c - 1) * ka:c * ka], w_ref[c])
    hn = (h1 * _rms_scale(h1) * xnw_ref[...]).astype(BF16)
    q = _dot(hn, wq_ref[...])
    outs = []
    for hh in range(X_HEADS):
        qf = q[:, hh * d:(hh + 1) * d]
        qn = (qf * _rms_scale(qf) * qw_ref[...] * (d ** -0.5 * LOG2E)).astype(BF16)
        s = _dot_nt(qn, kv_ref[:, hh * d:(hh + 1) * d])
        p = jnp.exp2(s - jnp.max(s, axis=-1, keepdims=True))
        v = kv_ref[:, (X_HEADS + hh) * d:(X_HEADS + hh + 1) * d]
        o = _dot(p.astype(BF16), v) / jnp.sum(p, axis=-1, keepdims=True)
        outs.append(o.astype(BF16))
    o_ref[...] = h1 + _dot(jnp.concatenate(outs, axis=-1), wo_ref[...])


def _mix_out(a2, s2, x2, w_out, xnw, wq, kv3, qw, wo, seq, tm=512):
    m, ka = a2.shape
    ks = s2.shape[1]
    dm = x2.shape[1]
    xw = wq.shape[1]
    mlen = kv3.shape[1]
    assert seq % tm == 0 and ks % ka == 0
    once = dict(pipeline_mode=pl.Buffered(1))
    return pl.pallas_call(
        _mix_out_kernel,
        out_shape=jax.ShapeDtypeStruct((m, dm), F32),
        grid=(m // tm,),
        in_specs=[
            pl.BlockSpec((tm, ka), lambda i: (i, 0)),
            pl.BlockSpec((tm, ks), lambda i: (i, 0)),
            pl.BlockSpec((tm, dm), lambda i: (i, 0)),
            pl.BlockSpec((1 + ks // ka, ka, dm), lambda i: (0, 0, 0), **once),
            pl.BlockSpec((1, dm), lambda i: (0, 0)),
            pl.BlockSpec((dm, xw), lambda i: (0, 0), **once),
            pl.BlockSpec((None, mlen, 2 * xw), lambda i: (i // (seq // tm), 0, 0)),
            pl.BlockSpec((1, X_HEAD_DIM), lambda i: (0, 0)),
            pl.BlockSpec((xw, dm), lambda i: (0, 0), **once),
        ],
        out_specs=pl.BlockSpec((tm, dm), lambda i: (i, 0)),
        compiler_params=pltpu.CompilerParams(
            dimension_semantics=("parallel",), vmem_limit_bytes=BIG_VMEM_LIMIT),
        name="mix_out",
    )(a2, s2, x2, w_out.reshape(1 + ks // ka, ka, dm), xnw, wq, kv3, qw, wo)


def _ffn_kernel(h_ref, nw_ref, wg_ref, wu_ref, wd_ref, o_ref, hn_ref):
    f = pl.program_id(1)

    @pl.when(f == 0)
    def _():
        hf = h_ref[...]
        hn_ref[...] = (hf * _rms_scale(hf) * nw_ref[...]).astype(BF16)
        o_ref[...] = hf

    hn = hn_ref[...]
    act = (_silu(_dot(hn, wg_ref[...])) * _dot(hn, wu_ref[...])).astype(BF16)
    o_ref[...] += _dot(act, wd_ref[...])


def _ffn(h2, nw, wg, wu, wd, tm=1024, tf=512):
    m, k = h2.shape
    dff = wg.shape[1]
    return pl.pallas_call(
        _ffn_kernel,
        out_shape=jax.ShapeDtypeStruct((m, k), F32),
        grid=(m // tm, dff // tf),
        in_specs=[
            pl.BlockSpec((tm, k), lambda i, f: (i, 0), pipeline_mode=pl.Buffered(1)),
            pl.BlockSpec((1, k), lambda i, f: (0, 0)),
            pl.BlockSpec((k, tf), lambda i, f: (0, f)),
            pl.BlockSpec((k, tf), lambda i, f: (0, f)),
            pl.BlockSpec((tf, k), lambda i, f: (f, 0)),
        ],
        out_specs=pl.BlockSpec((tm, k), lambda i, f: (i, 0)),
        scratch_shapes=[pltpu.VMEM((tm, k), BF16)],
        compiler_params=pltpu.CompilerParams(
            dimension_semantics=("parallel", "arbitrary"), vmem_limit_bytes=BIG_VMEM_LIMIT),
        name="ffn",
    )(h2, nw, wg, wu, wd)


def _pad_lanes(v, fill=0.0):
    return jnp.pad(v, ((0, 0), (0, LANES - v.shape[1])), constant_values=fill)


def kernel(x, mem, mix_norm_w, w_in, da_q_norm_w, da_k_norm_w, da_lambda_q1, da_lambda_k1, da_lambda_q2, da_lambda_k2, da_subln_w, ssd_conv_w, ssd_conv_b, ssd_dt_bias, ssd_a_log, ssd_d, ssd_norm_w, w_out, xattn_norm_w, mem_norm_w, xattn_w_q, xattn_w_kv, xattn_q_norm_w, xattn_k_norm_w, xattn_w_o, ffn_norm_w, ffn_w_gate, ffn_w_up, ffn_w_down):
    b, s, dm = x.shape
    mlen = mem.shape[1]
    depth = w_in.shape[0]
    da_w = DA_HEADS * DA_V_DIM
    ssd_w = ssd_norm_w.shape[1]
    ssd_heads = ssd_w // SSD_HEAD_DIM
    gn = SSD_GROUPS * SSD_STATE
    xw = X_HEADS * X_HEAD_DIM
    o_q, o_k, o_v, o_z = 0, da_w, 2 * da_w, 3 * da_w
    o_xs = o_z + ssd_w
    o_bc = o_xs + ssd_w
    o_dt = o_bc + 2 * gn
    z_col, xs_col = 0, 1
    bc_col = (2 * ssd_w) // (2 * gn)
    q_col = (2 * ssd_w + 2 * gn) // DA_V_DIM
    k_col = q_col + DA_HEADS
    v_col = k_col + DA_HEADS
    assert (2 * ssd_w) % (2 * gn) == 0 and ssd_heads <= LANES

    expand = np.zeros((LANES, ssd_w), np.float32)
    for h in range(ssd_heads):
        expand[h, h * SSD_HEAD_DIM:(h + 1) * SSD_HEAD_DIM] = 1.0
    expand = jnp.asarray(expand, BF16)

    h = x.reshape(b * s, dm)
    mem2 = mem.reshape(b * mlen, dm)
    for i in range(depth):
        lambda_init = 0.8 - 0.6 * math.exp(-0.3 * i)
        proj, dt_raw = _in_proj(h, mix_norm_w[i][None], jnp.swapaxes(w_in, 1, 2), i, o_dt, o_z)
        proj3 = proj.reshape(b, s, -1)

        lam_p = jnp.stack([da_lambda_q1[i], da_lambda_k1[i], da_lambda_q2[i], da_lambda_k2[i]])
        a_out, (wo_b, wq_b, xwo_b, wg_b, wu_b, wd_b) = _diff_attn(
            proj3, jnp.tile(da_q_norm_w[i], 2)[None], jnp.tile(da_k_norm_w[i], 2)[None],
            lam_p, da_subln_w[i][None], lambda_init, q_col, k_col, v_col,
            [w_out, xattn_w_q, xattn_w_o, ffn_w_gate, ffn_w_up, ffn_w_down], i)

        cw, cb = ssd_conv_w[i], ssd_conv_b[i][None]
        s_out = _ssd(proj3, dt_raw.reshape(b, s, LANES),
                     cw[:, :ssd_w], cb[:, :ssd_w], cw[:, ssd_w:], cb[:, ssd_w:],
                     _pad_lanes(ssd_dt_bias[i][None]), _pad_lanes(ssd_a_log[i][None]),
                     jnp.repeat(ssd_d[i], SSD_HEAD_DIM)[None], ssd_norm_w[i][None], expand,
                     z_col, xs_col, bc_col)

        xkv = _xkv_proj(mem2, mem_norm_w[i][None], xattn_w_kv, i, xattn_k_norm_w[i][None])
        h2 = _mix_out(a_out.reshape(b * s, da_w), s_out.reshape(b * s, ssd_w), h, wo_b,
                      xattn_norm_w[i][None], wq_b, xkv.reshape(b, mlen, 2 * xw), xattn_q_norm_w[i][None], xwo_b, s)

        h = _ffn(h2, ffn_norm_w[i][None], wg_b, wu_b, wd_b)
    return h.reshape(b, s, dm)
```

```python
import functools
import math

import jax
import jax.numpy as jnp
import numpy as np
from jax import lax
from jax.experimental import pallas as pl
from jax.experimental.pallas import tpu as pltpu

EPS = 1e-6
LANES = 128
SUBLANES = 8
VMEM_LIMIT = 48 * 1024 * 1024
BIG_VMEM_LIMIT = 56 * 1024 * 1024
LOG2E = math.log2(math.e)
DEN_ROWS = 16
CONV_TAIL = 16

DA_HEADS = 8
DA_HEAD_DIM = 64
DA_V_DIM = 128
SSD_HEAD_DIM = 64
SSD_GROUPS = 4
SSD_STATE = 128
SSD_CONV = 4
SSD_CHUNK = 128
X_HEADS = 4
X_HEAD_DIM = 128

F32 = jnp.float32
BF16 = jnp.bfloat16


def _rms_scale(xf, eps=EPS):
    return lax.rsqrt(jnp.mean(xf * xf, axis=-1, keepdims=True) + eps)


def _silu(x):
    h = 0.5 * x
    return h + h * jnp.tanh(h)


def _split_bf16(x, parts):
    out = []
    r = x
    for _ in range(parts):
        p = r.astype(BF16)
        out.append(p)
        r = r - p.astype(F32)
    return out


def _dot(a, b):
    return jnp.dot(a, b, preferred_element_type=F32)


def _dot_nt(a, b):
    return lax.dot_general(a, b, (((1,), (1,)), ((), ())), preferred_element_type=F32)


def _lane_group_mean(xf, avg):
    hi, lo = _split_bf16(xf, 2)
    return _dot(hi, avg) + _dot(lo, avg)


def _in_proj_kernel(x_ref, nw_ref, wt_ref, wdt_ref, o_ref, dt_ref, hn_ref):
    n, mi = pl.program_id(1), pl.program_id(2)

    @pl.when(n == 0)
    def _():
        xf = x_ref[...]
        hn = (xf * _rms_scale(xf) * nw_ref[...]).astype(BF16)
        hn_ref[mi] = hn
        wdt = wdt_ref[...].astype(BF16)
        wdt = jnp.concatenate([wdt, jnp.zeros((LANES - wdt.shape[0], wdt.shape[1]), BF16)], axis=0)
        dt_ref[...] = _dot_nt(hn, wdt)

    o_ref[...] = _dot_nt(hn_ref[mi], wt_ref[...].astype(BF16)).astype(o_ref.dtype)


def _in_proj(x2, nw, wt_all, layer, n, col_shift, tm=1024, tn=1024, m_inner=2):
    m, k = x2.shape
    n_dt = wt_all.shape[1] - n
    shift, nblk = col_shift // tn, n // tn
    assert col_shift % tn == 0 and n % tn == 0 and n % n_dt == 0 and m % (tm * m_inner) == 0
    row = lambda mo, j, mi: mo * m_inner + jnp.where(j == 0, mi, m_inner - 1)
    return pl.pallas_call(
        _in_proj_kernel,
        out_shape=(jax.ShapeDtypeStruct((m, n), BF16), jax.ShapeDtypeStruct((m, LANES), F32)),
        grid=(m // (tm * m_inner), nblk, m_inner),
        in_specs=[
            pl.BlockSpec((tm, k), lambda mo, j, mi: (row(mo, j, mi), 0)),
            pl.BlockSpec((1, k), lambda mo, j, mi: (0, 0)),
            pl.BlockSpec((None, tn, k), lambda mo, j, mi: (layer, (j + shift) % nblk, 0)),
            pl.BlockSpec((None, n_dt, k), lambda mo, j, mi: (layer, n // n_dt, 0)),
        ],
        out_specs=(
            pl.BlockSpec((tm, tn), lambda mo, j, mi: (mo * m_inner + mi, j)),
            pl.BlockSpec((tm, LANES), lambda mo, j, mi: (row(mo, j, mi), 0)),
        ),
        scratch_shapes=[pltpu.VMEM((m_inner, tm, k), BF16)],
        compiler_params=pltpu.CompilerParams(
            dimension_semantics=("parallel", "arbitrary", "arbitrary"), vmem_limit_bytes=BIG_VMEM_LIMIT),
        name="in_proj",
    )(x2, nw, wt_all, wt_all)


def _xkv_kernel(x_ref, nw_ref, w_ref, kw_ref, o_ref):
    d = X_HEAD_DIM
    xf = x_ref[...]
    hn = (xf * _rms_scale(xf) * nw_ref[...]).astype(BF16)
    kv = _dot(hn, w_ref[...].astype(BF16))
    kw = X_HEADS * d
    for hh in range(X_HEADS):
        kf = kv[:, hh * d:(hh + 1) * d]
        o_ref[:, hh * d:(hh + 1) * d] = (kf * _rms_scale(kf) * kw_ref[...]).astype(o_ref.dtype)
    o_ref[:, kw:] = kv[:, kw:].astype(o_ref.dtype)


def _xkv_proj(x2, nw, w_all, layer, kw, tm=256):
    m, k = x2.shape
    n = w_all.shape[2]
    return pl.pallas_call(
        _xkv_kernel,
        out_shape=jax.ShapeDtypeStruct((m, n), BF16),
        grid=(m // tm,),
        in_specs=[
            pl.BlockSpec((tm, k), lambda i: (i, 0)),
            pl.BlockSpec((1, k), lambda i: (0, 0)),
            pl.BlockSpec((None, k, n), lambda i: (layer, 0, 0)),
            pl.BlockSpec((1, X_HEAD_DIM), lambda i: (0, 0)),
        ],
        out_specs=pl.BlockSpec((tm, n), lambda i: (i, 0)),
        compiler_params=pltpu.CompilerParams(
            dimension_semantics=("parallel",), vmem_limit_bytes=VMEM_LIMIT),
        name="xkv_proj",
    )(x2, nw, w_all, kw)


def _diff_attn_kernel(q_ref, k_ref, v_ref, qw_ref, kw_ref, lam_ref, sw_ref, *rest,
                      tq, hb, lambda_init, n_cast):
    cast_in, o_ref, cast_out = rest[:n_cast], rest[n_cast], rest[n_cast + 1:2 * n_cast + 1]
    kn_ref, vt_ref, qcat_ref, m_ref, acc_ref, s_ref = rest[2 * n_cast + 1:]
    for w_ref, wb_ref in zip(cast_in, cast_out):
        wb_ref[...] = w_ref[...].astype(BF16)
    i = pl.program_id(2)
    n_tiles = vt_ref.shape[1]
    d = DA_V_DIM
    lane = lax.broadcasted_iota(jnp.int32, (1, d), 1)
    first = lane < DA_HEAD_DIM
    ones_row = lax.broadcasted_iota(jnp.int32, (DEN_ROWS, tq), 0) == 0

    half_r = lax.broadcasted_iota(jnp.int32, (d, d), 0) < DA_HEAD_DIM
    half_c = lax.broadcasted_iota(jnp.int32, (d, d), 1) < DA_HEAD_DIM
    sub_avg = jnp.where(half_r == half_c, 1.0 / DA_HEAD_DIM, 0.0).astype(BF16)

    def sub_head_norm(xf):
        return xf * lax.rsqrt(_lane_group_mean(xf * xf, sub_avg) + EPS)

    @pl.when(i == 0)
    def _():
        for hh in range(hb):
            hs = slice(hh * d, (hh + 1) * d)
            kn_ref[hh] = (sub_head_norm(k_ref[:, hs].astype(F32)) * kw_ref[...]).astype(BF16)
            for t in range(n_tiles):
                vt_ref[hh, t, :d] = v_ref[t * tq:(t + 1) * tq, hs].astype(F32).T.astype(BF16)
                vt_ref[hh, t, d:] = jnp.where(ones_row, 1.0, 0.0).astype(BF16)

    for hh in range(hb):
        qn = sub_head_norm(q_ref[:, hh * d:(hh + 1) * d].astype(F32)) * qw_ref[...] * (DA_HEAD_DIM ** -0.5 * LOG2E)
        qcat_ref[hh] = jnp.concatenate(
            [jnp.where(first, qn, 0.0), jnp.where(first, 0.0, qn)], axis=0).astype(BF16)
    m_ref[...] = jnp.full(m_ref.shape, -jnp.inf, F32)
    acc_ref[...] = jnp.zeros(acc_ref.shape, F32)

    def scores(j, hh):
        start = pl.multiple_of(j * tq, tq)
        return _dot_nt(kn_ref[hh, pl.ds(start, tq), :], qcat_ref[hh])

    def tile(j, cur, last):
        if last:
            key = lax.broadcasted_iota(jnp.int32, (tq, 2 * tq), 0)
            qry = lax.broadcasted_iota(jnp.int32, (tq, 2 * tq), 1)
            causal = key <= jnp.where(qry >= tq, qry - tq, qry)
        for hh in range(hb):
            if not last:
                s_ref[1 - cur, hh] = scores(j + 1, hh)
            s = s_ref[cur, hh]
            if last:
                s = jnp.where(causal, s, -jnp.inf)
            m_prev = m_ref[hh]
            m_new = jnp.maximum(m_prev, jnp.max(s, axis=0, keepdims=True))
            alpha = jnp.exp2(m_prev - m_new)
            p = jnp.exp2(s - m_new)
            acc_ref[hh] = alpha * acc_ref[hh] + _dot(vt_ref[hh, j], p.astype(BF16))
            m_ref[hh] = m_new

    for hh in range(hb):
        s_ref[0, hh] = scores(0, hh)

    def body(jj, carry):
        tile(2 * jj, 0, False)
        tile(2 * jj + 1, 1, False)
        return carry

    lax.fori_loop(0, i // 2, body, 0)

    @pl.when(i % 2 == 0)
    def _():
        tile(i, 0, True)

    @pl.when(i % 2 == 1)
    def _():
        tile(i - 1, 0, False)
        tile(i, 1, True)

    lam_p = lam_ref[...]
    lam = (jnp.exp(jnp.sum(lam_p[0:1] * lam_p[1:2], axis=-1, keepdims=True))
           - jnp.exp(jnp.sum(lam_p[2:3] * lam_p[3:4], axis=-1, keepdims=True)) + lambda_init)
    for hh in range(hb):
        ot = acc_ref[hh, :d] / acc_ref[hh, d:d + 1]
        o = (ot[:, :tq] - lam * ot[:, tq:]).T
        o = o * _rms_scale(o) * sw_ref[...] * (1.0 - lambda_init)
        o_ref[:, hh * d:(hh + 1) * d] = o.astype(o_ref.dtype)


def _diff_attn(proj3, qw2, kw2, lam_p, sw, lambda_init, q_col, k_col, v_col, cast_ws, layer, tq=256, hb=8):
    b, s, _ = proj3.shape
    d = DA_V_DIM
    nh, nq = DA_HEADS // hb, s // tq
    steps = b * nh * nq
    kern = functools.partial(_diff_attn_kernel, tq=tq, hb=hb, lambda_init=lambda_init, n_cast=len(cast_ws))
    vec = lambda shape: pl.BlockSpec(shape, lambda bi, h, i: (0, 0))
    cast_in, cast_out, cast_shapes = [], [], []
    for w in cast_ws:
        rows, cols = w.shape[1:]
        rb = rows // steps
        assert rows % steps == 0 and rb % (2 * SUBLANES) == 0
        cast_in.append(pl.BlockSpec((None, rb, cols), lambda bi, h, i: (layer, (bi * nh + h) * nq + i, 0)))
        cast_out.append(pl.BlockSpec((rb, cols), lambda bi, h, i: ((bi * nh + h) * nq + i, 0)))
        cast_shapes.append(jax.ShapeDtypeStruct((rows, cols), BF16))
    outs = pl.pallas_call(
        kern,
        out_shape=[jax.ShapeDtypeStruct((b, s, DA_HEADS * d), BF16)] + cast_shapes,
        grid=(b, nh, nq),
        in_specs=[
            pl.BlockSpec((None, tq, hb * d), lambda bi, h, i: (bi, i, q_col // hb + h)),
            pl.BlockSpec((None, s, hb * d), lambda bi, h, i: (bi, 0, k_col // hb + h), pipeline_mode=pl.Buffered(1)),
            pl.BlockSpec((None, s, hb * d), lambda bi, h, i: (bi, 0, v_col // hb + h), pipeline_mode=pl.Buffered(1)),
            vec((1, d)), vec((1, d)), vec((4, DA_HEAD_DIM)), vec((1, d)),
        ] + cast_in,
        out_specs=[pl.BlockSpec((None, tq, hb * d), lambda bi, h, i: (bi, i, h))] + cast_out,
        scratch_shapes=[
            pltpu.VMEM((hb, s, d), BF16),
            pltpu.VMEM((hb, s // tq, d + DEN_ROWS, tq), BF16),
            pltpu.VMEM((hb, 2 * tq, d), BF16),
            pltpu.VMEM((hb, 1, 2 * tq), F32),
            pltpu.VMEM((hb, d + DEN_ROWS, 2 * tq), F32),
            pltpu.VMEM((2, hb, tq, 2 * tq), F32),
        ],
        compiler_params=pltpu.CompilerParams(
            dimension_semantics=("parallel", "parallel", "arbitrary"), vmem_limit_bytes=BIG_VMEM_LIMIT),
        name="diff_attn",
    )(proj3, proj3, proj3, qw2, kw2, lam_p, sw, *cast_ws)
    return outs[0], outs[1:]


def _ssd_kernel(z_ref, xs_ref, bc_ref, dt_ref, cwx_ref, cbx_ref, cwbc_ref, cbbc_ref,
                dtb_ref, alog_ref, dx_ref, nw_ref, e_ref, o_ref,
                state_ref, tailx_ref, tailbc_ref, y_ref):
    L, N, G = SSD_CHUNK, SSD_STATE, SSD_GROUPS
    gw = o_ref.shape[-1] // G
    c = pl.program_id(1)

    @pl.when(c == 0)
    def _():
        state_ref[...] = jnp.zeros(state_ref.shape, F32)
        tailx_ref[...] = jnp.zeros(tailx_ref.shape, BF16)
        tailbc_ref[...] = jnp.zeros(tailbc_ref.shape, BF16)

    ext = CONV_TAIL + L
    s_row = lax.broadcasted_iota(jnp.int32, (L, SSD_CONV * ext), 0)
    s_col = lax.broadcasted_iota(jnp.int32, (L, SSD_CONV * ext), 1)
    pick = jnp.zeros((L, SSD_CONV * ext), F32)
    for k in range(SSD_CONV):
        pick = jnp.where(s_col == s_row + (k * ext + CONV_TAIL - k), 1.0, pick)
    pick = pick.astype(BF16)

    def conv_silu(cur, tail_ref, w_ref, b_ref):
        rows = jnp.concatenate([tail_ref[...], cur], axis=0)
        wb = w_ref[...].astype(BF16)
        taps = jnp.concatenate([rows * wb[SSD_CONV - 1 - k:SSD_CONV - k, :] for k in range(SSD_CONV)], axis=0)
        tail_ref[...] = cur[L - CONV_TAIL:, :]
        return _silu(_dot(pick, taps) + b_ref[...])

    xs_c = conv_silu(xs_ref[...], tailx_ref, cwx_ref, cbx_ref)
    bc_c = conv_silu(bc_ref[...], tailbc_ref, cwbc_ref, cbbc_ref)

    dtx = dt_ref[...] + dtb_ref[...]
    dt = jnp.maximum(dtx, 0.0) + jnp.log1p(jnp.exp(-jnp.abs(dtx)))
    a = -jnp.exp(alog_ref[...]) * LOG2E
    da = dt * a
    r_i = lax.broadcasted_iota(jnp.int32, (L, L), 0)
    c_i = lax.broadcasted_iota(jnp.int32, (L, L), 1)
    causal = c_i <= r_i
    tril = jnp.where(causal, 1.0, 0.0).astype(BF16)
    acs = sum(_dot(tril, p) for p in _split_bf16(da, 3))
    acs_t = acs.T

    lane = lax.broadcasted_iota(jnp.int32, (1, LANES), 1)
    n_heads = e_ref.shape[1] // SSD_HEAD_DIM

    def expand(v, parts):
        packed, r = None, jnp.where(lane < n_heads, v, 0.0)
        for t in range(parts):
            p = r.astype(BF16).astype(F32)
            packed = p if t == 0 else packed + pltpu.roll(p, t * n_heads, 1)
            r = r - p
        return _dot(packed.astype(BF16), e_ref[...])

    dt_x = expand(dt, 2)
    acs_x = expand(acs, 3)
    last_x = acs_x[L - 1:L, :]
    xdt_f = xs_c * dt_x
    xds = (xdt_f * jnp.exp2(last_x - acs_x)).astype(BF16)
    expacs_x = jnp.exp2(acs_x)
    cdecay_x = jnp.exp2(last_x)

    first = lane < SSD_HEAD_DIM
    heads_per_group = gw // SSD_HEAD_DIM

    for g in range(G):
        cols = slice(g * gw, (g + 1) * gw)
        b_g = bc_c[:, g * N:(g + 1) * N]
        c_g = bc_c[:, (G + g) * N:(G + g + 1) * N].astype(BF16)
        cb = _dot_nt(c_g, b_g.astype(BF16))
        state_old = state_ref[:, cols]
        y_off = _dot(c_g, state_old.astype(BF16)) * expacs_x[:, cols]
        state_ref[:, cols] = state_old * cdecay_x[:, cols] + _dot(b_g.T.astype(BF16), xds[:, cols])
        for pair in range(heads_per_group // 2):
            h0 = g * heads_per_group + 2 * pair
            pc = slice(g * gw + pair * LANES, g * gw + (pair + 1) * LANES)
            xp = xdt_f[:, pc]
            x_pair = jnp.concatenate([jnp.where(first, xp, 0.0), jnp.where(first, 0.0, xp)], axis=0).astype(BF16)
            mats = []
            for h in (h0, h0 + 1):
                seg = acs[:, h:h + 1] - acs_t[h:h + 1, :]
                dec = jnp.exp2(jnp.where(causal, seg, -jnp.inf))
                mats.append((cb * dec).astype(BF16))
            y_ref[:, pc] = _dot(jnp.concatenate(mats, axis=1), x_pair)
        y = y_ref[:, cols] + y_off + xs_c[:, cols] * dx_ref[:, cols]
        y = y * _silu(z_ref[:, cols].astype(F32))
        o_ref[:, cols] = (y * _rms_scale(y) * nw_ref[:, cols]).astype(o_ref.dtype)


def _ssd(proj3, dt3, cwx, cbx, cwbc, cbbc, dtb, alog, dx, nw, e, z_col, xs_col, bc_col):
    b, s, _ = proj3.shape
    w = e.shape[1]
    bcw = cwbc.shape[1]
    L = SSD_CHUNK
    full = lambda shape: pl.BlockSpec(shape, lambda bi, c: (0, 0))
    return pl.pallas_call(
        _ssd_kernel,
        out_shape=jax.ShapeDtypeStruct((b, s, w), BF16),
        grid=(b, s // L),
        in_specs=[
            pl.BlockSpec((None, L, w), lambda bi, c: (bi, c, z_col)),
            pl.BlockSpec((None, L, w), lambda bi, c: (bi, c, xs_col)),
            pl.BlockSpec((None, L, bcw), lambda bi, c: (bi, c, bc_col)),
            pl.BlockSpec((None, L, LANES), lambda bi, c: (bi, c, 0)),
            full((SSD_CONV, w)), full((1, w)), full((SSD_CONV, bcw)), full((1, bcw)),
            full((1, LANES)), full((1, LANES)), full((1, w)), full((1, w)), full((LANES, w)),
        ],
        out_specs=pl.BlockSpec((None, L, w), lambda bi, c: (bi, c, 0)),
        scratch_shapes=[
            pltpu.VMEM((SSD_STATE, w), F32),
            pltpu.VMEM((CONV_TAIL, w), BF16),
            pltpu.VMEM((CONV_TAIL, bcw), BF16),
            pltpu.VMEM((L, w), F32),
        ],
        compiler_params=pltpu.CompilerParams(
            dimension_semantics=("parallel", "arbitrary"), vmem_limit_bytes=VMEM_LIMIT),
        name="ssd",
    )(proj3, proj3, proj3, dt3, cwx, cbx, cwbc, cbbc, dtb, alog, dx, nw, e)


def _mix_out_kernel(a_ref, s_ref, x_ref, w_ref, xnw_ref, wq_ref, kv_ref, qw_ref, wo_ref, o_ref):
    d = X_HEAD_DIM
    ka = a_ref.shape[1]
    h1 = x_ref[...] + _dot(a_ref[...], w_ref[0])
    for c in range(1, w_ref.shape[0]):
        h1 = h1 + _dot(s_ref[:, (c - 1) * ka:c * ka], w_ref[c])
    hn = (h1 * _rms_scale(h1) * xnw_ref[...]).astype(BF16)
    q = _dot(hn, wq_ref[...])
    outs = []
    for hh in range(X_HEADS):
        qf = q[:, hh * d:(hh + 1) * d]
        qn = (qf * _rms_scale(qf) * qw_ref[...] * (d ** -0.5 * LOG2E)).astype(BF16)
        s = _dot_nt(qn, kv_ref[:, hh * d:(hh + 1) * d])
        p = jnp.exp2(s - jnp.max(s, axis=-1, keepdims=True))
        v = kv_ref[:, (X_HEADS + hh) * d:(X_HEADS + hh + 1) * d]
        o = _dot(p.astype(BF16), v) / jnp.sum(p, axis=-1, keepdims=True)
        outs.append(o.astype(BF16))
    o_ref[...] = h1 + _dot(jnp.concatenate(outs, axis=-1), wo_ref[...])


def _mix_out(a2, s2, x2, w_out, xnw, wq, kv3, qw, wo, seq, tm=512):
    m, ka = a2.shape
    ks = s2.shape[1]
    dm = x2.shape[1]
    xw = wq.shape[1]
    mlen = kv3.shape[1]
    assert seq % tm == 0 and ks % ka == 0
    once = dict(pipeline_mode=pl.Buffered(1))
    return pl.pallas_call(
        _mix_out_kernel,
        out_shape=jax.ShapeDtypeStruct((m, dm), F32),
        grid=(m // tm,),
        in_specs=[
            pl.BlockSpec((tm, ka), lambda i: (i, 0)),
            pl.BlockSpec((tm, ks), lambda i: (i, 0)),
            pl.BlockSpec((tm, dm), lambda i: (i, 0)),
            pl.BlockSpec((1 + ks // ka, ka, dm), lambda i: (0, 0, 0), **once),
            pl.BlockSpec((1, dm), lambda i: (0, 0)),
            pl.BlockSpec((dm, xw), lambda i: (0, 0), **once),
            pl.BlockSpec((None, mlen, 2 * xw), lambda i: (i // (seq // tm), 0, 0)),
            pl.BlockSpec((1, X_HEAD_DIM), lambda i: (0, 0)),
            pl.BlockSpec((xw, dm), lambda i: (0, 0), **once),
        ],
        out_specs=pl.BlockSpec((tm, dm), lambda i: (i, 0)),
        compiler_params=pltpu.CompilerParams(
            dimension_semantics=("parallel",), vmem_limit_bytes=BIG_VMEM_LIMIT),
        name="mix_out",
    )(a2, s2, x2, w_out.reshape(1 + ks // ka, ka, dm), xnw, wq, kv3, qw, wo)


def _ffn_kernel(h_ref, nw_ref, wg_ref, wu_ref, wd_ref, o_ref, hn_ref):
    f = pl.program_id(1)

    @pl.when(f == 0)
    def _():
        hf = h_ref[...]
        hn_ref[...] = (hf * _rms_scale(hf) * nw_ref[...]).astype(BF16)
        o_ref[...] = hf

    hn = hn_ref[...]
    act = (_silu(_dot(hn, wg_ref[...])) * _dot(hn, wu_ref[...])).astype(BF16)
    o_ref[...] += _dot(act, wd_ref[...])


def _ffn(h2, nw, wg, wu, wd, tm=1024, tf=512):
    m, k = h2.shape
    dff = wg.shape[1]
    return pl.pallas_call(
        _ffn_kernel,
        out_shape=jax.ShapeDtypeStruct((m, k), F32),
        grid=(m // tm, dff // tf),
        in_specs=[
            pl.BlockSpec((tm, k), lambda i, f: (i, 0), pipeline_mode=pl.Buffered(1)),
            pl.BlockSpec((1, k), lambda i, f: (0, 0)),
            pl.BlockSpec((k, tf), lambda i, f: (0, f)),
            pl.BlockSpec((k, tf), lambda i, f: (0, f)),
            pl.BlockSpec((tf, k), lambda i, f: (f, 0)),
        ],
        out_specs=pl.BlockSpec((tm, k), lambda i, f: (i, 0)),
        scratch_shapes=[pltpu.VMEM((tm, k), BF16)],
        compiler_params=pltpu.CompilerParams(
            dimension_semantics=("parallel", "arbitrary"), vmem_limit_bytes=BIG_VMEM_LIMIT),
        name="ffn",
    )(h2, nw, wg, wu, wd)


def _pad_lanes(v, fill=0.0):
    return jnp.pad(v, ((0, 0), (0, LANES - v.shape[1])), constant_values=fill)


def kernel(x, mem, mix_norm_w, w_in, da_q_norm_w, da_k_norm_w, da_lambda_q1, da_lambda_k1, da_lambda_q2, da_lambda_k2, da_subln_w, ssd_conv_w, ssd_conv_b, ssd_dt_bias, ssd_a_log, ssd_d, ssd_norm_w, w_out, xattn_norm_w, mem_norm_w, xattn_w_q, xattn_w_kv, xattn_q_norm_w, xattn_k_norm_w, xattn_w_o, ffn_norm_w, ffn_w_gate, ffn_w_up, ffn_w_down):
    b, s, dm = x.shape
    mlen = mem.shape[1]
    depth = w_in.shape[0]
    da_w = DA_HEADS * DA_V_DIM
    ssd_w = ssd_norm_w.shape[1]
    ssd_heads = ssd_w // SSD_HEAD_DIM
    gn = SSD_GROUPS * SSD_STATE
    xw = X_HEADS * X_HEAD_DIM
    o_q, o_k, o_v, o_z = 0, da_w, 2 * da_w, 3 * da_w
    o_xs = o_z + ssd_w
    o_bc = o_xs + ssd_w
    o_dt = o_bc + 2 * gn
    z_col, xs_col = 0, 1
    bc_col = (2 * ssd_w) // (2 * gn)
    q_col = (2 * ssd_w + 2 * gn) // DA_V_DIM
    k_col = q_col + DA_HEADS
    v_col = k_col + DA_HEADS
    assert (2 * ssd_w) % (2 * gn) == 0 and ssd_heads <= LANES

    assert LANES % ssd_heads == 0 and LANES // ssd_heads >= 3
    expand = np.zeros((LANES, ssd_w), np.float32)
    for r in range(LANES):
        h = r % ssd_heads
        expand[r, h * SSD_HEAD_DIM:(h + 1) * SSD_HEAD_DIM] = 1.0
    expand = jnp.asarray(expand, BF16)

    h = x.reshape(b * s, dm)
    mem2 = mem.reshape(b * mlen, dm)
    for i in range(depth):
        lambda_init = 0.8 - 0.6 * math.exp(-0.3 * i)
        proj, dt_raw = _in_proj(h, mix_norm_w[i][None], jnp.swapaxes(w_in, 1, 2), i, o_dt, o_z)
        proj3 = proj.reshape(b, s, -1)

        lam_p = jnp.stack([da_lambda_q1[i], da_lambda_k1[i], da_lambda_q2[i], da_lambda_k2[i]])
        a_out, (wo_b, wq_b, xwo_b, wg_b, wu_b, wd_b) = _diff_attn(
            proj3, jnp.tile(da_q_norm_w[i], 2)[None], jnp.tile(da_k_norm_w[i], 2)[None],
            lam_p, da_subln_w[i][None], lambda_init, q_col, k_col, v_col,
            [w_out, xattn_w_q, xattn_w_o, ffn_w_gate, ffn_w_up, ffn_w_down], i)

        cw, cb = ssd_conv_w[i], ssd_conv_b[i][None]
        s_out = _ssd(proj3, dt_raw.reshape(b, s, LANES),
                     cw[:, :ssd_w], cb[:, :ssd_w], cw[:, ssd_w:], cb[:, ssd_w:],
                     _pad_lanes(ssd_dt_bias[i][None]), _pad_lanes(ssd_a_log[i][None]),
                     jnp.repeat(ssd_d[i], SSD_HEAD_DIM)[None], ssd_norm_w[i][None], expand,
                     z_col, xs_col, bc_col)

        xkv = _xkv_proj(mem2, mem_norm_w[i][None], xattn_w_kv, i, xattn_k_norm_w[i][None])
        h2 = _mix_out(a_out.reshape(b * s, da_w), s_out.reshape(b * s, ssd_w), h, wo_b,
                      xattn_norm_w[i][None], wq_b, xkv.reshape(b, mlen, 2 * xw), xattn_q_norm_w[i][None], xwo_b, s)

        h = _ffn(h2, ffn_norm_w[i][None], wg_b, wu_b, wd_b)
    return h.reshape(b, s, dm)
```

```python
import functools
import math

import jax
import jax.numpy as jnp
import numpy as np
from jax import lax
from jax.experimental import pallas as pl
from jax.experimental.pallas import tpu as pltpu

EPS = 1e-6
LANES = 128
SUBLANES = 8
VMEM_LIMIT = 48 * 1024 * 1024
BIG_VMEM_LIMIT = 56 * 1024 * 1024
LOG2E = math.log2(math.e)
DEN_ROWS = 16
CONV_TAIL = 16

DA_HEADS = 8
DA_HEAD_DIM = 64
DA_V_DIM = 128
SSD_HEAD_DIM = 64
SSD_GROUPS = 4
SSD_STATE = 128
SSD_CONV = 4
SSD_CHUNK = 128
X_HEADS = 4
X_HEAD_DIM = 128

F32 = jnp.float32
BF16 = jnp.bfloat16


def _rms_scale(xf, eps=EPS):
    return lax.rsqrt(jnp.mean(xf * xf, axis=-1, keepdims=True) + eps)


def _silu(x):
    h = 0.5 * x
    return h + h * jnp.tanh(h)


def _split_bf16(x, parts):
    out = []
    r = x
    for _ in range(parts):
        p = r.astype(BF16)
        out.append(p)
        r = r - p.astype(F32)
    return out


def _dot(a, b):
    return jnp.dot(a, b, preferred_element_type=F32)


def _dot_nt(a, b):
    return lax.dot_general(a, b, (((1,), (1,)), ((), ())), preferred_element_type=F32)


def _lane_group_mean(xf, avg):
    hi, lo = _split_bf16(xf, 2)
    return _dot(hi, avg) + _dot(lo, avg)


def _in_proj_kernel(x_ref, nw_ref, wt_ref, wdt_ref, o_ref, dt_ref, hn_ref):
    n, mi = pl.program_id(1), pl.program_id(2)

    @pl.when(n == 0)
    def _():
        xf = x_ref[...]
        hn = (xf * _rms_scale(xf) * nw_ref[...]).astype(BF16)
        hn_ref[mi] = hn
        wdt = wdt_ref[...].astype(BF16)
        wdt = jnp.concatenate([wdt, jnp.zeros((LANES - wdt.shape[0], wdt.shape[1]), BF16)], axis=0)
        dt_ref[...] = _dot_nt(hn, wdt)

    o_ref[...] = _dot_nt(hn_ref[mi], wt_ref[...].astype(BF16)).astype(o_ref.dtype)


def _in_proj(x2, nw, wt_all, layer, n, col_shift, tm=1024, tn=1024, m_inner=2):
    m, k = x2.shape
    n_dt = wt_all.shape[1] - n
    shift, nblk = col_shift // tn, n // tn
    assert col_shift % tn == 0 and n % tn == 0 and n % n_dt == 0 and m % (tm * m_inner) == 0
    row = lambda mo, j, mi: mo * m_inner + jnp.where(j == 0, mi, m_inner - 1)
    return pl.pallas_call(
        _in_proj_kernel,
        out_shape=(jax.ShapeDtypeStruct((m, n), BF16), jax.ShapeDtypeStruct((m, LANES), F32)),
        grid=(m // (tm * m_inner), nblk, m_inner),
        in_specs=[
            pl.BlockSpec((tm, k), lambda mo, j, mi: (row(mo, j, mi), 0)),
            pl.BlockSpec((1, k), lambda mo, j, mi: (0, 0)),
            pl.BlockSpec((None, tn, k), lambda mo, j, mi: (layer, (j + shift) % nblk, 0)),
            pl.BlockSpec((None, n_dt, k), lambda mo, j, mi: (layer, n // n_dt, 0)),
        ],
        out_specs=(
            pl.BlockSpec((tm, tn), lambda mo, j, mi: (mo * m_inner + mi, j)),
            pl.BlockSpec((tm, LANES), lambda mo, j, mi: (row(mo, j, mi), 0)),
        ),
        scratch_shapes=[pltpu.VMEM((m_inner, tm, k), BF16)],
        compiler_params=pltpu.CompilerParams(
            dimension_semantics=("parallel", "arbitrary", "arbitrary"), vmem_limit_bytes=BIG_VMEM_LIMIT),
        name="in_proj",
    )(x2, nw, wt_all, wt_all)


def _xkv_kernel(x_ref, nw_ref, w_ref, kw_ref, o_ref):
    d = X_HEAD_DIM
    xf = x_ref[...]
    hn = (xf * _rms_scale(xf) * nw_ref[...]).astype(BF16)
    kv = _dot(hn, w_ref[...].astype(BF16))
    kw = X_HEADS * d
    for hh in range(X_HEADS):
        kf = kv[:, hh * d:(hh + 1) * d]
        o_ref[:, hh * d:(hh + 1) * d] = (kf * _rms_scale(kf) * kw_ref[...]).astype(o_ref.dtype)
    o_ref[:, kw:] = kv[:, kw:].astype(o_ref.dtype)


def _xkv_proj(x2, nw, w_all, layer, kw, tm=256):
    m, k = x2.shape
    n = w_all.shape[2]
    return pl.pallas_call(
        _xkv_kernel,
        out_shape=jax.ShapeDtypeStruct((m, n), BF16),
        grid=(m // tm,),
        in_specs=[
            pl.BlockSpec((tm, k), lambda i: (i, 0)),
            pl.BlockSpec((1, k), lambda i: (0, 0)),
            pl.BlockSpec((None, k, n), lambda i: (layer, 0, 0)),
            pl.BlockSpec((1, X_HEAD_DIM), lambda i: (0, 0)),
        ],
        out_specs=pl.BlockSpec((tm, n), lambda i: (i, 0)),
        compiler_params=pltpu.CompilerParams(
            dimension_semantics=("parallel",), vmem_limit_bytes=VMEM_LIMIT),
        name="xkv_proj",
    )(x2, nw, w_all, kw)


def _diff_attn_kernel(q_ref, k_ref, v_ref, qw_ref, kw_ref, lam_ref, sw_ref, *rest,
                      tq, hb, lambda_init, n_cast):
    cast_in, o_ref, cast_out = rest[:n_cast], rest[n_cast], rest[n_cast + 1:2 * n_cast + 1]
    kn_ref, vt_ref, qcat_ref, m_ref, acc_ref, s_ref = rest[2 * n_cast + 1:]
    for w_ref, wb_ref in zip(cast_in, cast_out):
        wb_ref[...] = w_ref[...].astype(BF16)
    i = pl.program_id(2)
    n_tiles = vt_ref.shape[1]
    d = DA_V_DIM
    lane = lax.broadcasted_iota(jnp.int32, (1, d), 1)
    first = lane < DA_HEAD_DIM
    ones_row = lax.broadcasted_iota(jnp.int32, (DEN_ROWS, tq), 0) == 0

    half_r = lax.broadcasted_iota(jnp.int32, (d, d), 0) < DA_HEAD_DIM
    half_c = lax.broadcasted_iota(jnp.int32, (d, d), 1) < DA_HEAD_DIM
    sub_avg = jnp.where(half_r == half_c, 1.0 / DA_HEAD_DIM, 0.0).astype(BF16)

    def sub_head_norm(xf):
        return xf * lax.rsqrt(_lane_group_mean(xf * xf, sub_avg) + EPS)

    @pl.when(i == 0)
    def _():
        for hh in range(hb):
            hs = slice(hh * d, (hh + 1) * d)
            kn_ref[hh] = (sub_head_norm(k_ref[:, hs].astype(F32)) * kw_ref[...]).astype(BF16)
            for t in range(n_tiles):
                vt_ref[hh, t, :d] = v_ref[t * tq:(t + 1) * tq, hs].astype(F32).T.astype(BF16)
                vt_ref[hh, t, d:] = jnp.where(ones_row, 1.0, 0.0).astype(BF16)

    for hh in range(hb):
        qn = sub_head_norm(q_ref[:, hh * d:(hh + 1) * d].astype(F32)) * qw_ref[...] * (DA_HEAD_DIM ** -0.5 * LOG2E)
        qcat_ref[hh] = jnp.concatenate(
            [jnp.where(first, qn, 0.0).T, jnp.where(first, 0.0, qn).T], axis=1).astype(BF16)
    m_ref[...] = jnp.full(m_ref.shape, -jnp.inf, F32)
    acc_ref[...] = jnp.zeros(acc_ref.shape, F32)

    def scores(j, hh):
        start = pl.multiple_of(j * tq, tq)
        return _dot(kn_ref[hh, pl.ds(start, tq), :], qcat_ref[hh])

    def tile(j, cur, last):
        if last:
            key = lax.broadcasted_iota(jnp.int32, (tq, 2 * tq), 0)
            qry = lax.broadcasted_iota(jnp.int32, (tq, 2 * tq), 1)
            causal = key <= jnp.where(qry >= tq, qry - tq, qry)
        for hh in range(hb):
            if not last:
                s_ref[1 - cur, hh] = scores(j + 1, hh)
            s = s_ref[cur, hh]
            if last:
                s = jnp.where(causal, s, -jnp.inf)
            m_prev = m_ref[hh]
            m_new = jnp.maximum(m_prev, jnp.max(s, axis=0, keepdims=True))
            alpha = jnp.exp2(m_prev - m_new)
            p = jnp.exp2(s - m_new)
            acc_ref[hh] = alpha * acc_ref[hh] + _dot(vt_ref[hh, j], p.astype(BF16))
            m_ref[hh] = m_new

    for hh in range(hb):
        s_ref[0, hh] = scores(0, hh)

    def body(jj, carry):
        tile(2 * jj, 0, False)
        tile(2 * jj + 1, 1, False)
        return carry

    lax.fori_loop(0, i // 2, body, 0)

    @pl.when(i % 2 == 0)
    def _():
        tile(i, 0, True)

    @pl.when(i % 2 == 1)
    def _():
        tile(i - 1, 0, False)
        tile(i, 1, True)

    lam_p = lam_ref[...]
    lam = (jnp.exp(jnp.sum(lam_p[0:1] * lam_p[1:2], axis=-1, keepdims=True))
           - jnp.exp(jnp.sum(lam_p[2:3] * lam_p[3:4], axis=-1, keepdims=True)) + lambda_init)
    for hh in range(hb):
        ot = acc_ref[hh, :d] / acc_ref[hh, d:d + 1]
        o = (ot[:, :tq] - lam * ot[:, tq:]).T
        o = o * _rms_scale(o) * sw_ref[...] * (1.0 - lambda_init)
        o_ref[:, hh * d:(hh + 1) * d] = o.astype(o_ref.dtype)


def _diff_attn(proj3, qw2, kw2, lam_p, sw, lambda_init, q_col, k_col, v_col, cast_ws, layer, tq=256, hb=8):
    b, s, _ = proj3.shape
    d = DA_V_DIM
    nh, nq = DA_HEADS // hb, s // tq
    steps = b * nh * nq
    kern = functools.partial(_diff_attn_kernel, tq=tq, hb=hb, lambda_init=lambda_init, n_cast=len(cast_ws))
    vec = lambda shape: pl.BlockSpec(shape, lambda bi, h, i: (0, 0))
    cast_in, cast_out, cast_shapes = [], [], []
    for w in cast_ws:
        rows, cols = w.shape[1:]
        rb = rows // steps
        assert rows % steps == 0 and rb % (2 * SUBLANES) == 0
        cast_in.append(pl.BlockSpec((None, rb, cols), lambda bi, h, i: (layer, (bi * nh + h) * nq + i, 0)))
        cast_out.append(pl.BlockSpec((rb, cols), lambda bi, h, i: ((bi * nh + h) * nq + i, 0)))
        cast_shapes.append(jax.ShapeDtypeStruct((rows, cols), BF16))
    outs = pl.pallas_call(
        kern,
        out_shape=[jax.ShapeDtypeStruct((b, s, DA_HEADS * d), BF16)] + cast_shapes,
        grid=(b, nh, nq),
        in_specs=[
            pl.BlockSpec((None, tq, hb * d), lambda bi, h, i: (bi, i, q_col // hb + h)),
            pl.BlockSpec((None, s, hb * d), lambda bi, h, i: (bi, 0, k_col // hb + h), pipeline_mode=pl.Buffered(1)),
            pl.BlockSpec((None, s, hb * d), lambda bi, h, i: (bi, 0, v_col // hb + h), pipeline_mode=pl.Buffered(1)),
            vec((1, d)), vec((1, d)), vec((4, DA_HEAD_DIM)), vec((1, d)),
        ] + cast_in,
        out_specs=[pl.BlockSpec((None, tq, hb * d), lambda bi, h, i: (bi, i, h))] + cast_out,
        scratch_shapes=[
            pltpu.VMEM((hb, s, d), BF16),
            pltpu.VMEM((hb, s // tq, d + DEN_ROWS, tq), BF16),
            pltpu.VMEM((hb, d, 2 * tq), BF16),
            pltpu.VMEM((hb, 1, 2 * tq), F32),
            pltpu.VMEM((hb, d + DEN_ROWS, 2 * tq), F32),
            pltpu.VMEM((2, hb, tq, 2 * tq), F32),
        ],
        compiler_params=pltpu.CompilerParams(
            dimension_semantics=("parallel", "parallel", "arbitrary"), vmem_limit_bytes=BIG_VMEM_LIMIT),
        name="diff_attn",
    )(proj3, proj3, proj3, qw2, kw2, lam_p, sw, *cast_ws)
    return outs[0], outs[1:]


def _ssd_kernel(z_ref, xs_ref, bc_ref, dt_ref, cwx_ref, cbx_ref, cwbc_ref, cbbc_ref,
                dtb_ref, alog_ref, dx_ref, nw_ref, e_ref, o_ref,
                state_ref, tailx_ref, tailbc_ref, y_ref):
    L, N, G = SSD_CHUNK, SSD_STATE, SSD_GROUPS
    gw = o_ref.shape[-1] // G
    c = pl.program_id(1)

    @pl.when(c == 0)
    def _():
        state_ref[...] = jnp.zeros(state_ref.shape, F32)
        tailx_ref[...] = jnp.zeros(tailx_ref.shape, BF16)
        tailbc_ref[...] = jnp.zeros(tailbc_ref.shape, BF16)

    ext = CONV_TAIL + L
    s_row = lax.broadcasted_iota(jnp.int32, (L, SSD_CONV * ext), 0)
    s_col = lax.broadcasted_iota(jnp.int32, (L, SSD_CONV * ext), 1)
    pick = jnp.zeros((L, SSD_CONV * ext), F32)
    for k in range(SSD_CONV):
        pick = jnp.where(s_col == s_row + (k * ext + CONV_TAIL - k), 1.0, pick)
    pick = pick.astype(BF16)

    def conv_silu(cur, tail_ref, w_ref, b_ref):
        rows = jnp.concatenate([tail_ref[...], cur], axis=0)
        wb = w_ref[...].astype(BF16)
        taps = jnp.concatenate([rows * wb[SSD_CONV - 1 - k:SSD_CONV - k, :] for k in range(SSD_CONV)], axis=0)
        tail_ref[...] = cur[L - CONV_TAIL:, :]
        return _silu(_dot(pick, taps) + b_ref[...])

    xs_c = conv_silu(xs_ref[...], tailx_ref, cwx_ref, cbx_ref)
    bc_c = conv_silu(bc_ref[...], tailbc_ref, cwbc_ref, cbbc_ref)

    dtx = dt_ref[...] + dtb_ref[...]
    dt = jnp.maximum(dtx, 0.0) + jnp.log1p(jnp.exp(-jnp.abs(dtx)))
    a = -jnp.exp(alog_ref[...]) * LOG2E
    da = dt * a
    r_i = lax.broadcasted_iota(jnp.int32, (L, L), 0)
    c_i = lax.broadcasted_iota(jnp.int32, (L, L), 1)
    causal = c_i <= r_i
    tril = jnp.where(causal, 1.0, 0.0).astype(BF16)
    acs = sum(_dot(tril, p) for p in _split_bf16(da, 3))
    acs_t = acs.T

    lane = lax.broadcasted_iota(jnp.int32, (1, LANES), 1)
    n_heads = e_ref.shape[1] // SSD_HEAD_DIM

    def expand(v, parts):
        packed, r = None, jnp.where(lane < n_heads, v, 0.0)
        for t in range(parts):
            p = r.astype(BF16).astype(F32)
            packed = p if t == 0 else packed + pltpu.roll(p, t * n_heads, 1)
            r = r - p
        return _dot(packed.astype(BF16), e_ref[...])

    dt_x = expand(dt, 2)
    acs_x = expand(acs, 3)
    last_x = acs_x[L - 1:L, :]
    xdt_f = xs_c * dt_x
    xds = (xdt_f * jnp.exp2(last_x - acs_x)).astype(BF16)
    expacs_x = jnp.exp2(acs_x)
    cdecay_x = jnp.exp2(last_x)

    first = lane < SSD_HEAD_DIM
    heads_per_group = gw // SSD_HEAD_DIM

    for g in range(G):
        cols = slice(g * gw, (g + 1) * gw)
        b_g = bc_c[:, g * N:(g + 1) * N]
        c_g = bc_c[:, (G + g) * N:(G + g + 1) * N].astype(BF16)
        cb = _dot_nt(c_g, b_g.astype(BF16))
        state_old = state_ref[:, cols]
        y_off = _dot(c_g, state_old.astype(BF16)) * expacs_x[:, cols]
        state_ref[:, cols] = state_old * cdecay_x[:, cols] + _dot(b_g.T.astype(BF16), xds[:, cols])
        for pair in range(heads_per_group // 2):
            h0 = g * heads_per_group + 2 * pair
            pc = slice(g * gw + pair * LANES, g * gw + (pair + 1) * LANES)
            xp = xdt_f[:, pc]
            x_pair = jnp.concatenate([jnp.where(first, xp, 0.0), jnp.where(first, 0.0, xp)], axis=0).astype(BF16)
            mats = []
            for h in (h0, h0 + 1):
                seg = acs[:, h:h + 1] - acs_t[h:h + 1, :]
                dec = jnp.exp2(jnp.where(causal, seg, -jnp.inf))
                mats.append((cb * dec).astype(BF16))
            y_ref[:, pc] = _dot(jnp.concatenate(mats, axis=1), x_pair)
        y = y_ref[:, cols] + y_off + xs_c[:, cols] * dx_ref[:, cols]
        y = y * _silu(z_ref[:, cols].astype(F32))
        o_ref[:, cols] = (y * _rms_scale(y) * nw_ref[:, cols]).astype(o_ref.dtype)


def _ssd(proj3, dt3, cwx, cbx, cwbc, cbbc, dtb, alog, dx, nw, e, z_col, xs_col, bc_col):
    b, s, _ = proj3.shape
    w = e.shape[1]
    bcw = cwbc.shape[1]
    L = SSD_CHUNK
    full = lambda shape: pl.BlockSpec(shape, lambda bi, c: (0, 0))
    return pl.pallas_call(
        _ssd_kernel,
        out_shape=jax.ShapeDtypeStruct((b, s, w), BF16),
        grid=(b, s // L),
        in_specs=[
            pl.BlockSpec((None, L, w), lambda bi, c: (bi, c, z_col)),
            pl.BlockSpec((None, L, w), lambda bi, c: (bi, c, xs_col)),
            pl.BlockSpec((None, L, bcw), lambda bi, c: (bi, c, bc_col)),
            pl.BlockSpec((None, L, LANES), lambda bi, c: (bi, c, 0)),
            full((SSD_CONV, w)), full((1, w)), full((SSD_CONV, bcw)), full((1, bcw)),
            full((1, LANES)), full((1, LANES)), full((1, w)), full((1, w)), full((LANES, w)),
        ],
        out_specs=pl.BlockSpec((None, L, w), lambda bi, c: (bi, c, 0)),
        scratch_shapes=[
            pltpu.VMEM((SSD_STATE, w), F32),
            pltpu.VMEM((CONV_TAIL, w), BF16),
            pltpu.VMEM((CONV_TAIL, bcw), BF16),
            pltpu.VMEM((L, w), F32),
        ],
        compiler_params=pltpu.CompilerParams(
            dimension_semantics=("parallel", "arbitrary"), vmem_limit_bytes=VMEM_LIMIT),
        name="ssd",
    )(proj3, proj3, proj3, dt3, cwx, cbx, cwbc, cbbc, dtb, alog, dx, nw, e)


def _mix_out_kernel(a_ref, s_ref, x_ref, w_ref, xnw_ref, wq_ref, kv_ref, qw_ref, wo_ref, o_ref):
    d = X_HEAD_DIM
    ka = a_ref.shape[1]
    h1 = x_ref[...] + _dot(a_ref[...], w_ref[0])
    for c in range(1, w_ref.shape[0]):
        h1 = h1 + _dot(s_ref[:, (c - 1) * ka:c * ka], w_ref[c])
    hn = (h1 * _rms_scale(h1) * xnw_ref[...]).astype(BF16)
    q = _dot(hn, wq_ref[...])
    outs = []
    for hh in range(X_HEADS):
        qf = q[:, hh * d:(hh + 1) * d]
        qn = (qf * _rms_scale(qf) * qw_ref[...] * (d ** -0.5 * LOG2E)).astype(BF16)
        s = _dot_nt(qn, kv_ref[:, hh * d:(hh + 1) * d])
        p = jnp.exp2(s - jnp.max(s, axis=-1, keepdims=True))
        v = kv_ref[:, (X_HEADS + hh) * d:(X_HEADS + hh + 1) * d]
        o = _dot(p.astype(BF16), v) / jnp.sum(p, axis=-1, keepdims=True)
        outs.append(o.astype(BF16))
    o_ref[...] = h1 + _dot(jnp.concatenate(outs, axis=-1), wo_ref[...])


def _mix_out(a2, s2, x2, w_out, xnw, wq, kv3, qw, wo, seq, tm=512):
    m, ka = a2.shape
    ks = s2.shape[1]
    dm = x2.shape[1]
    xw = wq.shape[1]
    mlen = kv3.shape[1]
    assert seq % tm == 0 and ks % ka == 0
    once = dict(pipeline_mode=pl.Buffered(1))
    return pl.pallas_call(
        _mix_out_kernel,
        out_shape=jax.ShapeDtypeStruct((m, dm), F32),
        grid=(m // tm,),
        in_specs=[
            pl.BlockSpec((tm, ka), lambda i: (i, 0)),
            pl.BlockSpec((tm, ks), lambda i: (i, 0)),
            pl.BlockSpec((tm, dm), lambda i: (i, 0)),
            pl.BlockSpec((1 + ks // ka, ka, dm), lambda i: (0, 0, 0), **once),
            pl.BlockSpec((1, dm), lambda i: (0, 0)),
            pl.BlockSpec((dm, xw), lambda i: (0, 0), **once),
            pl.BlockSpec((None, mlen, 2 * xw), lambda i: (i // (seq // tm), 0, 0)),
            pl.BlockSpec((1, X_HEAD_DIM), lambda i: (0, 0)),
            pl.BlockSpec((xw, dm), lambda i: (0, 0), **once),
        ],
        out_specs=pl.BlockSpec((tm, dm), lambda i: (i, 0)),
        compiler_params=pltpu.CompilerParams(
            dimension_semantics=("parallel",), vmem_limit_bytes=BIG_VMEM_LIMIT),
        name="mix_out",
    )(a2, s2, x2, w_out.reshape(1 + ks // ka, ka, dm), xnw, wq, kv3, qw, wo)


def _ffn_kernel(h_ref, nw_ref, wg_ref, wu_ref, wd_ref, o_ref, hn_ref):
    f = pl.program_id(1)

    @pl.when(f == 0)
    def _():
        hf = h_ref[...]
        hn_ref[...] = (hf * _rms_scale(hf) * nw_ref[...]).astype(BF16)
        o_ref[...] = hf

    hn = hn_ref[...]
    act = (_silu(_dot(hn, wg_ref[...])) * _dot(hn, wu_ref[...])).astype(BF16)
    o_ref[...] += _dot(act, wd_ref[...])


def _ffn(h2, nw, wg, wu, wd, tm=1024, tf=512):
    m, k = h2.shape
    dff = wg.shape[1]
    return pl.pallas_call(
        _ffn_kernel,
        out_shape=jax.ShapeDtypeStruct((m, k), F32),
        grid=(m // tm, dff // tf),
        in_specs=[
            pl.BlockSpec((tm, k), lambda i, f: (i, 0), pipeline_mode=pl.Buffered(1)),
            pl.BlockSpec((1, k), lambda i, f: (0, 0)),
            pl.BlockSpec((k, tf), lambda i, f: (0, f)),
            pl.BlockSpec((k, tf), lambda i, f: (0, f)),
            pl.BlockSpec((tf, k), lambda i, f: (f, 0)),
        ],
        out_specs=pl.BlockSpec((tm, k), lambda i, f: (i, 0)),
        scratch_shapes=[pltpu.VMEM((tm, k), BF16)],
        compiler_params=pltpu.CompilerParams(
            dimension_semantics=("parallel", "arbitrary"), vmem_limit_bytes=BIG_VMEM_LIMIT),
        name="ffn",
    )(h2, nw, wg, wu, wd)


def _pad_lanes(v, fill=0.0):
    return jnp.pad(v, ((0, 0), (0, LANES - v.shape[1])), constant_values=fill)


def kernel(x, mem, mix_norm_w, w_in, da_q_norm_w, da_k_norm_w, da_lambda_q1, da_lambda_k1, da_lambda_q2, da_lambda_k2, da_subln_w, ssd_conv_w, ssd_conv_b, ssd_dt_bias, ssd_a_log, ssd_d, ssd_norm_w, w_out, xattn_norm_w, mem_norm_w, xattn_w_q, xattn_w_kv, xattn_q_norm_w, xattn_k_norm_w, xattn_w_o, ffn_norm_w, ffn_w_gate, ffn_w_up, ffn_w_down):
    b, s, dm = x.shape
    mlen = mem.shape[1]
    depth = w_in.shape[0]
    da_w = DA_HEADS * DA_V_DIM
    ssd_w = ssd_norm_w.shape[1]
    ssd_heads = ssd_w // SSD_HEAD_DIM
    gn = SSD_GROUPS * SSD_STATE
    xw = X_HEADS * X_HEAD_DIM
    o_q, o_k, o_v, o_z = 0, da_w, 2 * da_w, 3 * da_w
    o_xs = o_z + ssd_w
    o_bc = o_xs + ssd_w
    o_dt = o_bc + 2 * gn
    z_col, xs_col = 0, 1
    bc_col = (2 * ssd_w) // (2 * gn)
    q_col = (2 * ssd_w + 2 * gn) // DA_V_DIM
    k_col = q_col + DA_HEADS
    v_col = k_col + DA_HEADS
    assert (2 * ssd_w) % (2 * gn) == 0 and ssd_heads <= LANES

    assert LANES % ssd_heads == 0 and LANES // ssd_heads >= 3
    expand = np.zeros((LANES, ssd_w), np.float32)
    for r in range(LANES):
        h = r % ssd_heads
        expand[r, h * SSD_HEAD_DIM:(h + 1) * SSD_HEAD_DIM] = 1.0
    expand = jnp.asarray(expand, BF16)

    h = x.reshape(b * s, dm)
    mem2 = mem.reshape(b * mlen, dm)
    for i in range(depth):
        lambda_init = 0.8 - 0.6 * math.exp(-0.3 * i)
        proj, dt_raw = _in_proj(h, mix_norm_w[i][None], jnp.swapaxes(w_in, 1, 2), i, o_dt, o_z)
        proj3 = proj.reshape(b, s, -1)

        lam_p = jnp.stack([da_lambda_q1[i], da_lambda_k1[i], da_lambda_q2[i], da_lambda_k2[i]])
        a_out, (wo_b, wq_b, xwo_b, wg_b, wu_b, wd_b) = _diff_attn(
            proj3, jnp.tile(da_q_norm_w[i], 2)[None], jnp.tile(da_k_norm_w[i], 2)[None],
            lam_p, da_subln_w[i][None], lambda_init, q_col, k_col, v_col,
            [w_out, xattn_w_q, xattn_w_o, ffn_w_gate, ffn_w_up, ffn_w_down], i)

        cw, cb = ssd_conv_w[i], ssd_conv_b[i][None]
        s_out = _ssd(proj3, dt_raw.reshape(b, s, LANES),
                     cw[:, :ssd_w], cb[:, :ssd_w], cw[:, ssd_w:], cb[:, ssd_w:],
                     _pad_lanes(ssd_dt_bias[i][None]), _pad_lanes(ssd_a_log[i][None]),
                     jnp.repeat(ssd_d[i], SSD_HEAD_DIM)[None], ssd_norm_w[i][None], expand,
                     z_col, xs_col, bc_col)

        xkv = _xkv_proj(mem2, mem_norm_w[i][None], xattn_w_kv, i, xattn_k_norm_w[i][None])
        h2 = _mix_out(a_out.reshape(b * s, da_w), s_out.reshape(b * s, ssd_w), h, wo_b,
                      xattn_norm_w[i][None], wq_b, xkv.reshape(b, mlen, 2 * xw), xattn_q_norm_w[i][None], xwo_b, s)

        h = _ffn(h2, ffn_norm_w[i][None], wg_b, wu_b, wd_b)
    return h.reshape(b, s, dm)
```

```python
import functools
import math

import jax
import jax.numpy as jnp
import numpy as np
from jax import lax
from jax.experimental import pallas as pl
from jax.experimental.pallas import tpu as pltpu

EPS = 1e-6
LANES = 128
SUBLANES = 8
VMEM_LIMIT = 48 * 1024 * 1024
BIG_VMEM_LIMIT = 56 * 1024 * 1024
LOG2E = math.log2(math.e)
DEN_ROWS = 16
CONV_TAIL = 16

DA_HEADS = 8
DA_HEAD_DIM = 64
DA_V_DIM = 128
SSD_HEAD_DIM = 64
SSD_GROUPS = 4
SSD_STATE = 128
SSD_CONV = 4
SSD_CHUNK = 128
X_HEADS = 4
X_HEAD_DIM = 128

F32 = jnp.float32
BF16 = jnp.bfloat16


def _rms_scale(xf, eps=EPS):
    return lax.rsqrt(jnp.mean(xf * xf, axis=-1, keepdims=True) + eps)


def _silu(x):
    h = 0.5 * x
    return h + h * jnp.tanh(h)


def _split_bf16(x, parts):
    out = []
    r = x
    for _ in range(parts):
        p = r.astype(BF16)
        out.append(p)
        r = r - p.astype(F32)
    return out


def _dot(a, b):
    return jnp.dot(a, b, preferred_element_type=F32)


def _dot_nt(a, b):
    return lax.dot_general(a, b, (((1,), (1,)), ((), ())), preferred_element_type=F32)


def _lane_group_mean(xf, avg):
    hi, lo = _split_bf16(xf, 2)
    return _dot(hi, avg) + _dot(lo, avg)


def _in_proj_kernel(x_ref, nw_ref, wt_ref, wdt_ref, o_ref, dt_ref, hn_ref):
    n, mi = pl.program_id(1), pl.program_id(2)

    @pl.when(n == 0)
    def _():
        xf = x_ref[...]
        hn = (xf * _rms_scale(xf) * nw_ref[...]).astype(BF16)
        hn_ref[mi] = hn
        wdt = wdt_ref[...].astype(BF16)
        wdt = jnp.concatenate([wdt, jnp.zeros((LANES - wdt.shape[0], wdt.shape[1]), BF16)], axis=0)
        dt_ref[...] = _dot_nt(hn, wdt)

    o_ref[...] = _dot_nt(hn_ref[mi], wt_ref[...].astype(BF16)).astype(o_ref.dtype)


def _in_proj(x2, nw, wt_all, layer, n, col_shift, tm=1024, tn=1024, m_inner=2):
    m, k = x2.shape
    n_dt = wt_all.shape[1] - n
    shift, nblk = col_shift // tn, n // tn
    assert col_shift % tn == 0 and n % tn == 0 and n % n_dt == 0 and m % (tm * m_inner) == 0
    row = lambda mo, j, mi: mo * m_inner + jnp.where(j == 0, mi, m_inner - 1)
    return pl.pallas_call(
        _in_proj_kernel,
        out_shape=(jax.ShapeDtypeStruct((m, n), BF16), jax.ShapeDtypeStruct((m, LANES), F32)),
        grid=(m // (tm * m_inner), nblk, m_inner),
        in_specs=[
            pl.BlockSpec((tm, k), lambda mo, j, mi: (row(mo, j, mi), 0)),
            pl.BlockSpec((1, k), lambda mo, j, mi: (0, 0)),
            pl.BlockSpec((None, tn, k), lambda mo, j, mi: (layer, (j + shift) % nblk, 0)),
            pl.BlockSpec((None, n_dt, k), lambda mo, j, mi: (layer, n // n_dt, 0)),
        ],
        out_specs=(
            pl.BlockSpec((tm, tn), lambda mo, j, mi: (mo * m_inner + mi, j)),
            pl.BlockSpec((tm, LANES), lambda mo, j, mi: (row(mo, j, mi), 0)),
        ),
        scratch_shapes=[pltpu.VMEM((m_inner, tm, k), BF16)],
        compiler_params=pltpu.CompilerParams(
            dimension_semantics=("parallel", "arbitrary", "arbitrary"), vmem_limit_bytes=BIG_VMEM_LIMIT),
        name="in_proj",
    )(x2, nw, wt_all, wt_all)


def _xkv_kernel(x_ref, nw_ref, w_ref, kw_ref, o_ref):
    d = X_HEAD_DIM
    xf = x_ref[...]
    hn = (xf * _rms_scale(xf) * nw_ref[...]).astype(BF16)
    kv = _dot(hn, w_ref[...].astype(BF16))
    kw = X_HEADS * d
    for hh in range(X_HEADS):
        kf = kv[:, hh * d:(hh + 1) * d]
        o_ref[:, hh * d:(hh + 1) * d] = (kf * _rms_scale(kf) * kw_ref[...]).astype(o_ref.dtype)
    o_ref[:, kw:] = kv[:, kw:].astype(o_ref.dtype)


def _xkv_proj(x2, nw, w_all, layer, kw, tm=256):
    m, k = x2.shape
    n = w_all.shape[2]
    return pl.pallas_call(
        _xkv_kernel,
        out_shape=jax.ShapeDtypeStruct((m, n), BF16),
        grid=(m // tm,),
        in_specs=[
            pl.BlockSpec((tm, k), lambda i: (i, 0)),
            pl.BlockSpec((1, k), lambda i: (0, 0)),
            pl.BlockSpec((None, k, n), lambda i: (layer, 0, 0)),
            pl.BlockSpec((1, X_HEAD_DIM), lambda i: (0, 0)),
        ],
        out_specs=pl.BlockSpec((tm, n), lambda i: (i, 0)),
        compiler_params=pltpu.CompilerParams(
            dimension_semantics=("parallel",), vmem_limit_bytes=VMEM_LIMIT),
        name="xkv_proj",
    )(x2, nw, w_all, kw)


def _diff_attn_kernel(q_ref, k_ref, v_ref, qw_ref, kw_ref, lam_ref, sw_ref, *rest,
                      tq, hb, lambda_init, n_cast):
    cast_in, o_ref, cast_out = rest[:n_cast], rest[n_cast], rest[n_cast + 1:2 * n_cast + 1]
    kn_ref, vt_ref, qcat_ref, m_ref, acc_ref, s_ref = rest[2 * n_cast + 1:]
    for w_ref, wb_ref in zip(cast_in, cast_out):
        wb_ref[...] = w_ref[...].astype(BF16)
    i = pl.program_id(2)
    n_tiles = vt_ref.shape[1]
    d = DA_V_DIM
    lane = lax.broadcasted_iota(jnp.int32, (1, d), 1)
    first = lane < DA_HEAD_DIM
    ones_row = lax.broadcasted_iota(jnp.int32, (DEN_ROWS, tq), 0) == 0

    half_r = lax.broadcasted_iota(jnp.int32, (d, d), 0) < DA_HEAD_DIM
    half_c = lax.broadcasted_iota(jnp.int32, (d, d), 1) < DA_HEAD_DIM
    sub_avg = jnp.where(half_r == half_c, 1.0 / DA_HEAD_DIM, 0.0).astype(BF16)

    def sub_head_norm(xf):
        return xf * lax.rsqrt(_lane_group_mean(xf * xf, sub_avg) + EPS)

    @pl.when(i == 0)
    def _():
        for hh in range(hb):
            hs = slice(hh * d, (hh + 1) * d)
            kn_ref[hh] = (sub_head_norm(k_ref[:, hs].astype(F32)) * kw_ref[...]).astype(BF16)
            for t in range(n_tiles):
                vt_ref[hh, t, :d] = v_ref[t * tq:(t + 1) * tq, hs].astype(F32).T.astype(BF16)
                vt_ref[hh, t, d:] = jnp.where(ones_row, 1.0, 0.0).astype(BF16)

    for hh in range(hb):
        qn = sub_head_norm(q_ref[:, hh * d:(hh + 1) * d].astype(F32)) * qw_ref[...] * (DA_HEAD_DIM ** -0.5 * LOG2E)
        qcat_ref[hh] = jnp.concatenate(
            [jnp.where(first, qn, 0.0).T, jnp.where(first, 0.0, qn).T], axis=1).astype(BF16)
    m_ref[...] = jnp.full(m_ref.shape, -jnp.inf, F32)
    acc_ref[...] = jnp.zeros(acc_ref.shape, F32)

    def scores(j, hh):
        start = pl.multiple_of(j * tq, tq)
        return _dot(kn_ref[hh, pl.ds(start, tq), :], qcat_ref[hh])

    def tile(j, cur, last):
        if last:
            key = lax.broadcasted_iota(jnp.int32, (tq, 2 * tq), 0)
            qry = lax.broadcasted_iota(jnp.int32, (tq, 2 * tq), 1)
            causal = key <= jnp.where(qry >= tq, qry - tq, qry)
        for hh in range(hb):
            if not last:
                s_ref[1 - cur, hh] = scores(j + 1, hh)
            s = s_ref[cur, hh]
            if last:
                s = jnp.where(causal, s, -jnp.inf)
            m_prev = m_ref[hh]
            m_new = jnp.maximum(m_prev, jnp.max(s, axis=0, keepdims=True))
            alpha = jnp.exp2(m_prev - m_new)
            p = jnp.exp2(s - m_new)
            acc_ref[hh] = alpha * acc_ref[hh] + _dot(vt_ref[hh, j], p.astype(BF16))
            m_ref[hh] = m_new

    for hh in range(hb):
        s_ref[0, hh] = scores(0, hh)

    def body(jj, carry):
        tile(2 * jj, 0, False)
        tile(2 * jj + 1, 1, False)
        return carry

    lax.fori_loop(0, i // 2, body, 0)

    @pl.when(i % 2 == 0)
    def _():
        tile(i, 0, True)

    @pl.when(i % 2 == 1)
    def _():
        tile(i - 1, 0, False)
        tile(i, 1, True)

    lam_p = lam_ref[...]
    lam = (jnp.exp(jnp.sum(lam_p[0:1] * lam_p[1:2], axis=-1, keepdims=True))
           - jnp.exp(jnp.sum(lam_p[2:3] * lam_p[3:4], axis=-1, keepdims=True)) + lambda_init)
    for hh in range(hb):
        ot = acc_ref[hh, :d] / acc_ref[hh, d:d + 1]
        o = (ot[:, :tq] - lam * ot[:, tq:]).T
        o = o * _rms_scale(o) * sw_ref[...] * (1.0 - lambda_init)
        o_ref[:, hh * d:(hh + 1) * d] = o.astype(o_ref.dtype)


def _diff_attn(proj3, qw2, kw2, lam_p, sw, lambda_init, q_col, k_col, v_col, cast_ws, layer, tq=256, hb=8):
    b, s, _ = proj3.shape
    d = DA_V_DIM
    nh, nq = DA_HEADS // hb, s // tq
    steps = b * nh * nq
    kern = functools.partial(_diff_attn_kernel, tq=tq, hb=hb, lambda_init=lambda_init, n_cast=len(cast_ws))
    vec = lambda shape: pl.BlockSpec(shape, lambda bi, h, i: (0, 0))
    cast_in, cast_out, cast_shapes = [], [], []
    for w in cast_ws:
        rows, cols = w.shape[1:]
        rb = rows // steps
        assert rows % steps == 0 and rb % (2 * SUBLANES) == 0
        cast_in.append(pl.BlockSpec((None, rb, cols), lambda bi, h, i: (layer, (bi * nh + h) * nq + i, 0)))
        cast_out.append(pl.BlockSpec((rb, cols), lambda bi, h, i: ((bi * nh + h) * nq + i, 0)))
        cast_shapes.append(jax.ShapeDtypeStruct((rows, cols), BF16))
    outs = pl.pallas_call(
        kern,
        out_shape=[jax.ShapeDtypeStruct((b, s, DA_HEADS * d), BF16)] + cast_shapes,
        grid=(b, nh, nq),
        in_specs=[
            pl.BlockSpec((None, tq, hb * d), lambda bi, h, i: (bi, i, q_col // hb + h)),
            pl.BlockSpec((None, s, hb * d), lambda bi, h, i: (bi, 0, k_col // hb + h)),
            pl.BlockSpec((None, s, hb * d), lambda bi, h, i: (bi, 0, v_col // hb + h)),
            vec((1, d)), vec((1, d)), vec((4, DA_HEAD_DIM)), vec((1, d)),
        ] + cast_in,
        out_specs=[pl.BlockSpec((None, tq, hb * d), lambda bi, h, i: (bi, i, h))] + cast_out,
        scratch_shapes=[
            pltpu.VMEM((hb, s, d), BF16),
            pltpu.VMEM((hb, s // tq, d + DEN_ROWS, tq), BF16),
            pltpu.VMEM((hb, d, 2 * tq), BF16),
            pltpu.VMEM((hb, 1, 2 * tq), F32),
            pltpu.VMEM((hb, d + DEN_ROWS, 2 * tq), F32),
            pltpu.VMEM((2, hb, tq, 2 * tq), F32),
        ],
        compiler_params=pltpu.CompilerParams(
            dimension_semantics=("parallel", "parallel", "arbitrary"), vmem_limit_bytes=BIG_VMEM_LIMIT),
        name="diff_attn",
    )(proj3, proj3, proj3, qw2, kw2, lam_p, sw, *cast_ws)
    return outs[0], outs[1:]


def _ssd_kernel(z_ref, xs_ref, bc_ref, xsn_ref, bcn_ref, dt_ref, cwx_ref, cbx_ref, cwbc_ref, cbbc_ref,
                dtb_ref, alog_ref, dx_ref, nw_ref, e_ref, *rest, n_cast):
    cast_in, o_ref, cast_out = rest[:n_cast], rest[n_cast], rest[n_cast + 1:2 * n_cast + 1]
    state_ref, cxa_ref, cba_ref, cxb_ref, cbb_ref, y_ref = rest[2 * n_cast + 1:]
    for w_ref, wb_ref in zip(cast_in, cast_out):
        wb_ref[...] = w_ref[...].astype(BF16)
    L, N, G = SSD_CHUNK, SSD_STATE, SSD_GROUPS
    w = o_ref.shape[-1]
    bcw = bc_ref.shape[-1]
    gw = w // G
    c = pl.program_id(1)

    ext = CONV_TAIL + L
    s_row = lax.broadcasted_iota(jnp.int32, (L, SSD_CONV * ext), 0)
    s_col = lax.broadcasted_iota(jnp.int32, (L, SSD_CONV * ext), 1)
    pick = jnp.zeros((L, SSD_CONV * ext), F32)
    for k in range(SSD_CONV):
        pick = jnp.where(s_col == s_row + (k * ext + CONV_TAIL - k), 0.5, pick)
    pick = pick.astype(BF16)

    def conv_silu(rows, w_ref, b_ref, cs, dst_ref):
        taps = jnp.concatenate(
            [rows * w_ref[SSD_CONV - 1 - k:SSD_CONV - k, cs].astype(BF16) for k in range(SSD_CONV)], axis=0)
        h = _dot(pick, taps) + 0.5 * b_ref[:, cs]
        dst_ref[:, cs] = h + h * jnp.tanh(h)

    def conv_slice(g, rows_of, dst_x_ref, dst_bc_ref):
        xc = slice(g * (w // G), (g + 1) * (w // G))
        bc = slice(g * (bcw // G), (g + 1) * (bcw // G))
        conv_silu(rows_of(xs_ref, xsn_ref, xc), cwx_ref, cbx_ref, xc, dst_x_ref)
        conv_silu(rows_of(bc_ref, bcn_ref, bc), cwbc_ref, cbbc_ref, bc, dst_bc_ref)

    def rows_first(cur_ref, nxt_ref, cs):
        return jnp.concatenate([jnp.zeros((CONV_TAIL, cs.stop - cs.start), BF16), cur_ref[0:L, cs]], axis=0)

    def rows_second(cur_ref, nxt_ref, cs):
        return cur_ref[L - CONV_TAIL:2 * L, cs]

    def rows_next(cur_ref, nxt_ref, cs):
        return jnp.concatenate([cur_ref[2 * L - CONV_TAIL:2 * L, cs], nxt_ref[:, cs]], axis=0)

    @pl.when(c == 0)
    def _():
        state_ref[...] = jnp.zeros(state_ref.shape, F32)
        for g in range(G):
            conv_slice(g, rows_first, cxa_ref, cba_ref)

    r_i = lax.broadcasted_iota(jnp.int32, (L, L), 0)
    c_i = lax.broadcasted_iota(jnp.int32, (L, L), 1)
    causal = c_i <= r_i
    tril = jnp.where(causal, 1.0, 0.0).astype(BF16)
    lane = lax.broadcasted_iota(jnp.int32, (1, LANES), 1)
    first = lane < SSD_HEAD_DIM
    n_heads = w // SSD_HEAD_DIM
    heads_per_group = gw // SSD_HEAD_DIM

    def expand(v, parts):
        packed, r = None, jnp.where(lane < n_heads, v, 0.0)
        for t in range(parts):
            p = r.astype(BF16).astype(F32)
            packed = p if t == 0 else packed + pltpu.roll(p, t * n_heads, 1)
            r = r - p
        return _dot(packed.astype(BF16), e_ref[...])

    def mix_chunk(r0, cx_ref, cb_ref, side):
        rows = slice(r0, r0 + L)
        dtx = dt_ref[rows, :] + dtb_ref[...]
        dt = jnp.maximum(dtx, 0.0) + jnp.log1p(jnp.exp(-jnp.abs(dtx)))
        a = -jnp.exp(alog_ref[...]) * LOG2E
        acs = sum(_dot(tril, p) for p in _split_bf16(dt * a, 3))
        acs_t = acs.T
        dt_x = expand(dt, 2)
        acs_x = expand(acs, 3)
        last_x = acs_x[L - 1:L, :]
        for g in range(G):
            side(g)
            cols = slice(g * gw, (g + 1) * gw)
            xs_g = cx_ref[:, cols]
            xdt_g = xs_g * dt_x[:, cols]
            xds_g = (xdt_g * jnp.exp2(last_x[:, cols] - acs_x[:, cols])).astype(BF16)
            b_g = cb_ref[:, g * N:(g + 1) * N]
            c_g = cb_ref[:, (G + g) * N:(G + g + 1) * N].astype(BF16)
            cb = _dot_nt(c_g, b_g.astype(BF16))
            state_old = state_ref[:, cols]
            y_off = _dot(c_g, state_old.astype(BF16)) * jnp.exp2(acs_x[:, cols])
            state_ref[:, cols] = state_old * jnp.exp2(last_x[:, cols]) + _dot(b_g.T.astype(BF16), xds_g)
            for pair in range(heads_per_group // 2):
                h0 = g * heads_per_group + 2 * pair
                pc = slice(pair * LANES, (pair + 1) * LANES)
                xp = xdt_g[:, pc]
                x_pair = jnp.concatenate([jnp.where(first, xp, 0.0), jnp.where(first, 0.0, xp)], axis=0).astype(BF16)
                mats = []
                for h in (h0, h0 + 1):
                    seg = acs[:, h:h + 1] - acs_t[h:h + 1, :]
                    dec = jnp.exp2(jnp.where(causal, seg, -jnp.inf))
                    mats.append((cb * dec).astype(BF16))
                y_ref[:, pc] = _dot(jnp.concatenate(mats, axis=1), x_pair)
            y = y_ref[...] + y_off + xs_g * dx_ref[:, cols]
            y = y * _silu(z_ref[rows, cols].astype(F32))
            o_ref[rows, cols] = (y * _rms_scale(y) * nw_ref[:, cols]).astype(o_ref.dtype)

    mix_chunk(0, cxa_ref, cba_ref, lambda g: conv_slice(g, rows_second, cxb_ref, cbb_ref))
    mix_chunk(L, cxb_ref, cbb_ref, lambda g: conv_slice(g, rows_next, cxa_ref, cba_ref))


def _ssd(proj3, dt3, cwx, cbx, cwbc, cbbc, dtb, alog, dx, nw, e, z_col, xs_col, bc_col, cast_ws, layer):
    b, s, _ = proj3.shape
    w = e.shape[1]
    bcw = cwbc.shape[1]
    L = SSD_CHUNK
    nc = s // L
    assert nc % 2 == 0
    steps = b * (nc // 2)
    cast_in, cast_out, cast_shapes = [], [], []
    for cw in cast_ws:
        rows, cols = cw.shape[1:]
        rb = rows // steps
        assert rows % steps == 0 and rb % (2 * SUBLANES) == 0
        cast_in.append(pl.BlockSpec((None, rb, cols), lambda bi, c: (layer, bi * (nc // 2) + c, 0)))
        cast_out.append(pl.BlockSpec((rb, cols), lambda bi, c: (bi * (nc // 2) + c, 0)))
        cast_shapes.append(jax.ShapeDtypeStruct((rows, cols), BF16))
    full = lambda shape: pl.BlockSpec(shape, lambda bi, c: (0, 0))
    nxt = lambda c: jnp.minimum(2 * c + 2, nc - 1)
    outs = pl.pallas_call(
        functools.partial(_ssd_kernel, n_cast=len(cast_ws)),
        out_shape=[jax.ShapeDtypeStruct((b, s, w), BF16)] + cast_shapes,
        grid=(b, nc // 2),
        in_specs=[
            pl.BlockSpec((None, 2 * L, w), lambda bi, c: (bi, c, z_col)),
            pl.BlockSpec((None, 2 * L, w), lambda bi, c: (bi, c, xs_col)),
            pl.BlockSpec((None, 2 * L, bcw), lambda bi, c: (bi, c, bc_col)),
            pl.BlockSpec((None, L, w), lambda bi, c: (bi, nxt(c), xs_col)),
            pl.BlockSpec((None, L, bcw), lambda bi, c: (bi, nxt(c), bc_col)),
            pl.BlockSpec((None, 2 * L, LANES), lambda bi, c: (bi, c, 0)),
            full((SSD_CONV, w)), full((1, w)), full((SSD_CONV, bcw)), full((1, bcw)),
            full((1, LANES)), full((1, LANES)), full((1, w)), full((1, w)), full((LANES, w)),
        ] + cast_in,
        out_specs=[pl.BlockSpec((None, 2 * L, w), lambda bi, c: (bi, c, 0))] + cast_out,
        scratch_shapes=[
            pltpu.VMEM((SSD_STATE, w), F32),
            pltpu.VMEM((L, w), F32), pltpu.VMEM((L, bcw), F32),
            pltpu.VMEM((L, w), F32), pltpu.VMEM((L, bcw), F32),
            pltpu.VMEM((L, w // SSD_GROUPS), F32),
        ],
        compiler_params=pltpu.CompilerParams(
            dimension_semantics=("parallel", "arbitrary"), vmem_limit_bytes=VMEM_LIMIT),
        name="ssd",
    )(proj3, proj3, proj3, proj3, proj3, dt3, cwx, cbx, cwbc, cbbc, dtb, alog, dx, nw, e, *cast_ws)
    return outs[0], outs[1:]


def _mix_out_kernel(a_ref, s_ref, x_ref, w_ref, xnw_ref, wq_ref, kv_ref, qw_ref, wo_ref, o_ref):
    d = X_HEAD_DIM
    ka = a_ref.shape[1]
    h1 = x_ref[...] + _dot(a_ref[...], w_ref[0])
    for c in range(1, w_ref.shape[0]):
        h1 = h1 + _dot(s_ref[:, (c - 1) * ka:c * ka], w_ref[c])
    hn = (h1 * _rms_scale(h1) * xnw_ref[...]).astype(BF16)
    q = _dot(hn, wq_ref[...])
    outs = []
    for hh in range(X_HEADS):
        qf = q[:, hh * d:(hh + 1) * d]
        qn = (qf * _rms_scale(qf) * qw_ref[...] * (d ** -0.5 * LOG2E)).astype(BF16)
        s = _dot_nt(qn, kv_ref[:, hh * d:(hh + 1) * d])
        p = jnp.exp2(s - jnp.max(s, axis=-1, keepdims=True))
        v = kv_ref[:, (X_HEADS + hh) * d:(X_HEADS + hh + 1) * d]
        o = _dot(p.astype(BF16), v) / jnp.sum(p, axis=-1, keepdims=True)
        outs.append(o.astype(BF16))
    o_ref[...] = h1 + _dot(jnp.concatenate(outs, axis=-1), wo_ref[...])


def _mix_out(a2, s2, x2, w_out, xnw, wq, kv3, qw, wo, seq, tm=512):
    m, ka = a2.shape
    ks = s2.shape[1]
    dm = x2.shape[1]
    xw = wq.shape[1]
    mlen = kv3.shape[1]
    assert seq % tm == 0 and ks % ka == 0
    once = dict(pipeline_mode=pl.Buffered(1))
    return pl.pallas_call(
        _mix_out_kernel,
        out_shape=jax.ShapeDtypeStruct((m, dm), F32),
        grid=(m // tm,),
        in_specs=[
            pl.BlockSpec((tm, ka), lambda i: (i, 0)),
            pl.BlockSpec((tm, ks), lambda i: (i, 0)),
            pl.BlockSpec((tm, dm), lambda i: (i, 0)),
            pl.BlockSpec((1 + ks // ka, ka, dm), lambda i: (0, 0, 0), **once),
            pl.BlockSpec((1, dm), lambda i: (0, 0)),
            pl.BlockSpec((dm, xw), lambda i: (0, 0), **once),
            pl.BlockSpec((None, mlen, 2 * xw), lambda i: (i // (seq // tm), 0, 0)),
            pl.BlockSpec((1, X_HEAD_DIM), lambda i: (0, 0)),
            pl.BlockSpec((xw, dm), lambda i: (0, 0), **once),
        ],
        out_specs=pl.BlockSpec((tm, dm), lambda i: (i, 0)),
        compiler_params=pltpu.CompilerParams(
            dimension_semantics=("parallel",), vmem_limit_bytes=BIG_VMEM_LIMIT),
        name="mix_out",
    )(a2, s2, x2, w_out.reshape(1 + ks // ka, ka, dm), xnw, wq, kv3, qw, wo)


def _ffn_kernel(h_ref, nw_ref, wg_ref, wu_ref, wd_ref, o_ref, hn_ref):
    f = pl.program_id(1)

    @pl.when(f == 0)
    def _():
        hf = h_ref[...]
        hn_ref[...] = (hf * _rms_scale(hf) * nw_ref[...]).astype(BF16)
        o_ref[...] = hf

    hn = hn_ref[...]
    act = (_silu(_dot(hn, wg_ref[...])) * _dot(hn, wu_ref[...])).astype(BF16)
    o_ref[...] += _dot(act, wd_ref[...])


def _ffn(h2, nw, wg, wu, wd, tm=1024, tf=512):
    m, k = h2.shape
    dff = wg.shape[1]
    return pl.pallas_call(
        _ffn_kernel,
        out_shape=jax.ShapeDtypeStruct((m, k), F32),
        grid=(m // tm, dff // tf),
        in_specs=[
            pl.BlockSpec((tm, k), lambda i, f: (i, 0), pipeline_mode=pl.Buffered(1)),
            pl.BlockSpec((1, k), lambda i, f: (0, 0)),
            pl.BlockSpec((k, tf), lambda i, f: (0, f)),
            pl.BlockSpec((k, tf), lambda i, f: (0, f)),
            pl.BlockSpec((tf, k), lambda i, f: (f, 0)),
        ],
        out_specs=pl.BlockSpec((tm, k), lambda i, f: (i, 0)),
        scratch_shapes=[pltpu.VMEM((tm, k), BF16)],
        compiler_params=pltpu.CompilerParams(
            dimension_semantics=("parallel", "arbitrary"), vmem_limit_bytes=BIG_VMEM_LIMIT),
        name="ffn",
    )(h2, nw, wg, wu, wd)


def _pad_lanes(v, fill=0.0):
    return jnp.pad(v, ((0, 0), (0, LANES - v.shape[1])), constant_values=fill)


def kernel(x, mem, mix_norm_w, w_in, da_q_norm_w, da_k_norm_w, da_lambda_q1, da_lambda_k1, da_lambda_q2, da_lambda_k2, da_subln_w, ssd_conv_w, ssd_conv_b, ssd_dt_bias, ssd_a_log, ssd_d, ssd_norm_w, w_out, xattn_norm_w, mem_norm_w, xattn_w_q, xattn_w_kv, xattn_q_norm_w, xattn_k_norm_w, xattn_w_o, ffn_norm_w, ffn_w_gate, ffn_w_up, ffn_w_down):
    b, s, dm = x.shape
    mlen = mem.shape[1]
    depth = w_in.shape[0]
    da_w = DA_HEADS * DA_V_DIM
    ssd_w = ssd_norm_w.shape[1]
    ssd_heads = ssd_w // SSD_HEAD_DIM
    gn = SSD_GROUPS * SSD_STATE
    xw = X_HEADS * X_HEAD_DIM
    o_q, o_k, o_v, o_z = 0, da_w, 2 * da_w, 3 * da_w
    o_xs = o_z + ssd_w
    o_bc = o_xs + ssd_w
    o_dt = o_bc + 2 * gn
    z_col, xs_col = 0, 1
    bc_col = (2 * ssd_w) // (2 * gn)
    q_col = (2 * ssd_w + 2 * gn) // DA_V_DIM
    k_col = q_col + DA_HEADS
    v_col = k_col + DA_HEADS
    assert (2 * ssd_w) % (2 * gn) == 0 and ssd_heads <= LANES

    assert LANES % ssd_heads == 0 and LANES // ssd_heads >= 3
    expand = np.zeros((LANES, ssd_w), np.float32)
    for r in range(LANES):
        h = r % ssd_heads
        expand[r, h * SSD_HEAD_DIM:(h + 1) * SSD_HEAD_DIM] = 1.0
    expand = jnp.asarray(expand, BF16)

    h = x.reshape(b * s, dm)
    mem2 = mem.reshape(b * mlen, dm)
    for i in range(depth):
        lambda_init = 0.8 - 0.6 * math.exp(-0.3 * i)
        proj, dt_raw = _in_proj(h, mix_norm_w[i][None], jnp.swapaxes(w_in, 1, 2), i, o_dt, o_z)
        proj3 = proj.reshape(b, s, -1)

        lam_p = jnp.stack([da_lambda_q1[i], da_lambda_k1[i], da_lambda_q2[i], da_lambda_k2[i]])
        a_out, _ = _diff_attn(
            proj3, jnp.tile(da_q_norm_w[i], 2)[None], jnp.tile(da_k_norm_w[i], 2)[None],
            lam_p, da_subln_w[i][None], lambda_init, q_col, k_col, v_col, [], i)

        cw, cb = ssd_conv_w[i], ssd_conv_b[i][None]
        s_out, (wo_b, wq_b, xwo_b, wg_b, wu_b, wd_b) = _ssd(
            proj3, dt_raw.reshape(b, s, LANES),
            cw[:, :ssd_w], cb[:, :ssd_w], cw[:, ssd_w:], cb[:, ssd_w:],
            _pad_lanes(ssd_dt_bias[i][None]), _pad_lanes(ssd_a_log[i][None]),
            jnp.repeat(ssd_d[i], SSD_HEAD_DIM)[None], ssd_norm_w[i][None], expand,
            z_col, xs_col, bc_col,
            [w_out, xattn_w_q, xattn_w_o, ffn_w_gate, ffn_w_up, ffn_w_down], i)

        xkv = _xkv_proj(mem2, mem_norm_w[i][None], xattn_w_kv, i, xattn_k_norm_w[i][None])
        h2 = _mix_out(a_out.reshape(b * s, da_w), s_out.reshape(b * s, ssd_w), h, wo_b,
                      xattn_norm_w[i][None], wq_b, xkv.reshape(b, mlen, 2 * xw), xattn_q_norm_w[i][None], xwo_b, s)

        h = _ffn(h2, ffn_norm_w[i][None], wg_b, wu_b, wd_b)
    return h.reshape(b, s, dm)
```

```python
import functools
import math

import jax
import jax.numpy as jnp
import numpy as np
from jax import lax
from jax.experimental import pallas as pl
from jax.experimental.pallas import tpu as pltpu

EPS = 1e-6
LANES = 128
SUBLANES = 8
VMEM_LIMIT = 48 * 1024 * 1024
BIG_VMEM_LIMIT = 56 * 1024 * 1024
LOG2E = math.log2(math.e)
DEN_ROWS = 16
CONV_TAIL = 16

DA_HEADS = 8
DA_HEAD_DIM = 64
DA_V_DIM = 128
SSD_HEAD_DIM = 64
SSD_GROUPS = 4
SSD_STATE = 128
SSD_CONV = 4
SSD_CHUNK = 128
X_HEADS = 4
X_HEAD_DIM = 128

F32 = jnp.float32
BF16 = jnp.bfloat16


def _rms_scale(xf, eps=EPS):
    return lax.rsqrt(jnp.mean(xf * xf, axis=-1, keepdims=True) + eps)


def _silu(x):
    h = 0.5 * x
    return h + h * jnp.tanh(h)


def _split_bf16(x, parts):
    out = []
    r = x
    for _ in range(parts):
        p = r.astype(BF16)
        out.append(p)
        r = r - p.astype(F32)
    return out


def _dot(a, b):
    return jnp.dot(a, b, preferred_element_type=F32)


def _dot_nt(a, b):
    return lax.dot_general(a, b, (((1,), (1,)), ((), ())), preferred_element_type=F32)


def _lane_group_mean(xf, avg):
    hi, lo = _split_bf16(xf, 2)
    return _dot(hi, avg) + _dot(lo, avg)


def _in_proj_kernel(x_ref, nw_ref, wt_ref, wdt_ref, o_ref, dt_ref, hn_ref):
    n, mi = pl.program_id(1), pl.program_id(2)

    @pl.when(n == 0)
    def _():
        xf = x_ref[...]
        hn = (xf * _rms_scale(xf) * nw_ref[...]).astype(BF16)
        hn_ref[mi] = hn
        wdt = wdt_ref[...].astype(BF16)
        wdt = jnp.concatenate([wdt, jnp.zeros((LANES - wdt.shape[0], wdt.shape[1]), BF16)], axis=0)
        dt_ref[...] = _dot_nt(hn, wdt)

    o_ref[...] = _dot_nt(hn_ref[mi], wt_ref[...].astype(BF16)).astype(o_ref.dtype)


def _in_proj(x2, nw, wt_all, layer, n, col_shift, tm=1024, tn=1024, m_inner=2):
    m, k = x2.shape
    n_dt = wt_all.shape[1] - n
    shift, nblk = col_shift // tn, n // tn
    assert col_shift % tn == 0 and n % tn == 0 and n % n_dt == 0 and m % (tm * m_inner) == 0
    row = lambda mo, j, mi: mo * m_inner + jnp.where(j == 0, mi, m_inner - 1)
    return pl.pallas_call(
        _in_proj_kernel,
        out_shape=(jax.ShapeDtypeStruct((m, n), BF16), jax.ShapeDtypeStruct((m, LANES), F32)),
        grid=(m // (tm * m_inner), nblk, m_inner),
        in_specs=[
            pl.BlockSpec((tm, k), lambda mo, j, mi: (row(mo, j, mi), 0)),
            pl.BlockSpec((1, k), lambda mo, j, mi: (0, 0)),
            pl.BlockSpec((None, tn, k), lambda mo, j, mi: (layer, (j + shift) % nblk, 0)),
            pl.BlockSpec((None, n_dt, k), lambda mo, j, mi: (layer, n // n_dt, 0)),
        ],
        out_specs=(
            pl.BlockSpec((tm, tn), lambda mo, j, mi: (mo * m_inner + mi, j)),
            pl.BlockSpec((tm, LANES), lambda mo, j, mi: (row(mo, j, mi), 0)),
        ),
        scratch_shapes=[pltpu.VMEM((m_inner, tm, k), BF16)],
        compiler_params=pltpu.CompilerParams(
            dimension_semantics=("parallel", "arbitrary", "arbitrary"), vmem_limit_bytes=BIG_VMEM_LIMIT),
        name="in_proj",
    )(x2, nw, wt_all, wt_all)


def _xkv_kernel(x_ref, nw_ref, w_ref, kw_ref, o_ref):
    d = X_HEAD_DIM
    xf = x_ref[...]
    hn = (xf * _rms_scale(xf) * nw_ref[...]).astype(BF16)
    kv = _dot(hn, w_ref[...].astype(BF16))
    kw = X_HEADS * d
    for hh in range(X_HEADS):
        kf = kv[:, hh * d:(hh + 1) * d]
        o_ref[:, hh * d:(hh + 1) * d] = (kf * _rms_scale(kf) * kw_ref[...]).astype(o_ref.dtype)
    o_ref[:, kw:] = kv[:, kw:].astype(o_ref.dtype)


def _xkv_proj(x2, nw, w_all, layer, kw, tm=256):
    m, k = x2.shape
    n = w_all.shape[2]
    return pl.pallas_call(
        _xkv_kernel,
        out_shape=jax.ShapeDtypeStruct((m, n), BF16),
        grid=(m // tm,),
        in_specs=[
            pl.BlockSpec((tm, k), lambda i: (i, 0)),
            pl.BlockSpec((1, k), lambda i: (0, 0)),
            pl.BlockSpec((None, k, n), lambda i: (layer, 0, 0)),
            pl.BlockSpec((1, X_HEAD_DIM), lambda i: (0, 0)),
        ],
        out_specs=pl.BlockSpec((tm, n), lambda i: (i, 0)),
        compiler_params=pltpu.CompilerParams(
            dimension_semantics=("parallel",), vmem_limit_bytes=VMEM_LIMIT),
        name="xkv_proj",
    )(x2, nw, w_all, kw)


def _diff_attn_kernel(q_ref, k_ref, v_ref, qw_ref, kw_ref, lam_ref, sw_ref, *rest,
                      tq, hb, lambda_init, n_cast):
    cast_in, o_ref, cast_out = rest[:n_cast], rest[n_cast], rest[n_cast + 1:2 * n_cast + 1]
    kn_ref, vt_ref, qcat_ref, m_ref, acc_ref, s_ref = rest[2 * n_cast + 1:]
    for w_ref, wb_ref in zip(cast_in, cast_out):
        wb_ref[...] = w_ref[...].astype(BF16)
    i = pl.program_id(2)
    n_tiles = vt_ref.shape[1]
    d = DA_V_DIM
    lane = lax.broadcasted_iota(jnp.int32, (1, d), 1)
    first = lane < DA_HEAD_DIM
    ones_row = lax.broadcasted_iota(jnp.int32, (DEN_ROWS, tq), 0) == 0

    half_r = lax.broadcasted_iota(jnp.int32, (d, d), 0) < DA_HEAD_DIM
    half_c = lax.broadcasted_iota(jnp.int32, (d, d), 1) < DA_HEAD_DIM
    sub_avg = jnp.where(half_r == half_c, 1.0 / DA_HEAD_DIM, 0.0).astype(BF16)

    def sub_head_norm(xf):
        return xf * lax.rsqrt(_lane_group_mean(xf * xf, sub_avg) + EPS)

    @pl.when(i == 0)
    def _():
        for hh in range(hb):
            hs = slice(hh * d, (hh + 1) * d)
            kn_ref[hh] = (sub_head_norm(k_ref[:, hs].astype(F32)) * kw_ref[...]).astype(BF16)
            for t in range(n_tiles):
                vt_ref[hh, t, :d] = v_ref[t * tq:(t + 1) * tq, hs].astype(F32).T.astype(BF16)
                vt_ref[hh, t, d:] = jnp.where(ones_row, 1.0, 0.0).astype(BF16)

    for hh in range(hb):
        qn = sub_head_norm(q_ref[:, hh * d:(hh + 1) * d].astype(F32)) * qw_ref[...] * (DA_HEAD_DIM ** -0.5 * LOG2E)
        qcat_ref[hh] = jnp.concatenate(
            [jnp.where(first, qn, 0.0).T, jnp.where(first, 0.0, qn).T], axis=1).astype(BF16)
    m_ref[...] = jnp.full(m_ref.shape, -jnp.inf, F32)
    acc_ref[...] = jnp.zeros(acc_ref.shape, F32)

    def scores(j, hh):
        start = pl.multiple_of(j * tq, tq)
        return _dot(kn_ref[hh, pl.ds(start, tq), :], qcat_ref[hh])

    lam_p = lam_ref[...]
    lam = (jnp.exp(jnp.sum(lam_p[0:1] * lam_p[1:2], axis=-1, keepdims=True))
           - jnp.exp(jnp.sum(lam_p[2:3] * lam_p[3:4], axis=-1, keepdims=True)) + lambda_init)

    def tile(j, cur, last):
        if last:
            key = lax.broadcasted_iota(jnp.int32, (tq, 2 * tq), 0)
            qry = lax.broadcasted_iota(jnp.int32, (tq, 2 * tq), 1)
            causal = key <= jnp.where(qry >= tq, qry - tq, qry)
        for hh in range(hb):
            if not last:
                s_ref[1 - cur, hh] = scores(j + 1, hh)
            s = s_ref[cur, hh]
            if last:
                s = jnp.where(causal, s, -jnp.inf)
            m_prev = m_ref[hh]
            m_new = jnp.maximum(m_prev, jnp.max(s, axis=0, keepdims=True))
            alpha = jnp.exp2(m_prev - m_new)
            p = jnp.exp2(s - m_new)
            acc = alpha * acc_ref[hh] + _dot(vt_ref[hh, j], p.astype(BF16))
            if not last:
                acc_ref[hh] = acc
                m_ref[hh] = m_new
            else:
                ot = acc[:d] / acc[d:d + 1]
                o = (ot[:, :tq] - lam * ot[:, tq:]).T
                o = o * _rms_scale(o) * sw_ref[...] * (1.0 - lambda_init)
                o_ref[:, hh * d:(hh + 1) * d] = o.astype(o_ref.dtype)

    for hh in range(hb):
        s_ref[0, hh] = scores(0, hh)

    def body(jj, carry):
        tile(2 * jj, 0, False)
        tile(2 * jj + 1, 1, False)
        return carry

    lax.fori_loop(0, i // 2, body, 0)

    @pl.when(i % 2 == 0)
    def _():
        tile(i, 0, True)

    @pl.when(i % 2 == 1)
    def _():
        tile(i - 1, 0, False)
        tile(i, 1, True)


def _diff_attn(proj3, qw2, kw2, lam_p, sw, lambda_init, q_col, k_col, v_col, cast_ws, layer, tq=256, hb=8):
    b, s, _ = proj3.shape
    d = DA_V_DIM
    nh, nq = DA_HEADS // hb, s // tq
    steps = b * nh * nq
    kern = functools.partial(_diff_attn_kernel, tq=tq, hb=hb, lambda_init=lambda_init, n_cast=len(cast_ws))
    vec = lambda shape: pl.BlockSpec(shape, lambda bi, h, i: (0, 0))
    cast_in, cast_out, cast_shapes = [], [], []
    for w in cast_ws:
        rows, cols = w.shape[1:]
        rb = rows // steps
        assert rows % steps == 0 and rb % (2 * SUBLANES) == 0
        cast_in.append(pl.BlockSpec((None, rb, cols), lambda bi, h, i: (layer, (bi * nh + h) * nq + i, 0)))
        cast_out.append(pl.BlockSpec((rb, cols), lambda bi, h, i: ((bi * nh + h) * nq + i, 0)))
        cast_shapes.append(jax.ShapeDtypeStruct((rows, cols), BF16))
    outs = pl.pallas_call(
        kern,
        out_shape=[jax.ShapeDtypeStruct((b, s, DA_HEADS * d), BF16)] + cast_shapes,
        grid=(b, nh, nq),
        in_specs=[
            pl.BlockSpec((None, tq, hb * d), lambda bi, h, i: (bi, i, q_col // hb + h)),
            pl.BlockSpec((None, s, hb * d), lambda bi, h, i: (bi, 0, k_col // hb + h)),
            pl.BlockSpec((None, s, hb * d), lambda bi, h, i: (bi, 0, v_col // hb + h)),
            vec((1, d)), vec((1, d)), vec((4, DA_HEAD_DIM)), vec((1, d)),
        ] + cast_in,
        out_specs=[pl.BlockSpec((None, tq, hb * d), lambda bi, h, i: (bi, i, h))] + cast_out,
        scratch_shapes=[
            pltpu.VMEM((hb, s, d), BF16),
            pltpu.VMEM((hb, s // tq, d + DEN_ROWS, tq), BF16),
            pltpu.VMEM((hb, d, 2 * tq), BF16),
            pltpu.VMEM((hb, 1, 2 * tq), F32),
            pltpu.VMEM((hb, d + DEN_ROWS, 2 * tq), F32),
            pltpu.VMEM((2, hb, tq, 2 * tq), F32),
        ],
        compiler_params=pltpu.CompilerParams(
            dimension_semantics=("parallel", "parallel", "arbitrary"), vmem_limit_bytes=BIG_VMEM_LIMIT),
        name="diff_attn",
    )(proj3, proj3, proj3, qw2, kw2, lam_p, sw, *cast_ws)
    return outs[0], outs[1:]


def _ssd_kernel(z_ref, xs_ref, bc_ref, xsn_ref, bcn_ref, dt_ref, cwx_ref, cbx_ref, cwbc_ref, cbbc_ref,
                dtb_ref, alog_ref, dx_ref, nw_ref, e_ref, *rest, n_cast):
    cast_in, o_ref, cast_out = rest[:n_cast], rest[n_cast], rest[n_cast + 1:2 * n_cast + 1]
    state_ref, cxa_ref, cba_ref, cxb_ref, cbb_ref, y_ref, dtx_ref, acsx_ref = rest[2 * n_cast + 1:]
    L, N, G = SSD_CHUNK, SSD_STATE, SSD_GROUPS
    w = o_ref.shape[-1]
    bcw = bc_ref.shape[-1]
    gw = w // G
    c = pl.program_id(1)

    ext = CONV_TAIL + L
    s_row = lax.broadcasted_iota(jnp.int32, (L, SSD_CONV * ext), 0)
    s_col = lax.broadcasted_iota(jnp.int32, (L, SSD_CONV * ext), 1)
    pick = jnp.zeros((L, SSD_CONV * ext), F32)
    for k in range(SSD_CONV):
        pick = jnp.where(s_col == s_row + (k * ext + CONV_TAIL - k), 0.5, pick)
    pick = pick.astype(BF16)

    def conv_silu(rows, w_ref, b_ref, cs, dst_ref):
        taps = jnp.concatenate(
            [rows * w_ref[SSD_CONV - 1 - k:SSD_CONV - k, cs].astype(BF16) for k in range(SSD_CONV)], axis=0)
        h = _dot(pick, taps) + 0.5 * b_ref[:, cs]
        dst_ref[:, cs] = h + h * jnp.tanh(h)

    def conv_slice(g, rows_of, dst_x_ref, dst_bc_ref):
        xc = slice(g * (w // G), (g + 1) * (w // G))
        bc = slice(g * (bcw // G), (g + 1) * (bcw // G))
        conv_silu(rows_of(xs_ref, xsn_ref, xc), cwx_ref, cbx_ref, xc, dst_x_ref)
        conv_silu(rows_of(bc_ref, bcn_ref, bc), cwbc_ref, cbbc_ref, bc, dst_bc_ref)

    def rows_first(cur_ref, nxt_ref, cs):
        return jnp.concatenate([jnp.zeros((CONV_TAIL, cs.stop - cs.start), BF16), cur_ref[0:L, cs]], axis=0)

    def rows_second(cur_ref, nxt_ref, cs):
        return cur_ref[L - CONV_TAIL:2 * L, cs]

    def rows_next(cur_ref, nxt_ref, cs):
        return jnp.concatenate([cur_ref[2 * L - CONV_TAIL:2 * L, cs], nxt_ref[:, cs]], axis=0)

    @pl.when(c == 0)
    def _():
        state_ref[...] = jnp.zeros(state_ref.shape, F32)
        for g in range(G):
            conv_slice(g, rows_first, cxa_ref, cba_ref)

    r_i = lax.broadcasted_iota(jnp.int32, (L, L), 0)
    c_i = lax.broadcasted_iota(jnp.int32, (L, L), 1)
    causal = c_i <= r_i
    tril = jnp.where(causal, 1.0, 0.0).astype(BF16)
    lane = lax.broadcasted_iota(jnp.int32, (1, LANES), 1)
    first = lane < SSD_HEAD_DIM
    n_heads = w // SSD_HEAD_DIM
    heads_per_group = gw // SSD_HEAD_DIM

    def expand(v, parts):
        packed, r = None, jnp.where(lane < n_heads, v, 0.0)
        for t in range(parts):
            p = r.astype(BF16).astype(F32)
            packed = p if t == 0 else packed + pltpu.roll(p, t * n_heads, 1)
            r = r - p
        return _dot(packed.astype(BF16), e_ref[...])

    def mix_chunk(r0, cx_ref, cb_ref, side):
        rows = slice(r0, r0 + L)
        dtx = dt_ref[rows, :] + dtb_ref[...]
        dt = jnp.maximum(dtx, 0.0) + jnp.log1p(jnp.exp(-jnp.abs(dtx)))
        a = -jnp.exp(alog_ref[...]) * LOG2E
        acs = sum(_dot(tril, p) for p in _split_bf16(dt * a, 3))
        acs_t = acs.T
        dtx_ref[...] = expand(dt, 2)
        acsx_ref[...] = expand(acs, 3)
        for g in range(G):
            side(g)
            cols = slice(g * gw, (g + 1) * gw)
            xs_g = cx_ref[:, cols]
            acs_g = acsx_ref[:, cols]
            last_g = acsx_ref[L - 1:L, cols]
            xdt_g = xs_g * dtx_ref[:, cols]
            xds_g = (xdt_g * jnp.exp2(last_g - acs_g)).astype(BF16)
            b_g = cb_ref[:, g * N:(g + 1) * N]
            c_g = cb_ref[:, (G + g) * N:(G + g + 1) * N].astype(BF16)
            cb = _dot_nt(c_g, b_g.astype(BF16))
            state_old = state_ref[:, cols]
            y_off = _dot(c_g, state_old.astype(BF16)) * jnp.exp2(acs_g)
            state_ref[:, cols] = state_old * jnp.exp2(last_g) + _dot(b_g.T.astype(BF16), xds_g)
            for pair in range(heads_per_group // 2):
                h0 = g * heads_per_group + 2 * pair
                pc = slice(pair * LANES, (pair + 1) * LANES)
                xp = xdt_g[:, pc]
                x_pair = jnp.concatenate([jnp.where(first, xp, 0.0), jnp.where(first, 0.0, xp)], axis=0).astype(BF16)
                mats = []
                for h in (h0, h0 + 1):
                    seg = acs[:, h:h + 1] - acs_t[h:h + 1, :]
                    dec = jnp.exp2(jnp.where(causal, seg, -jnp.inf))
                    mats.append((cb * dec).astype(BF16))
                y_ref[:, pc] = _dot(jnp.concatenate(mats, axis=1), x_pair)
            y = y_ref[...] + y_off + xs_g * dx_ref[:, cols]
            y = y * _silu(z_ref[rows, cols].astype(F32))
            o_ref[rows, cols] = (y * _rms_scale(y) * nw_ref[:, cols]).astype(o_ref.dtype)

    mix_chunk(0, cxa_ref, cba_ref, lambda g: conv_slice(g, rows_second, cxb_ref, cbb_ref))
    for w_ref, wb_ref in zip(cast_in, cast_out):
        wb_ref[...] = w_ref[...].astype(BF16)
    mix_chunk(L, cxb_ref, cbb_ref, lambda g: conv_slice(g, rows_next, cxa_ref, cba_ref))


def _ssd(proj3, dt3, cwx, cbx, cwbc, cbbc, dtb, alog, dx, nw, e, z_col, xs_col, bc_col, cast_ws, layer):
    b, s, _ = proj3.shape
    w = e.shape[1]
    bcw = cwbc.shape[1]
    L = SSD_CHUNK
    nc = s // L
    assert nc % 2 == 0
    steps = b * (nc // 2)
    cast_in, cast_out, cast_shapes = [], [], []
    for cw in cast_ws:
        rows, cols = cw.shape[1:]
        rb = rows // steps
        assert rows % steps == 0 and rb % (2 * SUBLANES) == 0
        cast_in.append(pl.BlockSpec((None, rb, cols), lambda bi, c: (layer, bi * (nc // 2) + c, 0)))
        cast_out.append(pl.BlockSpec((rb, cols), lambda bi, c: (bi * (nc // 2) + c, 0)))
        cast_shapes.append(jax.ShapeDtypeStruct((rows, cols), BF16))
    full = lambda shape: pl.BlockSpec(shape, lambda bi, c: (0, 0))
    nxt = lambda c: jnp.minimum(2 * c + 2, nc - 1)
    outs = pl.pallas_call(
        functools.partial(_ssd_kernel, n_cast=len(cast_ws)),
        out_shape=[jax.ShapeDtypeStruct((b, s, w), BF16)] + cast_shapes,
        grid=(b, nc // 2),
        in_specs=[
            pl.BlockSpec((None, 2 * L, w), lambda bi, c: (bi, c, z_col)),
            pl.BlockSpec((None, 2 * L, w), lambda bi, c: (bi, c, xs_col)),
            pl.BlockSpec((None, 2 * L, bcw), lambda bi, c: (bi, c, bc_col)),
            pl.BlockSpec((None, L, w), lambda bi, c: (bi, nxt(c), xs_col)),
            pl.BlockSpec((None, L, bcw), lambda bi, c: (bi, nxt(c), bc_col)),
            pl.BlockSpec((None, 2 * L, LANES), lambda bi, c: (bi, c, 0)),
            full((SSD_CONV, w)), full((1, w)), full((SSD_CONV, bcw)), full((1, bcw)),
            full((1, LANES)), full((1, LANES)), full((1, w)), full((1, w)), full((LANES, w)),
        ] + cast_in,
        out_specs=[pl.BlockSpec((None, 2 * L, w), lambda bi, c: (bi, c, 0))] + cast_out,
        scratch_shapes=[
            pltpu.VMEM((SSD_STATE, w), F32),
            pltpu.VMEM((L, w), F32), pltpu.VMEM((L, bcw), F32),
            pltpu.VMEM((L, w), F32), pltpu.VMEM((L, bcw), F32),
            pltpu.VMEM((L, w // SSD_GROUPS), F32),
            pltpu.VMEM((L, w), F32), pltpu.VMEM((L, w), F32),
        ],
        compiler_params=pltpu.CompilerParams(
            dimension_semantics=("parallel", "arbitrary"), vmem_limit_bytes=VMEM_LIMIT),
        name="ssd",
    )(proj3, proj3, proj3, proj3, proj3, dt3, cwx, cbx, cwbc, cbbc, dtb, alog, dx, nw, e, *cast_ws)
    return outs[0], outs[1:]


def _mix_out_kernel(a_ref, s_ref, x_ref, w_ref, xnw_ref, wq_ref, kv_ref, qw_ref, wo_ref, o_ref):
    d = X_HEAD_DIM
    ka = a_ref.shape[1]
    h1 = x_ref[...] + _dot(a_ref[...], w_ref[0])
    for c in range(1, w_ref.shape[0]):
        h1 = h1 + _dot(s_ref[:, (c - 1) * ka:c * ka], w_ref[c])
    hn = (h1 * _rms_scale(h1) * xnw_ref[...]).astype(BF16)
    q = _dot(hn, wq_ref[...])
    outs = []
    for hh in range(X_HEADS):
        qf = q[:, hh * d:(hh + 1) * d]
        qn = (qf * _rms_scale(qf) * qw_ref[...] * (d ** -0.5 * LOG2E)).astype(BF16)
        s = _dot_nt(qn, kv_ref[:, hh * d:(hh + 1) * d])
        p = jnp.exp2(s - jnp.max(s, axis=-1, keepdims=True))
        v = kv_ref[:, (X_HEADS + hh) * d:(X_HEADS + hh + 1) * d]
        o = _dot(p.astype(BF16), v) / jnp.sum(p, axis=-1, keepdims=True)
        outs.append(o.astype(BF16))
    o_ref[...] = h1 + _dot(jnp.concatenate(outs, axis=-1), wo_ref[...])


def _mix_out(a2, s2, x2, w_out, xnw, wq, kv3, qw, wo, seq, tm=512):
    m, ka = a2.shape
    ks = s2.shape[1]
    dm = x2.shape[1]
    xw = wq.shape[1]
    mlen = kv3.shape[1]
    assert seq % tm == 0 and ks % ka == 0
    once = dict(pipeline_mode=pl.Buffered(1))
    return pl.pallas_call(
        _mix_out_kernel,
        out_shape=jax.ShapeDtypeStruct((m, dm), F32),
        grid=(m // tm,),
        in_specs=[
            pl.BlockSpec((tm, ka), lambda i: (i, 0)),
            pl.BlockSpec((tm, ks), lambda i: (i, 0)),
            pl.BlockSpec((tm, dm), lambda i: (i, 0)),
            pl.BlockSpec((1 + ks // ka, ka, dm), lambda i: (0, 0, 0), **once),
            pl.BlockSpec((1, dm), lambda i: (0, 0)),
            pl.BlockSpec((dm, xw), lambda i: (0, 0), **once),
            pl.BlockSpec((None, mlen, 2 * xw), lambda i: (i // (seq // tm), 0, 0)),
            pl.BlockSpec((1, X_HEAD_DIM), lambda i: (0, 0)),
            pl.BlockSpec((xw, dm), lambda i: (0, 0), **once),
        ],
        out_specs=pl.BlockSpec((tm, dm), lambda i: (i, 0)),
        compiler_params=pltpu.CompilerParams(
            dimension_semantics=("parallel",), vmem_limit_bytes=BIG_VMEM_LIMIT),
        name="mix_out",
    )(a2, s2, x2, w_out.reshape(1 + ks // ka, ka, dm), xnw, wq, kv3, qw, wo)


def _ffn_kernel(h_ref, nw_ref, wg_ref, wu_ref, wd_ref, o_ref, hn_ref):
    f = pl.program_id(1)

    @pl.when(f == 0)
    def _():
        hf = h_ref[...]
        hn_ref[...] = (hf * _rms_scale(hf) * nw_ref[...]).astype(BF16)
        o_ref[...] = hf

    hn = hn_ref[...]
    act = (_silu(_dot(hn, wg_ref[...])) * _dot(hn, wu_ref[...])).astype(BF16)
    o_ref[...] += _dot(act, wd_ref[...])


def _ffn(h2, nw, wg, wu, wd, tm=1024, tf=512):
    m, k = h2.shape
    dff = wg.shape[1]
    return pl.pallas_call(
        _ffn_kernel,
        out_shape=jax.ShapeDtypeStruct((m, k), F32),
        grid=(m // tm, dff // tf),
        in_specs=[
            pl.BlockSpec((tm, k), lambda i, f: (i, 0), pipeline_mode=pl.Buffered(1)),
            pl.BlockSpec((1, k), lambda i, f: (0, 0)),
            pl.BlockSpec((k, tf), lambda i, f: (0, f)),
            pl.BlockSpec((k, tf), lambda i, f: (0, f)),
            pl.BlockSpec((tf, k), lambda i, f: (f, 0)),
        ],
        out_specs=pl.BlockSpec((tm, k), lambda i, f: (i, 0)),
        scratch_shapes=[pltpu.VMEM((tm, k), BF16)],
        compiler_params=pltpu.CompilerParams(
            dimension_semantics=("parallel", "arbitrary"), vmem_limit_bytes=BIG_VMEM_LIMIT),
        name="ffn",
    )(h2, nw, wg, wu, wd)


def _pad_lanes(v, fill=0.0):
    return jnp.pad(v, ((0, 0), (0, LANES - v.shape[1])), constant_values=fill)


def kernel(x, mem, mix_norm_w, w_in, da_q_norm_w, da_k_norm_w, da_lambda_q1, da_lambda_k1, da_lambda_q2, da_lambda_k2, da_subln_w, ssd_conv_w, ssd_conv_b, ssd_dt_bias, ssd_a_log, ssd_d, ssd_norm_w, w_out, xattn_norm_w, mem_norm_w, xattn_w_q, xattn_w_kv, xattn_q_norm_w, xattn_k_norm_w, xattn_w_o, ffn_norm_w, ffn_w_gate, ffn_w_up, ffn_w_down):
    b, s, dm = x.shape
    mlen = mem.shape[1]
    depth = w_in.shape[0]
    da_w = DA_HEADS * DA_V_DIM
    ssd_w = ssd_norm_w.shape[1]
    ssd_heads = ssd_w // SSD_HEAD_DIM
    gn = SSD_GROUPS * SSD_STATE
    xw = X_HEADS * X_HEAD_DIM
    o_q, o_k, o_v, o_z = 0, da_w, 2 * da_w, 3 * da_w
    o_xs = o_z + ssd_w
    o_bc = o_xs + ssd_w
    o_dt = o_bc + 2 * gn
    z_col, xs_col = 0, 1
    bc_col = (2 * ssd_w) // (2 * gn)
    q_col = (2 * ssd_w + 2 * gn) // DA_V_DIM
    k_col = q_col + DA_HEADS
    v_col = k_col + DA_HEADS
    assert (2 * ssd_w) % (2 * gn) == 0 and ssd_heads <= LANES

    assert LANES % ssd_heads == 0 and LANES // ssd_heads >= 3
    expand = np.zeros((LANES, ssd_w), np.float32)
    for r in range(LANES):
        h = r % ssd_heads
        expand[r, h * SSD_HEAD_DIM:(h + 1) * SSD_HEAD_DIM] = 1.0
    expand = jnp.asarray(expand, BF16)

    h = x.reshape(b * s, dm)
    mem2 = mem.reshape(b * mlen, dm)
    for i in range(depth):
        lambda_init = 0.8 - 0.6 * math.exp(-0.3 * i)
        proj, dt_raw = _in_proj(h, mix_norm_w[i][None], jnp.swapaxes(w_in, 1, 2), i, o_dt, o_z)
        proj3 = proj.reshape(b, s, -1)

        lam_p = jnp.stack([da_lambda_q1[i], da_lambda_k1[i], da_lambda_q2[i], da_lambda_k2[i]])
        a_out, _ = _diff_attn(
            proj3, jnp.tile(da_q_norm_w[i], 2)[None], jnp.tile(da_k_norm_w[i], 2)[None],
            lam_p, da_subln_w[i][None], lambda_init, q_col, k_col, v_col, [], i)

        cw, cb = ssd_conv_w[i], ssd_conv_b[i][None]
        s_out, (wo_b, wq_b, xwo_b, wg_b, wu_b, wd_b) = _ssd(
            proj3, dt_raw.reshape(b, s, LANES),
            cw[:, :ssd_w], cb[:, :ssd_w], cw[:, ssd_w:], cb[:, ssd_w:],
            _pad_lanes(ssd_dt_bias[i][None]), _pad_lanes(ssd_a_log[i][None]),
            jnp.repeat(ssd_d[i], SSD_HEAD_DIM)[None], ssd_norm_w[i][None], expand,
            z_col, xs_col, bc_col,
            [w_out, xattn_w_q, xattn_w_o, ffn_w_gate, ffn_w_up, ffn_w_down], i)

        xkv = _xkv_proj(mem2, mem_norm_w[i][None], xattn_w_kv, i, xattn_k_norm_w[i][None])
        h2 = _mix_out(a_out.reshape(b * s, da_w), s_out.reshape(b * s, ssd_w), h, wo_b,
                      xattn_norm_w[i][None], wq_b, xkv.reshape(b, mlen, 2 * xw), xattn_q_norm_w[i][None], xwo_b, s)

        h = _ffn(h2, ffn_norm_w[i][None], wg_b, wu_b, wd_b)
    return h.reshape(b, s, dm)
```

```python
import functools
import math

import jax
import jax.numpy as jnp
import numpy as np
from jax import lax
from jax.experimental import pallas as pl
from jax.experimental.pallas import tpu as pltpu

EPS = 1e-6
LANES = 128
SUBLANES = 8
VMEM_LIMIT = 48 * 1024 * 1024
BIG_VMEM_LIMIT = 56 * 1024 * 1024
LOG2E = math.log2(math.e)
DEN_ROWS = 16
CONV_TAIL = 16

DA_HEADS = 8
DA_HEAD_DIM = 64
DA_V_DIM = 128
SSD_HEAD_DIM = 64
SSD_GROUPS = 4
SSD_STATE = 128
SSD_CONV = 4
SSD_CHUNK = 128
X_HEADS = 4
X_HEAD_DIM = 128

F32 = jnp.float32
BF16 = jnp.bfloat16


def _rms_scale(xf, eps=EPS):
    return lax.rsqrt(jnp.mean(xf * xf, axis=-1, keepdims=True) + eps)


def _silu(x):
    h = 0.5 * x
    return h + h * jnp.tanh(h)


def _split_bf16(x, parts):
    out = []
    r = x
    for _ in range(parts):
        p = r.astype(BF16)
        out.append(p)
        r = r - p.astype(F32)
    return out


def _dot(a, b):
    return jnp.dot(a, b, preferred_element_type=F32)


def _dot_nt(a, b):
    return lax.dot_general(a, b, (((1,), (1,)), ((), ())), preferred_element_type=F32)


def _lane_group_mean(xf, avg):
    hi, lo = _split_bf16(xf, 2)
    return _dot(hi, avg) + _dot(lo, avg)


def _in_proj_kernel(x_ref, nw_ref, wt_ref, wdt_ref, o_ref, dt_ref, hn_ref):
    n, mi = pl.program_id(1), pl.program_id(2)

    @pl.when(n == 0)
    def _():
        xf = x_ref[...]
        hn = (xf * _rms_scale(xf) * nw_ref[...]).astype(BF16)
        hn_ref[mi] = hn
        wdt = wdt_ref[...].astype(BF16)
        wdt = jnp.concatenate([wdt, jnp.zeros((LANES - wdt.shape[0], wdt.shape[1]), BF16)], axis=0)
        dt_ref[...] = _dot_nt(hn, wdt)

    o_ref[...] = _dot_nt(hn_ref[mi], wt_ref[...].astype(BF16)).astype(o_ref.dtype)


def _in_proj(x2, nw, wt_all, layer, n, col_shift, tm=1024, tn=1024, m_inner=2):
    m, k = x2.shape
    n_dt = wt_all.shape[1] - n
    shift, nblk = col_shift // tn, n // tn
    assert col_shift % tn == 0 and n % tn == 0 and n % n_dt == 0 and m % (tm * m_inner) == 0
    row = lambda mo, j, mi: mo * m_inner + jnp.where(j == 0, mi, m_inner - 1)
    return pl.pallas_call(
        _in_proj_kernel,
        out_shape=(jax.ShapeDtypeStruct((m, n), BF16), jax.ShapeDtypeStruct((m, LANES), F32)),
        grid=(m // (tm * m_inner), nblk, m_inner),
        in_specs=[
            pl.BlockSpec((tm, k), lambda mo, j, mi: (row(mo, j, mi), 0)),
            pl.BlockSpec((1, k), lambda mo, j, mi: (0, 0)),
            pl.BlockSpec((None, tn, k), lambda mo, j, mi: (layer, (j + shift) % nblk, 0)),
            pl.BlockSpec((None, n_dt, k), lambda mo, j, mi: (layer, n // n_dt, 0)),
        ],
        out_specs=(
            pl.BlockSpec((tm, tn), lambda mo, j, mi: (mo * m_inner + mi, j)),
            pl.BlockSpec((tm, LANES), lambda mo, j, mi: (row(mo, j, mi), 0)),
        ),
        scratch_shapes=[pltpu.VMEM((m_inner, tm, k), BF16)],
        compiler_params=pltpu.CompilerParams(
            dimension_semantics=("parallel", "arbitrary", "arbitrary"), vmem_limit_bytes=BIG_VMEM_LIMIT),
        name="in_proj",
    )(x2, nw, wt_all, wt_all)


def _xkv_kernel(x_ref, nw_ref, w_ref, kw_ref, o_ref):
    d = X_HEAD_DIM
    xf = x_ref[...]
    hn = (xf * _rms_scale(xf) * nw_ref[...]).astype(BF16)
    kv = _dot(hn, w_ref[...].astype(BF16))
    kw = X_HEADS * d
    for hh in range(X_HEADS):
        kf = kv[:, hh * d:(hh + 1) * d]
        o_ref[:, hh * d:(hh + 1) * d] = (kf * _rms_scale(kf) * kw_ref[...]).astype(o_ref.dtype)
    o_ref[:, kw:] = kv[:, kw:].astype(o_ref.dtype)


def _xkv_proj(x2, nw, w_all, layer, kw, tm=512):
    m, k = x2.shape
    n = w_all.shape[2]
    return pl.pallas_call(
        _xkv_kernel,
        out_shape=jax.ShapeDtypeStruct((m, n), BF16),
        grid=(m // tm,),
        in_specs=[
            pl.BlockSpec((tm, k), lambda i: (i, 0)),
            pl.BlockSpec((1, k), lambda i: (0, 0)),
            pl.BlockSpec((None, k, n), lambda i: (layer, 0, 0)),
            pl.BlockSpec((1, X_HEAD_DIM), lambda i: (0, 0)),
        ],
        out_specs=pl.BlockSpec((tm, n), lambda i: (i, 0)),
        compiler_params=pltpu.CompilerParams(
            dimension_semantics=("parallel",), vmem_limit_bytes=VMEM_LIMIT),
        name="xkv_proj",
    )(x2, nw, w_all, kw)


def _diff_attn_kernel(q_ref, k_ref, v_ref, qw_ref, kw_ref, lam_ref, sw_ref, *rest,
                      tq, hb, lambda_init, n_cast):
    cast_in, o_ref, cast_out = rest[:n_cast], rest[n_cast], rest[n_cast + 1:2 * n_cast + 1]
    kn_ref, vt_ref, qcat_ref, m_ref, acc_ref, s_ref = rest[2 * n_cast + 1:]
    for w_ref, wb_ref in zip(cast_in, cast_out):
        wb_ref[...] = w_ref[...].astype(BF16)
    i = pl.program_id(2)
    n_tiles = vt_ref.shape[1]
    d = DA_V_DIM
    lane = lax.broadcasted_iota(jnp.int32, (1, d), 1)
    first = lane < DA_HEAD_DIM
    ones_row = lax.broadcasted_iota(jnp.int32, (DEN_ROWS, tq), 0) == 0

    half_r = lax.broadcasted_iota(jnp.int32, (d, d), 0) < DA_HEAD_DIM
    half_c = lax.broadcasted_iota(jnp.int32, (d, d), 1) < DA_HEAD_DIM
    sub_avg = jnp.where(half_r == half_c, 1.0 / DA_HEAD_DIM, 0.0).astype(BF16)

    def sub_head_norm(xf):
        return xf * lax.rsqrt(_lane_group_mean(xf * xf, sub_avg) + EPS)

    @pl.when(i == 0)
    def _():
        for hh in range(hb):
            hs = slice(hh * d, (hh + 1) * d)
            kn_ref[hh] = (sub_head_norm(k_ref[:, hs].astype(F32)) * kw_ref[...]).astype(BF16)
            for t in range(n_tiles):
                vt_ref[hh, t, :d] = v_ref[t * tq:(t + 1) * tq, hs].T
                vt_ref[hh, t, d:] = jnp.where(ones_row, 1.0, 0.0).astype(BF16)

    for hh in range(hb):
        qn = sub_head_norm(q_ref[:, hh * d:(hh + 1) * d].astype(F32)) * qw_ref[...] * (DA_HEAD_DIM ** -0.5 * LOG2E)
        qcat_ref[hh] = jnp.concatenate(
            [jnp.where(first, qn, 0.0).T, jnp.where(first, 0.0, qn).T], axis=1).astype(BF16)
    m_ref[...] = jnp.full(m_ref.shape, -jnp.inf, F32)
    acc_ref[...] = jnp.zeros(acc_ref.shape, F32)

    def scores(j, hh):
        start = pl.multiple_of(j * tq, tq)
        return _dot(kn_ref[hh, pl.ds(start, tq), :], qcat_ref[hh])

    lam_p = lam_ref[...]
    lam = (jnp.exp(jnp.sum(lam_p[0:1] * lam_p[1:2], axis=-1, keepdims=True))
           - jnp.exp(jnp.sum(lam_p[2:3] * lam_p[3:4], axis=-1, keepdims=True)) + lambda_init)

    def tile(j, cur, last):
        if last:
            key = lax.broadcasted_iota(jnp.int32, (tq, 2 * tq), 0)
            qry = lax.broadcasted_iota(jnp.int32, (tq, 2 * tq), 1)
            causal = key <= jnp.where(qry >= tq, qry - tq, qry)
        for hh in range(hb):
            if not last:
                s_ref[1 - cur, hh] = scores(j + 1, hh)
            s = s_ref[cur, hh]
            if last:
                s = jnp.where(causal, s, -jnp.inf)
            m_prev = m_ref[hh]
            m_new = jnp.maximum(m_prev, jnp.max(s, axis=0, keepdims=True))
            alpha = jnp.exp2(m_prev - m_new)
            p = jnp.exp2(s - m_new)
            acc = alpha * acc_ref[hh] + _dot(vt_ref[hh, j], p.astype(BF16))
            if not last:
                acc_ref[hh] = acc
                m_ref[hh] = m_new
            else:
                ot = acc[:d] / acc[d:d + 1]
                o = (ot[:, :tq] - lam * ot[:, tq:]).T
                o = o * _rms_scale(o) * sw_ref[...] * (1.0 - lambda_init)
                o_ref[:, hh * d:(hh + 1) * d] = o.astype(o_ref.dtype)

    for hh in range(hb):
        s_ref[0, hh] = scores(0, hh)

    def body(jj, carry):
        tile(2 * jj, 0, False)
        tile(2 * jj + 1, 1, False)
        return carry

    lax.fori_loop(0, i // 2, body, 0)

    @pl.when(i % 2 == 0)
    def _():
        tile(i, 0, True)

    @pl.when(i % 2 == 1)
    def _():
        tile(i - 1, 0, False)
        tile(i, 1, True)


def _diff_attn(proj3, qw2, kw2, lam_p, sw, lambda_init, q_col, k_col, v_col, cast_ws, layer, tq=256, hb=8):
    b, s, _ = proj3.shape
    d = DA_V_DIM
    nh, nq = DA_HEADS // hb, s // tq
    steps = b * nh * nq
    kern = functools.partial(_diff_attn_kernel, tq=tq, hb=hb, lambda_init=lambda_init, n_cast=len(cast_ws))
    vec = lambda shape: pl.BlockSpec(shape, lambda bi, h, i: (0, 0))
    cast_in, cast_out, cast_shapes = [], [], []
    for w in cast_ws:
        rows, cols = w.shape[1:]
        rb = rows // steps
        assert rows % steps == 0 and rb % (2 * SUBLANES) == 0
        cast_in.append(pl.BlockSpec((None, rb, cols), lambda bi, h, i: (layer, (bi * nh + h) * nq + i, 0)))
        cast_out.append(pl.BlockSpec((rb, cols), lambda bi, h, i: ((bi * nh + h) * nq + i, 0)))
        cast_shapes.append(jax.ShapeDtypeStruct((rows, cols), BF16))
    outs = pl.pallas_call(
        kern,
        out_shape=[jax.ShapeDtypeStruct((b, s, DA_HEADS * d), BF16)] + cast_shapes,
        grid=(b, nh, nq),
        in_specs=[
            pl.BlockSpec((None, tq, hb * d), lambda bi, h, i: (bi, i, q_col // hb + h)),
            pl.BlockSpec((None, s, hb * d), lambda bi, h, i: (bi, 0, k_col // hb + h)),
            pl.BlockSpec((None, s, hb * d), lambda bi, h, i: (bi, 0, v_col // hb + h)),
            vec((1, d)), vec((1, d)), vec((4, DA_HEAD_DIM)), vec((1, d)),
        ] + cast_in,
        out_specs=[pl.BlockSpec((None, tq, hb * d), lambda bi, h, i: (bi, i, h))] + cast_out,
        scratch_shapes=[
            pltpu.VMEM((hb, s, d), BF16),
            pltpu.VMEM((hb, s // tq, d + DEN_ROWS, tq), BF16),
            pltpu.VMEM((hb, d, 2 * tq), BF16),
            pltpu.VMEM((hb, 1, 2 * tq), F32),
            pltpu.VMEM((hb, d + DEN_ROWS, 2 * tq), F32),
            pltpu.VMEM((2, hb, tq, 2 * tq), F32),
        ],
        compiler_params=pltpu.CompilerParams(
            dimension_semantics=("parallel", "parallel", "arbitrary"), vmem_limit_bytes=BIG_VMEM_LIMIT),
        name="diff_attn",
    )(proj3, proj3, proj3, qw2, kw2, lam_p, sw, *cast_ws)
    return outs[0], outs[1:]


def _ssd_kernel(z_ref, xs_ref, bc_ref, xsn_ref, bcn_ref, dt_ref, cwx_ref, cbx_ref, cwbc_ref, cbbc_ref,
                dtb_ref, alog_ref, dx_ref, nw_ref, e_ref, *rest, n_cast):
    cast_in, o_ref, cast_out = rest[:n_cast], rest[n_cast], rest[n_cast + 1:2 * n_cast + 1]
    state_ref, cxa_ref, cba_ref, cxb_ref, cbb_ref, y_ref, dtx_ref, acsx_ref = rest[2 * n_cast + 1:]
    L, N, G = SSD_CHUNK, SSD_STATE, SSD_GROUPS
    w = o_ref.shape[-1]
    bcw = bc_ref.shape[-1]
    gw = w // G
    c = pl.program_id(1)

    ext = CONV_TAIL + L
    s_row = lax.broadcasted_iota(jnp.int32, (L, SSD_CONV * ext), 0)
    s_col = lax.broadcasted_iota(jnp.int32, (L, SSD_CONV * ext), 1)
    pick = jnp.zeros((L, SSD_CONV * ext), F32)
    for k in range(SSD_CONV):
        pick = jnp.where(s_col == s_row + (k * ext + CONV_TAIL - k), 0.5, pick)
    pick = pick.astype(BF16)

    def conv_silu(rows, w_ref, b_ref, cs, dst_ref):
        taps = jnp.concatenate(
            [rows * w_ref[SSD_CONV - 1 - k:SSD_CONV - k, cs].astype(BF16) for k in range(SSD_CONV)], axis=0)
        h = _dot(pick, taps) + 0.5 * b_ref[:, cs]
        dst_ref[:, cs] = h + h * jnp.tanh(h)

    def conv_slice(g, rows_of, dst_x_ref, dst_bc_ref):
        xc = slice(g * (w // G), (g + 1) * (w // G))
        bc = slice(g * (bcw // G), (g + 1) * (bcw // G))
        conv_silu(rows_of(xs_ref, xsn_ref, xc), cwx_ref, cbx_ref, xc, dst_x_ref)
        conv_silu(rows_of(bc_ref, bcn_ref, bc), cwbc_ref, cbbc_ref, bc, dst_bc_ref)

    def rows_first(cur_ref, nxt_ref, cs):
        return jnp.concatenate([jnp.zeros((CONV_TAIL, cs.stop - cs.start), BF16), cur_ref[0:L, cs]], axis=0)

    def rows_second(cur_ref, nxt_ref, cs):
        return cur_ref[L - CONV_TAIL:2 * L, cs]

    def rows_next(cur_ref, nxt_ref, cs):
        return jnp.concatenate([cur_ref[2 * L - CONV_TAIL:2 * L, cs], nxt_ref[:, cs]], axis=0)

    @pl.when(c == 0)
    def _():
        state_ref[...] = jnp.zeros(state_ref.shape, F32)
        for g in range(G):
            conv_slice(g, rows_first, cxa_ref, cba_ref)

    r_i = lax.broadcasted_iota(jnp.int32, (L, L), 0)
    c_i = lax.broadcasted_iota(jnp.int32, (L, L), 1)
    causal = c_i <= r_i
    tril = jnp.where(causal, 1.0, 0.0).astype(BF16)
    lane = lax.broadcasted_iota(jnp.int32, (1, LANES), 1)
    first = lane < SSD_HEAD_DIM
    n_heads = w // SSD_HEAD_DIM
    heads_per_group = gw // SSD_HEAD_DIM

    def expand(v, parts):
        packed, r = None, jnp.where(lane < n_heads, v, 0.0)
        for t in range(parts):
            p = r.astype(BF16).astype(F32)
            packed = p if t == 0 else packed + pltpu.roll(p, t * n_heads, 1)
            r = r - p
        return _dot(packed.astype(BF16), e_ref[...])

    def mix_chunk(r0, cx_ref, cb_ref, side):
        rows = slice(r0, r0 + L)
        dtx = dt_ref[rows, :] + dtb_ref[...]
        dt = jnp.maximum(dtx, 0.0) + jnp.log1p(jnp.exp(-jnp.abs(dtx)))
        a = -jnp.exp(alog_ref[...]) * LOG2E
        acs = sum(_dot(tril, p) for p in _split_bf16(dt * a, 3))
        acs_t = acs.T
        dtx_ref[...] = expand(dt, 2)
        acsx_ref[...] = expand(acs, 3)
        for g in range(G):
            side(g)
            cols = slice(g * gw, (g + 1) * gw)
            xs_g = cx_ref[:, cols]
            acs_g = acsx_ref[:, cols]
            last_g = acsx_ref[L - 1:L, cols]
            xdt_g = xs_g * dtx_ref[:, cols]
            xds_g = (xdt_g * jnp.exp2(last_g - acs_g)).astype(BF16)
            b_g = cb_ref[:, g * N:(g + 1) * N]
            c_g = cb_ref[:, (G + g) * N:(G + g + 1) * N].astype(BF16)
            cb = _dot_nt(c_g, b_g.astype(BF16))
            state_old = state_ref[:, cols]
            y_off = _dot(c_g, state_old.astype(BF16)) * jnp.exp2(acs_g)
            state_ref[:, cols] = state_old * jnp.exp2(last_g) + _dot(b_g.T.astype(BF16), xds_g)
            for pair in range(heads_per_group // 2):
                h0 = g * heads_per_group + 2 * pair
                pc = slice(pair * LANES, (pair + 1) * LANES)
                xp = xdt_g[:, pc]
                x_pair = jnp.concatenate([jnp.where(first, xp, 0.0), jnp.where(first, 0.0, xp)], axis=0).astype(BF16)
                mats = []
                for h in (h0, h0 + 1):
                    seg = acs[:, h:h + 1] - acs_t[h:h + 1, :]
                    dec = jnp.exp2(jnp.where(causal, seg, -jnp.inf))
                    mats.append((cb * dec).astype(BF16))
                y_ref[:, pc] = _dot(jnp.concatenate(mats, axis=1), x_pair)
            y = y_ref[...] + y_off + xs_g * dx_ref[:, cols]
            y = y * _silu(z_ref[rows, cols].astype(F32))
            o_ref[rows, cols] = (y * _rms_scale(y) * nw_ref[:, cols]).astype(o_ref.dtype)

    mix_chunk(0, cxa_ref, cba_ref, lambda g: conv_slice(g, rows_second, cxb_ref, cbb_ref))
    for w_ref, wb_ref in zip(cast_in, cast_out):
        wb_ref[...] = w_ref[...].astype(BF16)
    mix_chunk(L, cxb_ref, cbb_ref, lambda g: conv_slice(g, rows_next, cxa_ref, cba_ref))


def _ssd(proj3, dt3, cwx, cbx, cwbc, cbbc, dtb, alog, dx, nw, e, z_col, xs_col, bc_col, cast_ws, layer):
    b, s, _ = proj3.shape
    w = e.shape[1]
    bcw = cwbc.shape[1]
    L = SSD_CHUNK
    nc = s // L
    assert nc % 2 == 0
    steps = b * (nc // 2)
    cast_in, cast_out, cast_shapes = [], [], []
    for cw in cast_ws:
        rows, cols = cw.shape[1:]
        rb = rows // steps
        assert rows % steps == 0 and rb % (2 * SUBLANES) == 0
        cast_in.append(pl.BlockSpec((None, rb, cols), lambda bi, c: (layer, bi * (nc // 2) + c, 0)))
        cast_out.append(pl.BlockSpec((rb, cols), lambda bi, c: (bi * (nc // 2) + c, 0)))
        cast_shapes.append(jax.ShapeDtypeStruct((rows, cols), BF16))
    full = lambda shape: pl.BlockSpec(shape, lambda bi, c: (0, 0))
    nxt = lambda c: jnp.minimum(2 * c + 2, nc - 1)
    outs = pl.pallas_call(
        functools.partial(_ssd_kernel, n_cast=len(cast_ws)),
        out_shape=[jax.ShapeDtypeStruct((b, s, w), BF16)] + cast_shapes,
        grid=(b, nc // 2),
        in_specs=[
            pl.BlockSpec((None, 2 * L, w), lambda bi, c: (bi, c, z_col)),
            pl.BlockSpec((None, 2 * L, w), lambda bi, c: (bi, c, xs_col)),
            pl.BlockSpec((None, 2 * L, bcw), lambda bi, c: (bi, c, bc_col)),
            pl.BlockSpec((None, L, w), lambda bi, c: (bi, nxt(c), xs_col)),
            pl.BlockSpec((None, L, bcw), lambda bi, c: (bi, nxt(c), bc_col)),
            pl.BlockSpec((None, 2 * L, LANES), lambda bi, c: (bi, c, 0)),
            full((SSD_CONV, w)), full((1, w)), full((SSD_CONV, bcw)), full((1, bcw)),
            full((1, LANES)), full((1, LANES)), full((1, w)), full((1, w)), full((LANES, w)),
        ] + cast_in,
        out_specs=[pl.BlockSpec((None, 2 * L, w), lambda bi, c: (bi, c, 0))] + cast_out,
        scratch_shapes=[
            pltpu.VMEM((SSD_STATE, w), F32),
            pltpu.VMEM((L, w), F32), pltpu.VMEM((L, bcw), F32),
            pltpu.VMEM((L, w), F32), pltpu.VMEM((L, bcw), F32),
            pltpu.VMEM((L, w // SSD_GROUPS), F32),
            pltpu.VMEM((L, w), F32), pltpu.VMEM((L, w), F32),
        ],
        compiler_params=pltpu.CompilerParams(
            dimension_semantics=("parallel", "arbitrary"), vmem_limit_bytes=VMEM_LIMIT),
        name="ssd",
    )(proj3, proj3, proj3, proj3, proj3, dt3, cwx, cbx, cwbc, cbbc, dtb, alog, dx, nw, e, *cast_ws)
    return outs[0], outs[1:]


def _mix_out_kernel(a_ref, s_ref, x_ref, w_ref, xnw_ref, wq_ref, kv_ref, qw_ref, wo_ref, o_ref):
    d = X_HEAD_DIM
    ka = a_ref.shape[1]
    h1 = x_ref[...] + _dot(a_ref[...], w_ref[0])
    for c in range(1, w_ref.shape[0]):
        h1 = h1 + _dot(s_ref[:, (c - 1) * ka:c * ka], w_ref[c])
    hn = (h1 * _rms_scale(h1) * xnw_ref[...]).astype(BF16)
    q = _dot(hn, wq_ref[...])
    outs = []
    for hh in range(X_HEADS):
        qf = q[:, hh * d:(hh + 1) * d]
        qn = (qf * _rms_scale(qf) * qw_ref[...] * (d ** -0.5 * LOG2E)).astype(BF16)
        s = _dot_nt(qn, kv_ref[:, hh * d:(hh + 1) * d])
        p = jnp.exp2(s - jnp.max(s, axis=-1, keepdims=True))
        v = kv_ref[:, (X_HEADS + hh) * d:(X_HEADS + hh + 1) * d]
        o = _dot(p.astype(BF16), v) / jnp.sum(p, axis=-1, keepdims=True)
        outs.append(o.astype(BF16))
    o_ref[...] = h1 + _dot(jnp.concatenate(outs, axis=-1), wo_ref[...])


def _mix_out(a2, s2, x2, w_out, xnw, wq, kv3, qw, wo, seq, tm=512):
    m, ka = a2.shape
    ks = s2.shape[1]
    dm = x2.shape[1]
    xw = wq.shape[1]
    mlen = kv3.shape[1]
    assert seq % tm == 0 and ks % ka == 0
    once = dict(pipeline_mode=pl.Buffered(1))
    return pl.pallas_call(
        _mix_out_kernel,
        out_shape=jax.ShapeDtypeStruct((m, dm), F32),
        grid=(m // tm,),
        in_specs=[
            pl.BlockSpec((tm, ka), lambda i: (i, 0)),
            pl.BlockSpec((tm, ks), lambda i: (i, 0)),
            pl.BlockSpec((tm, dm), lambda i: (i, 0)),
            pl.BlockSpec((1 + ks // ka, ka, dm), lambda i: (0, 0, 0), **once),
            pl.BlockSpec((1, dm), lambda i: (0, 0)),
            pl.BlockSpec((dm, xw), lambda i: (0, 0), **once),
            pl.BlockSpec((None, mlen, 2 * xw), lambda i: (i // (seq // tm), 0, 0)),
            pl.BlockSpec((1, X_HEAD_DIM), lambda i: (0, 0)),
            pl.BlockSpec((xw, dm), lambda i: (0, 0), **once),
        ],
        out_specs=pl.BlockSpec((tm, dm), lambda i: (i, 0)),
        compiler_params=pltpu.CompilerParams(
            dimension_semantics=("parallel",), vmem_limit_bytes=BIG_VMEM_LIMIT),
        name="mix_out",
    )(a2, s2, x2, w_out.reshape(1 + ks // ka, ka, dm), xnw, wq, kv3, qw, wo)


def _ffn_kernel(h_ref, nw_ref, wg_ref, wu_ref, wd_ref, o_ref, hn_ref):
    f = pl.program_id(1)

    @pl.when(f == 0)
    def _():
        hf = h_ref[...]
        hn_ref[...] = (hf * _rms_scale(hf) * nw_ref[...]).astype(BF16)
        o_ref[...] = hf

    hn = hn_ref[...]
    act = (_silu(_dot(hn, wg_ref[...])) * _dot(hn, wu_ref[...])).astype(BF16)
    o_ref[...] += _dot(act, wd_ref[...])


def _ffn(h2, nw, wg, wu, wd, tm=1024, tf=512):
    m, k = h2.shape
    dff = wg.shape[1]
    return pl.pallas_call(
        _ffn_kernel,
        out_shape=jax.ShapeDtypeStruct((m, k), F32),
        grid=(m // tm, dff // tf),
        in_specs=[
            pl.BlockSpec((tm, k), lambda i, f: (i, 0), pipeline_mode=pl.Buffered(1)),
            pl.BlockSpec((1, k), lambda i, f: (0, 0)),
            pl.BlockSpec((k, tf), lambda i, f: (0, f)),
            pl.BlockSpec((k, tf), lambda i, f: (0, f)),
            pl.BlockSpec((tf, k), lambda i, f: (f, 0)),
        ],
        out_specs=pl.BlockSpec((tm, k), lambda i, f: (i, 0)),
        scratch_shapes=[pltpu.VMEM((tm, k), BF16)],
        compiler_params=pltpu.CompilerParams(
            dimension_semantics=("parallel", "arbitrary"), vmem_limit_bytes=BIG_VMEM_LIMIT),
        name="ffn",
    )(h2, nw, wg, wu, wd)


def _pad_lanes(v, fill=0.0):
    return jnp.pad(v, ((0, 0), (0, LANES - v.shape[1])), constant_values=fill)


def kernel(x, mem, mix_norm_w, w_in, da_q_norm_w, da_k_norm_w, da_lambda_q1, da_lambda_k1, da_lambda_q2, da_lambda_k2, da_subln_w, ssd_conv_w, ssd_conv_b, ssd_dt_bias, ssd_a_log, ssd_d, ssd_norm_w, w_out, xattn_norm_w, mem_norm_w, xattn_w_q, xattn_w_kv, xattn_q_norm_w, xattn_k_norm_w, xattn_w_o, ffn_norm_w, ffn_w_gate, ffn_w_up, ffn_w_down):
    b, s, dm = x.shape
    mlen = mem.shape[1]
    depth = w_in.shape[0]
    da_w = DA_HEADS * DA_V_DIM
    ssd_w = ssd_norm_w.shape[1]
    ssd_heads = ssd_w // SSD_HEAD_DIM
    gn = SSD_GROUPS * SSD_STATE
    xw = X_HEADS * X_HEAD_DIM
    o_q, o_k, o_v, o_z = 0, da_w, 2 * da_w, 3 * da_w
    o_xs = o_z + ssd_w
    o_bc = o_xs + ssd_w
    o_dt = o_bc + 2 * gn
    z_col, xs_col = 0, 1
    bc_col = (2 * ssd_w) // (2 * gn)
    q_col = (2 * ssd_w + 2 * gn) // DA_V_DIM
    k_col = q_col + DA_HEADS
    v_col = k_col + DA_HEADS
    assert (2 * ssd_w) % (2 * gn) == 0 and ssd_heads <= LANES

    assert LANES % ssd_heads == 0 and LANES // ssd_heads >= 3
    expand = np.zeros((LANES, ssd_w), np.float32)
    for r in range(LANES):
        h = r % ssd_heads
        expand[r, h * SSD_HEAD_DIM:(h + 1) * SSD_HEAD_DIM] = 1.0
    expand = jnp.asarray(expand, BF16)

    h = x.reshape(b * s, dm)
    mem2 = mem.reshape(b * mlen, dm)
    for i in range(depth):
        lambda_init = 0.8 - 0.6 * math.exp(-0.3 * i)
        proj, dt_raw = _in_proj(h, mix_norm_w[i][None], jnp.swapaxes(w_in, 1, 2), i, o_dt, o_z)
        proj3 = proj.reshape(b, s, -1)

        lam_p = jnp.stack([da_lambda_q1[i], da_lambda_k1[i], da_lambda_q2[i], da_lambda_k2[i]])
        a_out, _ = _diff_attn(
            proj3, jnp.tile(da_q_norm_w[i], 2)[None], jnp.tile(da_k_norm_w[i], 2)[None],
            lam_p, da_subln_w[i][None], lambda_init, q_col, k_col, v_col, [], i)

        cw, cb = ssd_conv_w[i], ssd_conv_b[i][None]
        s_out, (wo_b, wq_b, xwo_b, wg_b, wu_b, wd_b) = _ssd(
            proj3, dt_raw.reshape(b, s, LANES),
            cw[:, :ssd_w], cb[:, :ssd_w], cw[:, ssd_w:], cb[:, ssd_w:],
            _pad_lanes(ssd_dt_bias[i][None]), _pad_lanes(ssd_a_log[i][None]),
            jnp.repeat(ssd_d[i], SSD_HEAD_DIM)[None], ssd_norm_w[i][None], expand,
            z_col, xs_col, bc_col,
            [w_out, xattn_w_q, xattn_w_o, ffn_w_gate, ffn_w_up, ffn_w_down], i)

        xkv = _xkv_proj(mem2, mem_norm_w[i][None], xattn_w_kv, i, xattn_k_norm_w[i][None])
        h2 = _mix_out(a_out.reshape(b * s, da_w), s_out.reshape(b * s, ssd_w), h, wo_b,
                      xattn_norm_w[i][None], wq_b, xkv.reshape(b, mlen, 2 * xw), xattn_q_norm_w[i][None], xwo_b, s)

        h = _ffn(h2, ffn_norm_w[i][None], wg_b, wu_b, wd_b)
    return h.reshape(b, s, dm)
```

```python
import functools
import math

import jax
import jax.numpy as jnp
import numpy as np
from jax import lax
from jax.experimental import pallas as pl
from jax.experimental.pallas import tpu as pltpu

EPS = 1e-6
LANES = 128
SUBLANES = 8
VMEM_LIMIT = 48 * 1024 * 1024
BIG_VMEM_LIMIT = 56 * 1024 * 1024
LOG2E = math.log2(math.e)
DEN_ROWS = 16
CONV_TAIL = 16

DA_HEADS = 8
DA_HEAD_DIM = 64
DA_V_DIM = 128
SSD_HEAD_DIM = 64
SSD_GROUPS = 4
SSD_STATE = 128
SSD_CONV = 4
SSD_CHUNK = 128
X_HEADS = 4
X_HEAD_DIM = 128

F32 = jnp.float32
BF16 = jnp.bfloat16


def _rms_scale(xf, eps=EPS):
    return lax.rsqrt(jnp.mean(xf * xf, axis=-1, keepdims=True) + eps)


def _silu(x):
    h = 0.5 * x
    return h + h * jnp.tanh(h)


def _split_bf16(x, parts):
    out = []
    r = x
    for _ in range(parts):
        p = r.astype(BF16)
        out.append(p)
        r = r - p.astype(F32)
    return out


def _dot(a, b):
    return jnp.dot(a, b, preferred_element_type=F32)


def _dot_nt(a, b):
    return lax.dot_general(a, b, (((1,), (1,)), ((), ())), preferred_element_type=F32)


def _lane_group_mean(xf, avg):
    hi, lo = _split_bf16(xf, 2)
    return _dot(hi, avg) + _dot(lo, avg)


def _in_proj_kernel(x_ref, nw_ref, wt_ref, wdt_ref, o_ref, dt_ref, hn_ref):
    n, mi = pl.program_id(1), pl.program_id(2)

    @pl.when(n == 0)
    def _():
        xf = x_ref[...]
        hn = (xf * _rms_scale(xf) * nw_ref[...]).astype(BF16)
        hn_ref[mi] = hn
        wdt = wdt_ref[...].astype(BF16)
        wdt = jnp.concatenate([wdt, jnp.zeros((LANES - wdt.shape[0], wdt.shape[1]), BF16)], axis=0)
        dt_ref[...] = _dot_nt(hn, wdt)

    o_ref[...] = _dot_nt(hn_ref[mi], wt_ref[...].astype(BF16)).astype(o_ref.dtype)


def _in_proj(x2, nw, wt_all, layer, n, col_shift, tm=1024, tn=1024, m_inner=2):
    m, k = x2.shape
    n_dt = wt_all.shape[1] - n
    shift, nblk = col_shift // tn, n // tn
    assert col_shift % tn == 0 and n % tn == 0 and n % n_dt == 0 and m % (tm * m_inner) == 0
    row = lambda mo, j, mi: mo * m_inner + jnp.where(j == 0, mi, m_inner - 1)
    return pl.pallas_call(
        _in_proj_kernel,
        out_shape=(jax.ShapeDtypeStruct((m, n), BF16), jax.ShapeDtypeStruct((m, LANES), F32)),
        grid=(m // (tm * m_inner), nblk, m_inner),
        in_specs=[
            pl.BlockSpec((tm, k), lambda mo, j, mi: (row(mo, j, mi), 0)),
            pl.BlockSpec((1, k), lambda mo, j, mi: (0, 0)),
            pl.BlockSpec((None, tn, k), lambda mo, j, mi: (layer, (j + shift) % nblk, 0)),
            pl.BlockSpec((None, n_dt, k), lambda mo, j, mi: (layer, n // n_dt, 0)),
        ],
        out_specs=(
            pl.BlockSpec((tm, tn), lambda mo, j, mi: (mo * m_inner + mi, j)),
            pl.BlockSpec((tm, LANES), lambda mo, j, mi: (row(mo, j, mi), 0)),
        ),
        scratch_shapes=[pltpu.VMEM((m_inner, tm, k), BF16)],
        compiler_params=pltpu.CompilerParams(
            dimension_semantics=("parallel", "arbitrary", "arbitrary"), vmem_limit_bytes=BIG_VMEM_LIMIT),
        name="in_proj",
    )(x2, nw, wt_all, wt_all)


def _xkv_kernel(x_ref, nw_ref, w_ref, kw_ref, o_ref):
    d = X_HEAD_DIM
    xf = x_ref[...]
    hn = (xf * _rms_scale(xf) * nw_ref[...]).astype(BF16)
    kv = _dot(hn, w_ref[...].astype(BF16))
    kw = X_HEADS * d
    for hh in range(X_HEADS):
        kf = kv[:, hh * d:(hh + 1) * d]
        o_ref[:, hh * d:(hh + 1) * d] = (kf * _rms_scale(kf) * kw_ref[...]).astype(o_ref.dtype)
    o_ref[:, kw:] = kv[:, kw:].astype(o_ref.dtype)


def _xkv_proj(x2, nw, w_all, layer, kw, tm=512):
    m, k = x2.shape
    n = w_all.shape[2]
    return pl.pallas_call(
        _xkv_kernel,
        out_shape=jax.ShapeDtypeStruct((m, n), BF16),
        grid=(m // tm,),
        in_specs=[
            pl.BlockSpec((tm, k), lambda i: (i, 0)),
            pl.BlockSpec((1, k), lambda i: (0, 0)),
            pl.BlockSpec((None, k, n), lambda i: (layer, 0, 0)),
            pl.BlockSpec((1, X_HEAD_DIM), lambda i: (0, 0)),
        ],
        out_specs=pl.BlockSpec((tm, n), lambda i: (i, 0)),
        compiler_params=pltpu.CompilerParams(
            dimension_semantics=("parallel",), vmem_limit_bytes=VMEM_LIMIT),
        name="xkv_proj",
    )(x2, nw, w_all, kw)


def _diff_attn_kernel(q_ref, k_ref, v_ref, qw_ref, kw_ref, lam_ref, sw_ref, *rest,
                      tq, hb, lambda_init, n_cast):
    cast_in, o_ref, cast_out = rest[:n_cast], rest[n_cast], rest[n_cast + 1:2 * n_cast + 1]
    kn_ref, vt_ref, qcat_ref, m_ref, acc_ref, s_ref = rest[2 * n_cast + 1:]
    for w_ref, wb_ref in zip(cast_in, cast_out):
        wb_ref[...] = w_ref[...].astype(BF16)
    i = pl.program_id(2)
    n_tiles = vt_ref.shape[1]
    d = DA_V_DIM
    lane = lax.broadcasted_iota(jnp.int32, (1, d), 1)
    first = lane < DA_HEAD_DIM
    ones_row = lax.broadcasted_iota(jnp.int32, (DEN_ROWS, tq), 0) == 0

    half_r = lax.broadcasted_iota(jnp.int32, (d, d), 0) < DA_HEAD_DIM
    half_c = lax.broadcasted_iota(jnp.int32, (d, d), 1) < DA_HEAD_DIM
    sub_avg = jnp.where(half_r == half_c, 1.0 / DA_HEAD_DIM, 0.0).astype(BF16)

    def sub_head_norm(xf):
        return xf * lax.rsqrt(_lane_group_mean(xf * xf, sub_avg) + EPS)

    @pl.when(i == 0)
    def _():
        for hh in range(hb):
            hs = slice(hh * d, (hh + 1) * d)
            kn_ref[hh] = (sub_head_norm(k_ref[:, hs].astype(F32)) * kw_ref[...]).astype(BF16)
            for t in range(n_tiles):
                rows = slice(t * tq, (t + 1) * tq)
                vt_ref[hh, t, :d] = v_ref[rows, hs].T
                vt_ref[hh, t, d:] = jnp.where(ones_row, 1.0, 0.0).astype(BF16)
                qn = sub_head_norm(q_ref[rows, hs].astype(F32)) * qw_ref[...] * (DA_HEAD_DIM ** -0.5 * LOG2E)
                qcat_ref[hh, t] = jnp.concatenate(
                    [jnp.where(first, qn, 0.0).T, jnp.where(first, 0.0, qn).T], axis=1).astype(BF16)
            s_ref[0, hh] = _dot(kn_ref[hh, 0:tq, :], qcat_ref[hh, 0])

    m_ref[...] = jnp.full(m_ref.shape, -jnp.inf, F32)
    acc_ref[...] = jnp.zeros(acc_ref.shape, F32)

    def scores(j, hh):
        start = pl.multiple_of(j * tq, tq)
        return _dot(kn_ref[hh, pl.ds(start, tq), :], qcat_ref[hh, i])

    lam_p = lam_ref[...]
    lam = (jnp.exp(jnp.sum(lam_p[0:1] * lam_p[1:2], axis=-1, keepdims=True))
           - jnp.exp(jnp.sum(lam_p[2:3] * lam_p[3:4], axis=-1, keepdims=True)) + lambda_init)

    def tile(j, cur, last):
        if last:
            key = lax.broadcasted_iota(jnp.int32, (tq, 2 * tq), 0)
            qry = lax.broadcasted_iota(jnp.int32, (tq, 2 * tq), 1)
            causal = key <= jnp.where(qry >= tq, qry - tq, qry)
        for hh in range(hb):
            if not last:
                s_ref[1 - cur, hh] = scores(j + 1, hh)
            s = s_ref[cur, hh]
            if last:
                s = jnp.where(causal, s, -jnp.inf)
                s_ref[0, hh] = _dot(kn_ref[hh, 0:tq, :], qcat_ref[hh, jnp.minimum(i + 1, n_tiles - 1)])
            m_prev = m_ref[hh]
            m_new = jnp.maximum(m_prev, jnp.max(s, axis=0, keepdims=True))
            alpha = jnp.exp2(m_prev - m_new)
            p = jnp.exp2(s - m_new)
            acc = alpha * acc_ref[hh] + _dot(vt_ref[hh, j], p.astype(BF16))
            if not last:
                acc_ref[hh] = acc
                m_ref[hh] = m_new
            else:
                ot = acc[:d] / acc[d:d + 1]
                o = (ot[:, :tq] - lam * ot[:, tq:]).T
                o = o * _rms_scale(o) * sw_ref[...] * (1.0 - lambda_init)
                o_ref[:, hh * d:(hh + 1) * d] = o.astype(o_ref.dtype)

    def body(jj, carry):
        tile(2 * jj, 0, False)
        tile(2 * jj + 1, 1, False)
        return carry

    lax.fori_loop(0, i // 2, body, 0)

    @pl.when(i % 2 == 0)
    def _():
        tile(i, 0, True)

    @pl.when(i % 2 == 1)
    def _():
        tile(i - 1, 0, False)
        tile(i, 1, True)


def _diff_attn(proj3, qw2, kw2, lam_p, sw, lambda_init, q_col, k_col, v_col, cast_ws, layer, tq=256, hb=8):
    b, s, _ = proj3.shape
    d = DA_V_DIM
    nh, nq = DA_HEADS // hb, s // tq
    steps = b * nh * nq
    kern = functools.partial(_diff_attn_kernel, tq=tq, hb=hb, lambda_init=lambda_init, n_cast=len(cast_ws))
    vec = lambda shape: pl.BlockSpec(shape, lambda bi, h, i: (0, 0))
    cast_in, cast_out, cast_shapes = [], [], []
    for w in cast_ws:
        rows, cols = w.shape[1:]
        rb = rows // steps
        assert rows % steps == 0 and rb % (2 * SUBLANES) == 0
        cast_in.append(pl.BlockSpec((None, rb, cols), lambda bi, h, i: (layer, (bi * nh + h) * nq + i, 0)))
        cast_out.append(pl.BlockSpec((rb, cols), lambda bi, h, i: ((bi * nh + h) * nq + i, 0)))
        cast_shapes.append(jax.ShapeDtypeStruct((rows, cols), BF16))
    outs = pl.pallas_call(
        kern,
        out_shape=[jax.ShapeDtypeStruct((b, s, DA_HEADS * d), BF16)] + cast_shapes,
        grid=(b, nh, nq),
        in_specs=[
            pl.BlockSpec((None, s, hb * d), lambda bi, h, i: (bi, 0, q_col // hb + h), pipeline_mode=pl.Buffered(1)),
            pl.BlockSpec((None, s, hb * d), lambda bi, h, i: (bi, 0, k_col // hb + h)),
            pl.BlockSpec((None, s, hb * d), lambda bi, h, i: (bi, 0, v_col // hb + h), pipeline_mode=pl.Buffered(1)),
            vec((1, d)), vec((1, d)), vec((4, DA_HEAD_DIM)), vec((1, d)),
        ] + cast_in,
        out_specs=[pl.BlockSpec((None, tq, hb * d), lambda bi, h, i: (bi, i, h))] + cast_out,
        scratch_shapes=[
            pltpu.VMEM((hb, s, d), BF16),
            pltpu.VMEM((hb, s // tq, d + DEN_ROWS, tq), BF16),
            pltpu.VMEM((hb, s // tq, d, 2 * tq), BF16),
            pltpu.VMEM((hb, 1, 2 * tq), F32),
            pltpu.VMEM((hb, d + DEN_ROWS, 2 * tq), F32),
            pltpu.VMEM((2, hb, tq, 2 * tq), F32),
        ],
        compiler_params=pltpu.CompilerParams(
            dimension_semantics=("parallel", "parallel", "arbitrary"), vmem_limit_bytes=BIG_VMEM_LIMIT),
        name="diff_attn",
    )(proj3, proj3, proj3, qw2, kw2, lam_p, sw, *cast_ws)
    return outs[0], outs[1:]


def _ssd_kernel(z_ref, xs_ref, bc_ref, xsn_ref, bcn_ref, dt_ref, cwx_ref, cbx_ref, cwbc_ref, cbbc_ref,
                dtb_ref, alog_ref, dx_ref, nw_ref, e_ref, *rest, n_cast):
    cast_in, o_ref, cast_out = rest[:n_cast], rest[n_cast], rest[n_cast + 1:2 * n_cast + 1]
    state_ref, cxa_ref, cba_ref, cxb_ref, cbb_ref, y_ref, dtx_ref, acsx_ref = rest[2 * n_cast + 1:]
    L, N, G = SSD_CHUNK, SSD_STATE, SSD_GROUPS
    w = o_ref.shape[-1]
    bcw = bc_ref.shape[-1]
    gw = w // G
    c = pl.program_id(1)

    ext = CONV_TAIL + L
    s_row = lax.broadcasted_iota(jnp.int32, (L, SSD_CONV * ext), 0)
    s_col = lax.broadcasted_iota(jnp.int32, (L, SSD_CONV * ext), 1)
    pick = jnp.zeros((L, SSD_CONV * ext), F32)
    for k in range(SSD_CONV):
        pick = jnp.where(s_col == s_row + (k * ext + CONV_TAIL - k), 0.5, pick)
    pick = pick.astype(BF16)

    def conv_silu(rows, w_ref, b_ref, cs, dst_ref):
        taps = jnp.concatenate(
            [rows * w_ref[SSD_CONV - 1 - k:SSD_CONV - k, cs].astype(BF16) for k in range(SSD_CONV)], axis=0)
        h = _dot(pick, taps) + 0.5 * b_ref[:, cs]
        dst_ref[:, cs] = h + h * jnp.tanh(h)

    def conv_slice(g, rows_of, dst_x_ref, dst_bc_ref):
        xc = slice(g * (w // G), (g + 1) * (w // G))
        bc = slice(g * (bcw // G), (g + 1) * (bcw // G))
        conv_silu(rows_of(xs_ref, xsn_ref, xc), cwx_ref, cbx_ref, xc, dst_x_ref)
        conv_silu(rows_of(bc_ref, bcn_ref, bc), cwbc_ref, cbbc_ref, bc, dst_bc_ref)

    def rows_first(cur_ref, nxt_ref, cs):
        return jnp.concatenate([jnp.zeros((CONV_TAIL, cs.stop - cs.start), BF16), cur_ref[0:L, cs]], axis=0)

    def rows_second(cur_ref, nxt_ref, cs):
        return cur_ref[L - CONV_TAIL:2 * L, cs]

    def rows_next(cur_ref, nxt_ref, cs):
        return jnp.concatenate([cur_ref[2 * L - CONV_TAIL:2 * L, cs], nxt_ref[:, cs]], axis=0)

    @pl.when(c == 0)
    def _():
        state_ref[...] = jnp.zeros(state_ref.shape, F32)
        for g in range(G):
            conv_slice(g, rows_first, cxa_ref, cba_ref)

    r_i = lax.broadcasted_iota(jnp.int32, (L, L), 0)
    c_i = lax.broadcasted_iota(jnp.int32, (L, L), 1)
    causal = c_i <= r_i
    tril = jnp.where(causal, 1.0, 0.0).astype(BF16)
    lane = lax.broadcasted_iota(jnp.int32, (1, LANES), 1)
    first = lane < SSD_HEAD_DIM
    n_heads = w // SSD_HEAD_DIM
    heads_per_group = gw // SSD_HEAD_DIM

    def expand(v, parts):
        packed, r = None, jnp.where(lane < n_heads, v, 0.0)
        for t in range(parts):
            p = r.astype(BF16).astype(F32)
            packed = p if t == 0 else packed + pltpu.roll(p, t * n_heads, 1)
            r = r - p
        return _dot(packed.astype(BF16), e_ref[...])

    def mix_chunk(r0, cx_ref, cb_ref, side):
        rows = slice(r0, r0 + L)
        dtx = dt_ref[rows, :] + dtb_ref[...]
        dt = jnp.maximum(dtx, 0.0) + jnp.log1p(jnp.exp(-jnp.abs(dtx)))
        a = -jnp.exp(alog_ref[...]) * LOG2E
        acs = sum(_dot(tril, p) for p in _split_bf16(dt * a, 3))
        acs_t = acs.T
        dtx_ref[...] = expand(dt, 2)
        acsx_ref[...] = expand(acs, 3)
        for g in range(G):
            side(g)
            cols = slice(g * gw, (g + 1) * gw)
            xs_g = cx_ref[:, cols]
            acs_g = acsx_ref[:, cols]
            last_g = acsx_ref[L - 1:L, cols]
            xdt_g = xs_g * dtx_ref[:, cols]
            xds_g = (xdt_g * jnp.exp2(last_g - acs_g)).astype(BF16)
            b_g = cb_ref[:, g * N:(g + 1) * N]
            c_g = cb_ref[:, (G + g) * N:(G + g + 1) * N].astype(BF16)
            cb = _dot_nt(c_g, b_g.astype(BF16))
            state_old = state_ref[:, cols]
            y_off = _dot(c_g, state_old.astype(BF16)) * jnp.exp2(acs_g)
            state_ref[:, cols] = state_old * jnp.exp2(last_g) + _dot(b_g.T.astype(BF16), xds_g)
            for pair in range(heads_per_group // 2):
                h0 = g * heads_per_group + 2 * pair
                pc = slice(pair * LANES, (pair + 1) * LANES)
                xp = xdt_g[:, pc]
                x_pair = jnp.concatenate([jnp.where(first, xp, 0.0), jnp.where(first, 0.0, xp)], axis=0).astype(BF16)
                mats = []
                for h in (h0, h0 + 1):
                    seg = acs[:, h:h + 1] - acs_t[h:h + 1, :]
                    dec = jnp.exp2(jnp.where(causal, seg, -jnp.inf))
                    mats.append((cb * dec).astype(BF16))
                y_ref[:, pc] = _dot(jnp.concatenate(mats, axis=1), x_pair)
            y = y_ref[...] + y_off + xs_g * dx_ref[:, cols]
            y = y * _silu(z_ref[rows, cols].astype(F32))
            o_ref[rows, cols] = (y * _rms_scale(y) * nw_ref[:, cols]).astype(o_ref.dtype)

    mix_chunk(0, cxa_ref, cba_ref, lambda g: conv_slice(g, rows_second, cxb_ref, cbb_ref))
    for w_ref, wb_ref in zip(cast_in, cast_out):
        wb_ref[...] = w_ref[...].astype(BF16)
    mix_chunk(L, cxb_ref, cbb_ref, lambda g: conv_slice(g, rows_next, cxa_ref, cba_ref))


def _ssd(proj3, dt3, cwx, cbx, cwbc, cbbc, dtb, alog, dx, nw, e, z_col, xs_col, bc_col, cast_ws, layer):
    b, s, _ = proj3.shape
    w = e.shape[1]
    bcw = cwbc.shape[1]
    L = SSD_CHUNK
    nc = s // L
    assert nc % 2 == 0
    steps = b * (nc // 2)
    cast_in, cast_out, cast_shapes = [], [], []
    for cw in cast_ws:
        rows, cols = cw.shape[1:]
        rb = rows // steps
        assert rows % steps == 0 and rb % (2 * SUBLANES) == 0
        cast_in.append(pl.BlockSpec((None, rb, cols), lambda bi, c: (layer, bi * (nc // 2) + c, 0)))
        cast_out.append(pl.BlockSpec((rb, cols), lambda bi, c: (bi * (nc // 2) + c, 0)))
        cast_shapes.append(jax.ShapeDtypeStruct((rows, cols), BF16))
    full = lambda shape: pl.BlockSpec(shape, lambda bi, c: (0, 0))
    nxt = lambda c: jnp.minimum(2 * c + 2, nc - 1)
    outs = pl.pallas_call(
        functools.partial(_ssd_kernel, n_cast=len(cast_ws)),
        out_shape=[jax.ShapeDtypeStruct((b, s, w), BF16)] + cast_shapes,
        grid=(b, nc // 2),
        in_specs=[
            pl.BlockSpec((None, 2 * L, w), lambda bi, c: (bi, c, z_col)),
            pl.BlockSpec((None, 2 * L, w), lambda bi, c: (bi, c, xs_col)),
            pl.BlockSpec((None, 2 * L, bcw), lambda bi, c: (bi, c, bc_col)),
            pl.BlockSpec((None, L, w), lambda bi, c: (bi, nxt(c), xs_col)),
            pl.BlockSpec((None, L, bcw), lambda bi, c: (bi, nxt(c), bc_col)),
            pl.BlockSpec((None, 2 * L, LANES), lambda bi, c: (bi, c, 0)),
            full((SSD_CONV, w)), full((1, w)), full((SSD_CONV, bcw)), full((1, bcw)),
            full((1, LANES)), full((1, LANES)), full((1, w)), full((1, w)), full((LANES, w)),
        ] + cast_in,
        out_specs=[pl.BlockSpec((None, 2 * L, w), lambda bi, c: (bi, c, 0))] + cast_out,
        scratch_shapes=[
            pltpu.VMEM((SSD_STATE, w), F32),
            pltpu.VMEM((L, w), F32), pltpu.VMEM((L, bcw), F32),
            pltpu.VMEM((L, w), F32), pltpu.VMEM((L, bcw), F32),
            pltpu.VMEM((L, w // SSD_GROUPS), F32),
            pltpu.VMEM((L, w), F32), pltpu.VMEM((L, w), F32),
        ],
        compiler_params=pltpu.CompilerParams(
            dimension_semantics=("parallel", "arbitrary"), vmem_limit_bytes=VMEM_LIMIT),
        name="ssd",
    )(proj3, proj3, proj3, proj3, proj3, dt3, cwx, cbx, cwbc, cbbc, dtb, alog, dx, nw, e, *cast_ws)
    return outs[0], outs[1:]


def _mix_out_kernel(a_ref, s_ref, x_ref, w_ref, xnw_ref, wq_ref, kv_ref, qw_ref, wo_ref, o_ref):
    d = X_HEAD_DIM
    ka = a_ref.shape[1]
    h1 = x_ref[...] + _dot(a_ref[...], w_ref[0])
    for c in range(1, w_ref.shape[0]):
        h1 = h1 + _dot(s_ref[:, (c - 1) * ka:c * ka], w_ref[c])
    hn = (h1 * _rms_scale(h1) * xnw_ref[...]).astype(BF16)
    q = _dot(hn, wq_ref[...])
    outs = []
    for hh in range(X_HEADS):
        qf = q[:, hh * d:(hh + 1) * d]
        qn = (qf * _rms_scale(qf) * qw_ref[...] * (d ** -0.5 * LOG2E)).astype(BF16)
        s = _dot_nt(qn, kv_ref[:, hh * d:(hh + 1) * d])
        p = jnp.exp2(s - jnp.max(s, axis=-1, keepdims=True))
        v = kv_ref[:, (X_HEADS + hh) * d:(X_HEADS + hh + 1) * d]
        o = _dot(p.astype(BF16), v) / jnp.sum(p, axis=-1, keepdims=True)
        outs.append(o.astype(BF16))
    o_ref[...] = h1 + _dot(jnp.concatenate(outs, axis=-1), wo_ref[...])


def _mix_out(a2, s2, x2, w_out, xnw, wq, kv3, qw, wo, seq, tm=512):
    m, ka = a2.shape
    ks = s2.shape[1]
    dm = x2.shape[1]
    xw = wq.shape[1]
    mlen = kv3.shape[1]
    assert seq % tm == 0 and ks % ka == 0
    once = dict(pipeline_mode=pl.Buffered(1))
    return pl.pallas_call(
        _mix_out_kernel,
        out_shape=jax.ShapeDtypeStruct((m, dm), F32),
        grid=(m // tm,),
        in_specs=[
            pl.BlockSpec((tm, ka), lambda i: (i, 0)),
            pl.BlockSpec((tm, ks), lambda i: (i, 0)),
            pl.BlockSpec((tm, dm), lambda i: (i, 0)),
            pl.BlockSpec((1 + ks // ka, ka, dm), lambda i: (0, 0, 0), **once),
            pl.BlockSpec((1, dm), lambda i: (0, 0)),
            pl.BlockSpec((dm, xw), lambda i: (0, 0), **once),
            pl.BlockSpec((None, mlen, 2 * xw), lambda i: (i // (seq // tm), 0, 0)),
            pl.BlockSpec((1, X_HEAD_DIM), lambda i: (0, 0)),
            pl.BlockSpec((xw, dm), lambda i: (0, 0), **once),
        ],
        out_specs=pl.BlockSpec((tm, dm), lambda i: (i, 0)),
        compiler_params=pltpu.CompilerParams(
            dimension_semantics=("parallel",), vmem_limit_bytes=BIG_VMEM_LIMIT),
        name="mix_out",
    )(a2, s2, x2, w_out.reshape(1 + ks // ka, ka, dm), xnw, wq, kv3, qw, wo)


def _ffn_kernel(h_ref, nw_ref, wg_ref, wu_ref, wd_ref, o_ref, hn_ref):
    f = pl.program_id(1)

    @pl.when(f == 0)
    def _():
        hf = h_ref[...]
        hn_ref[...] = (hf * _rms_scale(hf) * nw_ref[...]).astype(BF16)
        o_ref[...] = hf

    hn = hn_ref[...]
    act = (_silu(_dot(hn, wg_ref[...])) * _dot(hn, wu_ref[...])).astype(BF16)
    o_ref[...] += _dot(act, wd_ref[...])


def _ffn(h2, nw, wg, wu, wd, tm=1024, tf=512):
    m, k = h2.shape
    dff = wg.shape[1]
    return pl.pallas_call(
        _ffn_kernel,
        out_shape=jax.ShapeDtypeStruct((m, k), F32),
        grid=(m // tm, dff // tf),
        in_specs=[
            pl.BlockSpec((tm, k), lambda i, f: (i, 0), pipeline_mode=pl.Buffered(1)),
            pl.BlockSpec((1, k), lambda i, f: (0, 0)),
            pl.BlockSpec((k, tf), lambda i, f: (0, f)),
            pl.BlockSpec((k, tf), lambda i, f: (0, f)),
            pl.BlockSpec((tf, k), lambda i, f: (f, 0)),
        ],
        out_specs=pl.BlockSpec((tm, k), lambda i, f: (i, 0)),
        scratch_shapes=[pltpu.VMEM((tm, k), BF16)],
        compiler_params=pltpu.CompilerParams(
            dimension_semantics=("parallel", "arbitrary"), vmem_limit_bytes=BIG_VMEM_LIMIT),
        name="ffn",
    )(h2, nw, wg, wu, wd)


def _pad_lanes(v, fill=0.0):
    return jnp.pad(v, ((0, 0), (0, LANES - v.shape[1])), constant_values=fill)


def kernel(x, mem, mix_norm_w, w_in, da_q_norm_w, da_k_norm_w, da_lambda_q1, da_lambda_k1, da_lambda_q2, da_lambda_k2, da_subln_w, ssd_conv_w, ssd_conv_b, ssd_dt_bias, ssd_a_log, ssd_d, ssd_norm_w, w_out, xattn_norm_w, mem_norm_w, xattn_w_q, xattn_w_kv, xattn_q_norm_w, xattn_k_norm_w, xattn_w_o, ffn_norm_w, ffn_w_gate, ffn_w_up, ffn_w_down):
    b, s, dm = x.shape
    mlen = mem.shape[1]
    depth = w_in.shape[0]
    da_w = DA_HEADS * DA_V_DIM
    ssd_w = ssd_norm_w.shape[1]
    ssd_heads = ssd_w // SSD_HEAD_DIM
    gn = SSD_GROUPS * SSD_STATE
    xw = X_HEADS * X_HEAD_DIM
    o_q, o_k, o_v, o_z = 0, da_w, 2 * da_w, 3 * da_w
    o_xs = o_z + ssd_w
    o_bc = o_xs + ssd_w
    o_dt = o_bc + 2 * gn
    z_col, xs_col = 0, 1
    bc_col = (2 * ssd_w) // (2 * gn)
    q_col = (2 * ssd_w + 2 * gn) // DA_V_DIM
    k_col = q_col + DA_HEADS
    v_col = k_col + DA_HEADS
    assert (2 * ssd_w) % (2 * gn) == 0 and ssd_heads <= LANES

    assert LANES % ssd_heads == 0 and LANES // ssd_heads >= 3
    expand = np.zeros((LANES, ssd_w), np.float32)
    for r in range(LANES):
        h = r % ssd_heads
        expand[r, h * SSD_HEAD_DIM:(h + 1) * SSD_HEAD_DIM] = 1.0
    expand = jnp.asarray(expand, BF16)

    h = x.reshape(b * s, dm)
    mem2 = mem.reshape(b * mlen, dm)
    for i in range(depth):
        lambda_init = 0.8 - 0.6 * math.exp(-0.3 * i)
        proj, dt_raw = _in_proj(h, mix_norm_w[i][None], jnp.swapaxes(w_in, 1, 2), i, o_dt, o_z)
        proj3 = proj.reshape(b, s, -1)

        lam_p = jnp.stack([da_lambda_q1[i], da_lambda_k1[i], da_lambda_q2[i], da_lambda_k2[i]])
        a_out, _ = _diff_attn(
            proj3, jnp.tile(da_q_norm_w[i], 2)[None], jnp.tile(da_k_norm_w[i], 2)[None],
            lam_p, da_subln_w[i][None], lambda_init, q_col, k_col, v_col, [], i)

        cw, cb = ssd_conv_w[i], ssd_conv_b[i][None]
        s_out, (wo_b, wq_b, xwo_b, wg_b, wu_b, wd_b) = _ssd(
            proj3, dt_raw.reshape(b, s, LANES),
            cw[:, :ssd_w], cb[:, :ssd_w], cw[:, ssd_w:], cb[:, ssd_w:],
            _pad_lanes(ssd_dt_bias[i][None]), _pad_lanes(ssd_a_log[i][None]),
            jnp.repeat(ssd_d[i], SSD_HEAD_DIM)[None], ssd_norm_w[i][None], expand,
            z_col, xs_col, bc_col,
            [w_out, xattn_w_q, xattn_w_o, ffn_w_gate, ffn_w_up, ffn_w_down], i)

        xkv = _xkv_proj(mem2, mem_norm_w[i][None], xattn_w_kv, i, xattn_k_norm_w[i][None])
        h2 = _mix_out(a_out.reshape(b * s, da_w), s_out.reshape(b * s, ssd_w), h, wo_b,
                      xattn_norm_w[i][None], wq_b, xkv.reshape(b, mlen, 2 * xw), xattn_q_norm_w[i][None], xwo_b, s)

        h = _ffn(h2, ffn_norm_w[i][None], wg_b, wu_b, wd_b)
    return h.reshape(b, s, dm)
```

```python
import functools
import math

import jax
import jax.numpy as jnp
import numpy as np
from jax import lax
from jax.experimental import pallas as pl
from jax.experimental.pallas import tpu as pltpu

EPS = 1e-6
LANES = 128
SUBLANES = 8
VMEM_LIMIT = 48 * 1024 * 1024
BIG_VMEM_LIMIT = 56 * 1024 * 1024
LOG2E = math.log2(math.e)
DEN_ROWS = 16
CONV_TAIL = 16

DA_HEADS = 8
DA_HEAD_DIM = 64
DA_V_DIM = 128
SSD_HEAD_DIM = 64
SSD_GROUPS = 4
SSD_STATE = 128
SSD_CONV = 4
SSD_CHUNK = 128
X_HEADS = 4
X_HEAD_DIM = 128

F32 = jnp.float32
BF16 = jnp.bfloat16


def _rms_scale(xf, eps=EPS):
    return lax.rsqrt(jnp.mean(xf * xf, axis=-1, keepdims=True) + eps)


def _silu(x):
    h = 0.5 * x
    return h + h * jnp.tanh(h)


def _split_bf16(x, parts):
    out = []
    r = x
    for _ in range(parts):
        p = r.astype(BF16)
        out.append(p)
        r = r - p.astype(F32)
    return out


def _dot(a, b):
    return jnp.dot(a, b, preferred_element_type=F32)


def _dot_nt(a, b):
    return lax.dot_general(a, b, (((1,), (1,)), ((), ())), preferred_element_type=F32)


def _lane_group_mean(xf, avg):
    hi, lo = _split_bf16(xf, 2)
    return _dot(hi, avg) + _dot(lo, avg)


def _in_proj_kernel(x_ref, nw_ref, wt_ref, wdt_ref, o_ref, dt_ref, hn_ref):
    n, mi = pl.program_id(1), pl.program_id(2)

    @pl.when(n == 0)
    def _():
        xf = x_ref[...]
        hn = (xf * _rms_scale(xf) * nw_ref[...]).astype(BF16)
        hn_ref[mi] = hn
        wdt = wdt_ref[...].astype(BF16)
        wdt = jnp.concatenate([wdt, jnp.zeros((LANES - wdt.shape[0], wdt.shape[1]), BF16)], axis=0)
        dt_ref[...] = _dot_nt(hn, wdt)

    o_ref[...] = _dot_nt(hn_ref[mi], wt_ref[...].astype(BF16)).astype(o_ref.dtype)


def _in_proj(x2, nw, wt_all, layer, n, col_shift, tm=1024, tn=1024, m_inner=2):
    m, k = x2.shape
    n_dt = wt_all.shape[1] - n
    shift, nblk = col_shift // tn, n // tn
    assert col_shift % tn == 0 and n % tn == 0 and n % n_dt == 0 and m % (tm * m_inner) == 0
    row = lambda mo, j, mi: mo * m_inner + jnp.where(j == 0, mi, m_inner - 1)
    return pl.pallas_call(
        _in_proj_kernel,
        out_shape=(jax.ShapeDtypeStruct((m, n), BF16), jax.ShapeDtypeStruct((m, LANES), F32)),
        grid=(m // (tm * m_inner), nblk, m_inner),
        in_specs=[
            pl.BlockSpec((tm, k), lambda mo, j, mi: (row(mo, j, mi), 0)),
            pl.BlockSpec((1, k), lambda mo, j, mi: (0, 0)),
            pl.BlockSpec((None, tn, k), lambda mo, j, mi: (layer, (j + shift) % nblk, 0)),
            pl.BlockSpec((None, n_dt, k), lambda mo, j, mi: (layer, n // n_dt, 0)),
        ],
        out_specs=(
            pl.BlockSpec((tm, tn), lambda mo, j, mi: (mo * m_inner + mi, j)),
            pl.BlockSpec((tm, LANES), lambda mo, j, mi: (row(mo, j, mi), 0)),
        ),
        scratch_shapes=[pltpu.VMEM((m_inner, tm, k), BF16)],
        compiler_params=pltpu.CompilerParams(
            dimension_semantics=("parallel", "arbitrary", "arbitrary"), vmem_limit_bytes=BIG_VMEM_LIMIT),
        name="in_proj",
    )(x2, nw, wt_all, wt_all)


def _xkv_kernel(x_ref, nw_ref, w_ref, kw_ref, o_ref):
    d = X_HEAD_DIM
    xf = x_ref[...]
    hn = (xf * _rms_scale(xf) * nw_ref[...]).astype(BF16)
    kv = _dot(hn, w_ref[...].astype(BF16))
    kw = X_HEADS * d
    for hh in range(X_HEADS):
        kf = kv[:, hh * d:(hh + 1) * d]
        o_ref[:, hh * d:(hh + 1) * d] = (kf * _rms_scale(kf) * kw_ref[...]).astype(o_ref.dtype)
    o_ref[:, kw:] = kv[:, kw:].astype(o_ref.dtype)


def _xkv_proj(x2, nw, w_all, layer, kw, tm=512):
    m, k = x2.shape
    n = w_all.shape[2]
    return pl.pallas_call(
        _xkv_kernel,
        out_shape=jax.ShapeDtypeStruct((m, n), BF16),
        grid=(m // tm,),
        in_specs=[
            pl.BlockSpec((tm, k), lambda i: (i, 0)),
            pl.BlockSpec((1, k), lambda i: (0, 0)),
            pl.BlockSpec((None, k, n), lambda i: (layer, 0, 0)),
            pl.BlockSpec((1, X_HEAD_DIM), lambda i: (0, 0)),
        ],
        out_specs=pl.BlockSpec((tm, n), lambda i: (i, 0)),
        compiler_params=pltpu.CompilerParams(
            dimension_semantics=("parallel",), vmem_limit_bytes=VMEM_LIMIT),
        name="xkv_proj",
    )(x2, nw, w_all, kw)


def _diff_attn_kernel(q_ref, k_ref, v_ref, qw_ref, kw_ref, lam_ref, sw_ref, *rest,
                      tq, hb, lambda_init, n_cast):
    cast_in, o_ref, cast_out = rest[:n_cast], rest[n_cast], rest[n_cast + 1:2 * n_cast + 1]
    kn_ref, vt_ref, qcat_ref, m_ref, acc_ref, s_ref = rest[2 * n_cast + 1:]
    for w_ref, wb_ref in zip(cast_in, cast_out):
        wb_ref[...] = w_ref[...].astype(BF16)
    i = pl.program_id(2)
    n_tiles = vt_ref.shape[1]
    d = DA_V_DIM
    lane = lax.broadcasted_iota(jnp.int32, (1, d), 1)
    first = lane < DA_HEAD_DIM
    ones_row = lax.broadcasted_iota(jnp.int32, (DEN_ROWS, tq), 0) == 0

    half_r = lax.broadcasted_iota(jnp.int32, (d, d), 0) < DA_HEAD_DIM
    half_c = lax.broadcasted_iota(jnp.int32, (d, d), 1) < DA_HEAD_DIM
    sub_avg = jnp.where(half_r == half_c, 1.0 / DA_HEAD_DIM, 0.0).astype(BF16)

    def sub_head_norm(xf):
        return xf * lax.rsqrt(_lane_group_mean(xf * xf, sub_avg) + EPS)

    @pl.when(i == 0)
    def _():
        for hh in range(hb):
            hs = slice(hh * d, (hh + 1) * d)
            kn_ref[hh] = (sub_head_norm(k_ref[:, hs].astype(F32)) * kw_ref[...]).astype(BF16)
            for t in range(n_tiles):
                vt_ref[hh, t, :d] = v_ref[t * tq:(t + 1) * tq, hs].T
                vt_ref[hh, t, d:] = jnp.where(ones_row, 1.0, 0.0).astype(BF16)

    for hh in range(hb):
        qn = sub_head_norm(q_ref[:, hh * d:(hh + 1) * d].astype(F32)) * qw_ref[...] * (DA_HEAD_DIM ** -0.5 * LOG2E)
        qcat_ref[hh] = jnp.concatenate(
            [jnp.where(first, qn, 0.0).T, jnp.where(first, 0.0, qn).T], axis=1).astype(BF16)
    m_ref[...] = jnp.full(m_ref.shape, -jnp.inf, F32)
    acc_ref[...] = jnp.zeros(acc_ref.shape, F32)

    def scores(j, hh):
        start = pl.multiple_of(j * tq, tq)
        return _dot(kn_ref[hh, pl.ds(start, tq), :], qcat_ref[hh])

    lam_p = lam_ref[...]
    lam = (jnp.exp(jnp.sum(lam_p[0:1] * lam_p[1:2], axis=-1, keepdims=True))
           - jnp.exp(jnp.sum(lam_p[2:3] * lam_p[3:4], axis=-1, keepdims=True)) + lambda_init)

    def tile(j, cur, last):
        if last:
            key = lax.broadcasted_iota(jnp.int32, (tq, 2 * tq), 0)
            qry = lax.broadcasted_iota(jnp.int32, (tq, 2 * tq), 1)
            causal = key <= jnp.where(qry >= tq, qry - tq, qry)
        for hh in range(hb):
            if not last:
                s_ref[1 - cur, hh] = scores(j + 1, hh)
            s = s_ref[cur, hh]
            if last:
                s = jnp.where(causal, s, -jnp.inf)
            m_prev = m_ref[hh]
            m_new = jnp.maximum(m_prev, jnp.max(s, axis=0, keepdims=True))
            alpha = jnp.exp2(m_prev - m_new)
            p = jnp.exp2(s - m_new)
            acc = alpha * acc_ref[hh] + _dot(vt_ref[hh, j], p.astype(BF16))
            if not last:
                acc_ref[hh] = acc
                m_ref[hh] = m_new
            else:
                ot = acc[:d] / acc[d:d + 1]
                o = (ot[:, :tq] - lam * ot[:, tq:]).T
                o = o * _rms_scale(o) * sw_ref[...] * (1.0 - lambda_init)
                o_ref[:, hh * d:(hh + 1) * d] = o.astype(o_ref.dtype)

    for hh in range(hb):
        s_ref[0, hh] = scores(0, hh)

    unroll = 4

    def body(jj, carry):
        for u in range(unroll):
            tile(unroll * jj + u, u % 2, False)
        return carry

    lax.fori_loop(0, i // unroll, body, 0)

    for rem in range(unroll):
        @pl.when(i % unroll == rem)
        def _(rem=rem):
            for u in range(rem):
                tile(i - rem + u, u % 2, False)
            tile(i, rem % 2, True)


def _diff_attn(proj3, qw2, kw2, lam_p, sw, lambda_init, q_col, k_col, v_col, cast_ws, layer, tq=256, hb=8):
    b, s, _ = proj3.shape
    d = DA_V_DIM
    nh, nq = DA_HEADS // hb, s // tq
    steps = b * nh * nq
    kern = functools.partial(_diff_attn_kernel, tq=tq, hb=hb, lambda_init=lambda_init, n_cast=len(cast_ws))
    vec = lambda shape: pl.BlockSpec(shape, lambda bi, h, i: (0, 0))
    cast_in, cast_out, cast_shapes = [], [], []
    for w in cast_ws:
        rows, cols = w.shape[1:]
        rb = rows // steps
        assert rows % steps == 0 and rb % (2 * SUBLANES) == 0
        cast_in.append(pl.BlockSpec((None, rb, cols), lambda bi, h, i: (layer, (bi * nh + h) * nq + i, 0)))
        cast_out.append(pl.BlockSpec((rb, cols), lambda bi, h, i: ((bi * nh + h) * nq + i, 0)))
        cast_shapes.append(jax.ShapeDtypeStruct((rows, cols), BF16))
    outs = pl.pallas_call(
        kern,
        out_shape=[jax.ShapeDtypeStruct((b, s, DA_HEADS * d), BF16)] + cast_shapes,
        grid=(b, nh, nq),
        in_specs=[
            pl.BlockSpec((None, tq, hb * d), lambda bi, h, i: (bi, i, q_col // hb + h)),
            pl.BlockSpec((None, s, hb * d), lambda bi, h, i: (bi, 0, k_col // hb + h)),
            pl.BlockSpec((None, s, hb * d), lambda bi, h, i: (bi, 0, v_col // hb + h)),
            vec((1, d)), vec((1, d)), vec((4, DA_HEAD_DIM)), vec((1, d)),
        ] + cast_in,
        out_specs=[pl.BlockSpec((None, tq, hb * d), lambda bi, h, i: (bi, i, h))] + cast_out,
        scratch_shapes=[
            pltpu.VMEM((hb, s, d), BF16),
            pltpu.VMEM((hb, s // tq, d + DEN_ROWS, tq), BF16),
            pltpu.VMEM((hb, d, 2 * tq), BF16),
            pltpu.VMEM((hb, 1, 2 * tq), F32),
            pltpu.VMEM((hb, d + DEN_ROWS, 2 * tq), F32),
            pltpu.VMEM((2, hb, tq, 2 * tq), F32),
        ],
        compiler_params=pltpu.CompilerParams(
            dimension_semantics=("parallel", "parallel", "arbitrary"), vmem_limit_bytes=BIG_VMEM_LIMIT),
        name="diff_attn",
    )(proj3, proj3, proj3, qw2, kw2, lam_p, sw, *cast_ws)
    return outs[0], outs[1:]


def _ssd_kernel(z_ref, xs_ref, bc_ref, xsn_ref, bcn_ref, dt_ref, cwx_ref, cbx_ref, cwbc_ref, cbbc_ref,
                dtb_ref, alog_ref, dx_ref, nw_ref, e_ref, *rest, n_cast):
    cast_in, o_ref, cast_out = rest[:n_cast], rest[n_cast], rest[n_cast + 1:2 * n_cast + 1]
    state_ref, cxa_ref, cba_ref, cxb_ref, cbb_ref, y_ref, dtx_ref, acsx_ref = rest[2 * n_cast + 1:]
    L, N, G = SSD_CHUNK, SSD_STATE, SSD_GROUPS
    w = o_ref.shape[-1]
    bcw = bc_ref.shape[-1]
    gw = w // G
    c = pl.program_id(1)

    ext = CONV_TAIL + L
    s_row = lax.broadcasted_iota(jnp.int32, (L, SSD_CONV * ext), 0)
    s_col = lax.broadcasted_iota(jnp.int32, (L, SSD_CONV * ext), 1)
    pick = jnp.zeros((L, SSD_CONV * ext), F32)
    for k in range(SSD_CONV):
        pick = jnp.where(s_col == s_row + (k * ext + CONV_TAIL - k), 0.5, pick)
    pick = pick.astype(BF16)

    def conv_silu(rows, w_ref, b_ref, cs, dst_ref):
        n_rt = rows.shape[0] // CONV_TAIL
        rows3 = rows.reshape(n_rt, CONV_TAIL, rows.shape[1])
        taps = jnp.concatenate(
            [(rows3 * w_ref[SSD_CONV - 1 - k, :, cs][None]).reshape(rows.shape) for k in range(SSD_CONV)], axis=0)
        h = _dot(pick, taps) + 0.5 * b_ref[:, cs]
        dst_ref[:, cs] = h + h * jnp.tanh(h)

    def conv_slice(g, rows_of, dst_x_ref, dst_bc_ref):
        xc = slice(g * (w // G), (g + 1) * (w // G))
        bc = slice(g * (bcw // G), (g + 1) * (bcw // G))
        conv_silu(rows_of(xs_ref, xsn_ref, xc), cwx_ref, cbx_ref, xc, dst_x_ref)
        conv_silu(rows_of(bc_ref, bcn_ref, bc), cwbc_ref, cbbc_ref, bc, dst_bc_ref)

    def rows_first(cur_ref, nxt_ref, cs):
        return jnp.concatenate([jnp.zeros((CONV_TAIL, cs.stop - cs.start), BF16), cur_ref[0:L, cs]], axis=0)

    def rows_second(cur_ref, nxt_ref, cs):
        return cur_ref[L - CONV_TAIL:2 * L, cs]

    def rows_next(cur_ref, nxt_ref, cs):
        return jnp.concatenate([cur_ref[2 * L - CONV_TAIL:2 * L, cs], nxt_ref[:, cs]], axis=0)

    @pl.when(c == 0)
    def _():
        state_ref[...] = jnp.zeros(state_ref.shape, F32)
        for g in range(G):
            conv_slice(g, rows_first, cxa_ref, cba_ref)

    r_i = lax.broadcasted_iota(jnp.int32, (L, L), 0)
    c_i = lax.broadcasted_iota(jnp.int32, (L, L), 1)
    causal = c_i <= r_i
    tril = jnp.where(causal, 1.0, 0.0).astype(BF16)
    lane = lax.broadcasted_iota(jnp.int32, (1, LANES), 1)
    first = lane < SSD_HEAD_DIM
    n_heads = w // SSD_HEAD_DIM
    heads_per_group = gw // SSD_HEAD_DIM

    def expand(v, parts):
        packed, r = None, jnp.where(lane < n_heads, v, 0.0)
        for t in range(parts):
            p = r.astype(BF16).astype(F32)
            packed = p if t == 0 else packed + pltpu.roll(p, t * n_heads, 1)
            r = r - p
        return _dot(packed.astype(BF16), e_ref[...])

    def mix_chunk(r0, cx_ref, cb_ref, side):
        rows = slice(r0, r0 + L)
        dtx = dt_ref[rows, :] + dtb_ref[...]
        dt = jnp.maximum(dtx, 0.0) + jnp.log1p(jnp.exp(-jnp.abs(dtx)))
        a = -jnp.exp(alog_ref[...]) * LOG2E
        acs = sum(_dot(tril, p) for p in _split_bf16(dt * a, 3))
        acs_t = acs.T
        dtx_ref[...] = expand(dt, 2)
        acsx_ref[...] = expand(acs, 3)
        for g in range(G):
            side(g)
            cols = slice(g * gw, (g + 1) * gw)
            xs_g = cx_ref[:, cols]
            acs_g = acsx_ref[:, cols]
            last_g = acsx_ref[L - 1:L, cols]
            xdt_g = xs_g * dtx_ref[:, cols]
            xds_g = (xdt_g * jnp.exp2(last_g - acs_g)).astype(BF16)
            b_g = cb_ref[:, g * N:(g + 1) * N]
            c_g = cb_ref[:, (G + g) * N:(G + g + 1) * N].astype(BF16)
            cb = _dot_nt(c_g, b_g.astype(BF16))
            state_old = state_ref[:, cols]
            y_off = _dot(c_g, state_old.astype(BF16)) * jnp.exp2(acs_g)
            state_ref[:, cols] = state_old * jnp.exp2(last_g) + _dot(b_g.T.astype(BF16), xds_g)
            for pair in range(heads_per_group // 2):
                h0 = g * heads_per_group + 2 * pair
                pc = slice(pair * LANES, (pair + 1) * LANES)
                xp = xdt_g[:, pc]
                x_pair = jnp.concatenate([jnp.where(first, xp, 0.0), jnp.where(first, 0.0, xp)], axis=0).astype(BF16)
                mats = []
                for h in (h0, h0 + 1):
                    seg = acs[:, h:h + 1] - acs_t[h:h + 1, :]
                    dec = jnp.exp2(jnp.where(causal, seg, -jnp.inf))
                    mats.append((cb * dec).astype(BF16))
                y_ref[:, pc] = _dot(jnp.concatenate(mats, axis=1), x_pair)
            y = y_ref[...] + y_off + xs_g * dx_ref[:, cols]
            y = y * _silu(z_ref[rows, cols].astype(F32))
            o_ref[rows, cols] = (y * _rms_scale(y) * nw_ref[:, cols]).astype(o_ref.dtype)

    mix_chunk(0, cxa_ref, cba_ref, lambda g: conv_slice(g, rows_second, cxb_ref, cbb_ref))
    for w_ref, wb_ref in zip(cast_in, cast_out):
        wb_ref[...] = w_ref[...].astype(BF16)
    mix_chunk(L, cxb_ref, cbb_ref, lambda g: conv_slice(g, rows_next, cxa_ref, cba_ref))


def _ssd(proj3, dt3, cwx, cbx, cwbc, cbbc, dtb, alog, dx, nw, e, z_col, xs_col, bc_col, cast_ws, layer):
    b, s, _ = proj3.shape
    w = e.shape[1]
    bcw = cwbc.shape[2]
    L = SSD_CHUNK
    nc = s // L
    assert nc % 2 == 0
    steps = b * (nc // 2)
    cast_in, cast_out, cast_shapes = [], [], []
    for cw in cast_ws:
        rows, cols = cw.shape[1:]
        rb = rows // steps
        assert rows % steps == 0 and rb % (2 * SUBLANES) == 0
        cast_in.append(pl.BlockSpec((None, rb, cols), lambda bi, c: (layer, bi * (nc // 2) + c, 0)))
        cast_out.append(pl.BlockSpec((rb, cols), lambda bi, c: (bi * (nc // 2) + c, 0)))
        cast_shapes.append(jax.ShapeDtypeStruct((rows, cols), BF16))
    full = lambda shape: pl.BlockSpec(shape, lambda bi, c: (0,) * len(shape))
    nxt = lambda c: jnp.minimum(2 * c + 2, nc - 1)
    outs = pl.pallas_call(
        functools.partial(_ssd_kernel, n_cast=len(cast_ws)),
        out_shape=[jax.ShapeDtypeStruct((b, s, w), BF16)] + cast_shapes,
        grid=(b, nc // 2),
        in_specs=[
            pl.BlockSpec((None, 2 * L, w), lambda bi, c: (bi, c, z_col)),
            pl.BlockSpec((None, 2 * L, w), lambda bi, c: (bi, c, xs_col)),
            pl.BlockSpec((None, 2 * L, bcw), lambda bi, c: (bi, c, bc_col)),
            pl.BlockSpec((None, L, w), lambda bi, c: (bi, nxt(c), xs_col)),
            pl.BlockSpec((None, L, bcw), lambda bi, c: (bi, nxt(c), bc_col)),
            pl.BlockSpec((None, 2 * L, LANES), lambda bi, c: (bi, c, 0)),
            full((SSD_CONV, CONV_TAIL, w)), full((1, w)), full((SSD_CONV, CONV_TAIL, bcw)), full((1, bcw)),
            full((1, LANES)), full((1, LANES)), full((1, w)), full((1, w)), full((LANES, w)),
        ] + cast_in,
        out_specs=[pl.BlockSpec((None, 2 * L, w), lambda bi, c: (bi, c, 0))] + cast_out,
        scratch_shapes=[
            pltpu.VMEM((SSD_STATE, w), F32),
            pltpu.VMEM((L, w), F32), pltpu.VMEM((L, bcw), F32),
            pltpu.VMEM((L, w), F32), pltpu.VMEM((L, bcw), F32),
            pltpu.VMEM((L, w // SSD_GROUPS), F32),
            pltpu.VMEM((L, w), F32), pltpu.VMEM((L, w), F32),
        ],
        compiler_params=pltpu.CompilerParams(
            dimension_semantics=("parallel", "arbitrary"), vmem_limit_bytes=VMEM_LIMIT),
        name="ssd",
    )(proj3, proj3, proj3, proj3, proj3, dt3, cwx, cbx, cwbc, cbbc, dtb, alog, dx, nw, e, *cast_ws)
    return outs[0], outs[1:]


def _mix_out_kernel(a_ref, s_ref, x_ref, w_ref, xnw_ref, wq_ref, kv_ref, qw_ref, wo_ref, o_ref):
    d = X_HEAD_DIM
    ka = a_ref.shape[1]
    h1 = x_ref[...] + _dot(a_ref[...], w_ref[0])
    for c in range(1, w_ref.shape[0]):
        h1 = h1 + _dot(s_ref[:, (c - 1) * ka:c * ka], w_ref[c])
    hn = (h1 * _rms_scale(h1) * xnw_ref[...]).astype(BF16)
    q = _dot(hn, wq_ref[...])
    outs = []
    for hh in range(X_HEADS):
        qf = q[:, hh * d:(hh + 1) * d]
        qn = (qf * _rms_scale(qf) * qw_ref[...] * (d ** -0.5 * LOG2E)).astype(BF16)
        s = _dot_nt(qn, kv_ref[:, hh * d:(hh + 1) * d])
        p = jnp.exp2(s - jnp.max(s, axis=-1, keepdims=True))
        v = kv_ref[:, (X_HEADS + hh) * d:(X_HEADS + hh + 1) * d]
        o = _dot(p.astype(BF16), v) / jnp.sum(p, axis=-1, keepdims=True)
        outs.append(o.astype(BF16))
    o_ref[...] = h1 + _dot(jnp.concatenate(outs, axis=-1), wo_ref[...])


def _mix_out(a2, s2, x2, w_out, xnw, wq, kv3, qw, wo, seq, tm=512):
    m, ka = a2.shape
    ks = s2.shape[1]
    dm = x2.shape[1]
    xw = wq.shape[1]
    mlen = kv3.shape[1]
    assert seq % tm == 0 and ks % ka == 0
    once = dict(pipeline_mode=pl.Buffered(1))
    return pl.pallas_call(
        _mix_out_kernel,
        out_shape=jax.ShapeDtypeStruct((m, dm), F32),
        grid=(m // tm,),
        in_specs=[
            pl.BlockSpec((tm, ka), lambda i: (i, 0)),
            pl.BlockSpec((tm, ks), lambda i: (i, 0)),
            pl.BlockSpec((tm, dm), lambda i: (i, 0)),
            pl.BlockSpec((1 + ks // ka, ka, dm), lambda i: (0, 0, 0), **once),
            pl.BlockSpec((1, dm), lambda i: (0, 0)),
            pl.BlockSpec((dm, xw), lambda i: (0, 0), **once),
            pl.BlockSpec((None, mlen, 2 * xw), lambda i: (i // (seq // tm), 0, 0)),
            pl.BlockSpec((1, X_HEAD_DIM), lambda i: (0, 0)),
            pl.BlockSpec((xw, dm), lambda i: (0, 0), **once),
        ],
        out_specs=pl.BlockSpec((tm, dm), lambda i: (i, 0)),
        compiler_params=pltpu.CompilerParams(
            dimension_semantics=("parallel",), vmem_limit_bytes=BIG_VMEM_LIMIT),
        name="mix_out",
    )(a2, s2, x2, w_out.reshape(1 + ks // ka, ka, dm), xnw, wq, kv3, qw, wo)


def _ffn_kernel(h_ref, nw_ref, wg_ref, wu_ref, wd_ref, o_ref, hn_ref):
    f = pl.program_id(1)

    @pl.when(f == 0)
    def _():
        hf = h_ref[...]
        hn_ref[...] = (hf * _rms_scale(hf) * nw_ref[...]).astype(BF16)
        o_ref[...] = hf

    hn = hn_ref[...]
    act = (_silu(_dot(hn, wg_ref[...])) * _dot(hn, wu_ref[...])).astype(BF16)
    o_ref[...] += _dot(act, wd_ref[...])


def _ffn(h2, nw, wg, wu, wd, tm=1024, tf=512):
    m, k = h2.shape
    dff = wg.shape[1]
    return pl.pallas_call(
        _ffn_kernel,
        out_shape=jax.ShapeDtypeStruct((m, k), F32),
        grid=(m // tm, dff // tf),
        in_specs=[
            pl.BlockSpec((tm, k), lambda i, f: (i, 0), pipeline_mode=pl.Buffered(1)),
            pl.BlockSpec((1, k), lambda i, f: (0, 0)),
            pl.BlockSpec((k, tf), lambda i, f: (0, f)),
            pl.BlockSpec((k, tf), lambda i, f: (0, f)),
            pl.BlockSpec((tf, k), lambda i, f: (f, 0)),
        ],
        out_specs=pl.BlockSpec((tm, k), lambda i, f: (i, 0)),
        scratch_shapes=[pltpu.VMEM((tm, k), BF16)],
        compiler_params=pltpu.CompilerParams(
            dimension_semantics=("parallel", "arbitrary"), vmem_limit_bytes=BIG_VMEM_LIMIT),
        name="ffn",
    )(h2, nw, wg, wu, wd)


def _pad_lanes(v, fill=0.0):
    return jnp.pad(v, ((0, 0), (0, LANES - v.shape[1])), constant_values=fill)


def kernel(x, mem, mix_norm_w, w_in, da_q_norm_w, da_k_norm_w, da_lambda_q1, da_lambda_k1, da_lambda_q2, da_lambda_k2, da_subln_w, ssd_conv_w, ssd_conv_b, ssd_dt_bias, ssd_a_log, ssd_d, ssd_norm_w, w_out, xattn_norm_w, mem_norm_w, xattn_w_q, xattn_w_kv, xattn_q_norm_w, xattn_k_norm_w, xattn_w_o, ffn_norm_w, ffn_w_gate, ffn_w_up, ffn_w_down):
    b, s, dm = x.shape
    mlen = mem.shape[1]
    depth = w_in.shape[0]
    da_w = DA_HEADS * DA_V_DIM
    ssd_w = ssd_norm_w.shape[1]
    ssd_heads = ssd_w // SSD_HEAD_DIM
    gn = SSD_GROUPS * SSD_STATE
    xw = X_HEADS * X_HEAD_DIM
    o_q, o_k, o_v, o_z = 0, da_w, 2 * da_w, 3 * da_w
    o_xs = o_z + ssd_w
    o_bc = o_xs + ssd_w
    o_dt = o_bc + 2 * gn
    z_col, xs_col = 0, 1
    bc_col = (2 * ssd_w) // (2 * gn)
    q_col = (2 * ssd_w + 2 * gn) // DA_V_DIM
    k_col = q_col + DA_HEADS
    v_col = k_col + DA_HEADS
    assert (2 * ssd_w) % (2 * gn) == 0 and ssd_heads <= LANES

    assert LANES % ssd_heads == 0 and LANES // ssd_heads >= 3
    expand = np.zeros((LANES, ssd_w), np.float32)
    for r in range(LANES):
        h = r % ssd_heads
        expand[r, h * SSD_HEAD_DIM:(h + 1) * SSD_HEAD_DIM] = 1.0
    expand = jnp.asarray(expand, BF16)

    h = x.reshape(b * s, dm)
    mem2 = mem.reshape(b * mlen, dm)
    for i in range(depth):
        lambda_init = 0.8 - 0.6 * math.exp(-0.3 * i)
        proj, dt_raw = _in_proj(h, mix_norm_w[i][None], jnp.swapaxes(w_in, 1, 2), i, o_dt, o_z)
        proj3 = proj.reshape(b, s, -1)

        lam_p = jnp.stack([da_lambda_q1[i], da_lambda_k1[i], da_lambda_q2[i], da_lambda_k2[i]])
        a_out, _ = _diff_attn(
            proj3, jnp.tile(da_q_norm_w[i], 2)[None], jnp.tile(da_k_norm_w[i], 2)[None],
            lam_p, da_subln_w[i][None], lambda_init, q_col, k_col, v_col, [], i)

        cw = jnp.broadcast_to(ssd_conv_w[i][:, None, :], (SSD_CONV, CONV_TAIL, ssd_conv_w.shape[2])).astype(BF16)
        cb = ssd_conv_b[i][None]
        s_out, (wo_b, wq_b, xwo_b, wg_b, wu_b, wd_b) = _ssd(
            proj3, dt_raw.reshape(b, s, LANES),
            cw[:, :, :ssd_w], cb[:, :ssd_w], cw[:, :, ssd_w:], cb[:, ssd_w:],
            _pad_lanes(ssd_dt_bias[i][None]), _pad_lanes(ssd_a_log[i][None]),
            jnp.repeat(ssd_d[i], SSD_HEAD_DIM)[None], ssd_norm_w[i][None], expand,
            z_col, xs_col, bc_col,
            [w_out, xattn_w_q, xattn_w_o, ffn_w_gate, ffn_w_up, ffn_w_down], i)

        xkv = _xkv_proj(mem2, mem_norm_w[i][None], xattn_w_kv, i, xattn_k_norm_w[i][None])
        h2 = _mix_out(a_out.reshape(b * s, da_w), s_out.reshape(b * s, ssd_w), h, wo_b,
                      xattn_norm_w[i][None], wq_b, xkv.reshape(b, mlen, 2 * xw), xattn_q_norm_w[i][None], xwo_b, s)

        h = _ffn(h2, ffn_norm_w[i][None], wg_b, wu_b, wd_b)
    return h.reshape(b, s, dm)
```

```python
import functools
import math

import jax
import jax.numpy as jnp
import numpy as np
from jax import lax
from jax.experimental import pallas as pl
from jax.experimental.pallas import tpu as pltpu

EPS = 1e-6
LANES = 128
SUBLANES = 8
VMEM_LIMIT = 48 * 1024 * 1024
BIG_VMEM_LIMIT = 56 * 1024 * 1024
LOG2E = math.log2(math.e)
DEN_ROWS = 16
CONV_TAIL = 16

DA_HEADS = 8
DA_HEAD_DIM = 64
DA_V_DIM = 128
SSD_HEAD_DIM = 64
SSD_GROUPS = 4
SSD_STATE = 128
SSD_CONV = 4
SSD_CHUNK = 128
X_HEADS = 4
X_HEAD_DIM = 128

F32 = jnp.float32
BF16 = jnp.bfloat16


def _rms_scale(xf, eps=EPS):
    return lax.rsqrt(jnp.mean(xf * xf, axis=-1, keepdims=True) + eps)


def _silu(x):
    h = 0.5 * x
    return h + h * jnp.tanh(h)


def _split_bf16(x, parts):
    out = []
    r = x
    for _ in range(parts):
        p = r.astype(BF16)
        out.append(p)
        r = r - p.astype(F32)
    return out


def _dot(a, b):
    return jnp.dot(a, b, preferred_element_type=F32)


def _dot_nt(a, b):
    return lax.dot_general(a, b, (((1,), (1,)), ((), ())), preferred_element_type=F32)


def _lane_group_mean(xf, avg):
    hi, lo = _split_bf16(xf, 2)
    return _dot(hi, avg) + _dot(lo, avg)


def _in_proj_kernel(x_ref, nw_ref, wt_ref, wdt_ref, o_ref, dt_ref, hn_ref):
    n, mi = pl.program_id(1), pl.program_id(2)

    @pl.when(n == 0)
    def _():
        xf = x_ref[...]
        hn = (xf * _rms_scale(xf) * nw_ref[...]).astype(BF16)
        hn_ref[mi] = hn
        wdt = wdt_ref[...].astype(BF16)
        wdt = jnp.concatenate([wdt, jnp.zeros((LANES - wdt.shape[0], wdt.shape[1]), BF16)], axis=0)
        dt_ref[...] = _dot_nt(hn, wdt)

    o_ref[...] = _dot_nt(hn_ref[mi], wt_ref[...].astype(BF16)).astype(o_ref.dtype)


def _in_proj(x2, nw, wt_all, layer, n, col_shift, tm=1024, tn=1024, m_inner=2):
    m, k = x2.shape
    n_dt = wt_all.shape[1] - n
    shift, nblk = col_shift // tn, n // tn
    assert col_shift % tn == 0 and n % tn == 0 and n % n_dt == 0 and m % (tm * m_inner) == 0
    row = lambda mo, j, mi: mo * m_inner + jnp.where(j == 0, mi, m_inner - 1)
    return pl.pallas_call(
        _in_proj_kernel,
        out_shape=(jax.ShapeDtypeStruct((m, n), BF16), jax.ShapeDtypeStruct((m, LANES), F32)),
        grid=(m // (tm * m_inner), nblk, m_inner),
        in_specs=[
            pl.BlockSpec((tm, k), lambda mo, j, mi: (row(mo, j, mi), 0)),
            pl.BlockSpec((1, k), lambda mo, j, mi: (0, 0)),
            pl.BlockSpec((None, tn, k), lambda mo, j, mi: (layer, (j + shift) % nblk, 0)),
            pl.BlockSpec((None, n_dt, k), lambda mo, j, mi: (layer, n // n_dt, 0)),
        ],
        out_specs=(
            pl.BlockSpec((tm, tn), lambda mo, j, mi: (mo * m_inner + mi, j)),
            pl.BlockSpec((tm, LANES), lambda mo, j, mi: (row(mo, j, mi), 0)),
        ),
        scratch_shapes=[pltpu.VMEM((m_inner, tm, k), BF16)],
        compiler_params=pltpu.CompilerParams(
            dimension_semantics=("parallel", "arbitrary", "arbitrary"), vmem_limit_bytes=BIG_VMEM_LIMIT),
        name="in_proj",
    )(x2, nw, wt_all, wt_all)


def _xkv_kernel(x_ref, nw_ref, w_ref, kw_ref, o_ref):
    d = X_HEAD_DIM
    xf = x_ref[...]
    hn = (xf * _rms_scale(xf) * nw_ref[...]).astype(BF16)
    kv = _dot(hn, w_ref[...].astype(BF16))
    kw = X_HEADS * d
    for hh in range(X_HEADS):
        kf = kv[:, hh * d:(hh + 1) * d]
        o_ref[:, hh * d:(hh + 1) * d] = (kf * _rms_scale(kf) * kw_ref[...]).astype(o_ref.dtype)
    o_ref[:, kw:] = kv[:, kw:].astype(o_ref.dtype)


def _xkv_proj(x2, nw, w_all, layer, kw, tm=512):
    m, k = x2.shape
    n = w_all.shape[2]
    return pl.pallas_call(
        _xkv_kernel,
        out_shape=jax.ShapeDtypeStruct((m, n), BF16),
        grid=(m // tm,),
        in_specs=[
            pl.BlockSpec((tm, k), lambda i: (i, 0)),
            pl.BlockSpec((1, k), lambda i: (0, 0)),
            pl.BlockSpec((None, k, n), lambda i: (layer, 0, 0)),
            pl.BlockSpec((1, X_HEAD_DIM), lambda i: (0, 0)),
        ],
        out_specs=pl.BlockSpec((tm, n), lambda i: (i, 0)),
        compiler_params=pltpu.CompilerParams(
            dimension_semantics=("parallel",), vmem_limit_bytes=VMEM_LIMIT),
        name="xkv_proj",
    )(x2, nw, w_all, kw)


def _diff_attn_kernel(q_ref, k_ref, v_ref, qw_ref, kw_ref, lam_ref, sw_ref, *rest,
                      tq, hb, lambda_init, n_cast):
    cast_in, o_ref, cast_out = rest[:n_cast], rest[n_cast], rest[n_cast + 1:2 * n_cast + 1]
    kn_ref, vt_ref, qcat_ref, m_ref, acc_ref, s_ref = rest[2 * n_cast + 1:]
    for w_ref, wb_ref in zip(cast_in, cast_out):
        wb_ref[...] = w_ref[...].astype(BF16)
    i = pl.program_id(2)
    n_tiles = vt_ref.shape[1]
    d = DA_V_DIM
    lane = lax.broadcasted_iota(jnp.int32, (1, d), 1)
    first = lane < DA_HEAD_DIM
    ones_row = lax.broadcasted_iota(jnp.int32, (DEN_ROWS, tq), 0) == 0

    half_r = lax.broadcasted_iota(jnp.int32, (d, d), 0) < DA_HEAD_DIM
    half_c = lax.broadcasted_iota(jnp.int32, (d, d), 1) < DA_HEAD_DIM
    sub_avg = jnp.where(half_r == half_c, 1.0 / DA_HEAD_DIM, 0.0).astype(BF16)

    def sub_head_norm(xf):
        return xf * lax.rsqrt(_lane_group_mean(xf * xf, sub_avg) + EPS)

    @pl.when(i == 0)
    def _():
        for hh in range(hb):
            hs = slice(hh * d, (hh + 1) * d)
            kn_ref[hh] = (sub_head_norm(k_ref[:, hs].astype(F32)) * kw_ref[...]).astype(BF16)
            for t in range(n_tiles):
                vt_ref[hh, t, :d] = v_ref[t * tq:(t + 1) * tq, hs].T
                vt_ref[hh, t, d:] = jnp.where(ones_row, 1.0, 0.0).astype(BF16)

    for hh in range(hb):
        qn = sub_head_norm(q_ref[:, hh * d:(hh + 1) * d].astype(F32)) * qw_ref[...] * (DA_HEAD_DIM ** -0.5 * LOG2E)
        qcat_ref[hh] = jnp.concatenate(
            [jnp.where(first, qn, 0.0).T, jnp.where(first, 0.0, qn).T], axis=1).astype(BF16)
    m_ref[...] = jnp.full(m_ref.shape, -jnp.inf, F32)
    acc_ref[...] = jnp.zeros(acc_ref.shape, F32)

    def scores(j, hh):
        start = pl.multiple_of(j * tq, tq)
        return _dot(kn_ref[hh, pl.ds(start, tq), :], qcat_ref[hh])

    lam_p = lam_ref[...]
    lam = (jnp.exp(jnp.sum(lam_p[0:1] * lam_p[1:2], axis=-1, keepdims=True))
           - jnp.exp(jnp.sum(lam_p[2:3] * lam_p[3:4], axis=-1, keepdims=True)) + lambda_init)

    def tile(j, cur, last):
        if last:
            key = lax.broadcasted_iota(jnp.int32, (tq, 2 * tq), 0)
            qry = lax.broadcasted_iota(jnp.int32, (tq, 2 * tq), 1)
            causal = key <= jnp.where(qry >= tq, qry - tq, qry)
        for hh in range(hb):
            if not last:
                s_ref[1 - cur, hh] = scores(j + 1, hh)
            s = s_ref[cur, hh]
            if last:
                s = jnp.where(causal, s, -jnp.inf)
            m_prev = m_ref[hh]
            m_new = jnp.maximum(m_prev, jnp.max(s, axis=0, keepdims=True))
            alpha = jnp.exp2(m_prev - m_new)
            p = jnp.exp2(s - m_new)
            acc = alpha * acc_ref[hh] + _dot(vt_ref[hh, j], p.astype(BF16))
            if not last:
                acc_ref[hh] = acc
                m_ref[hh] = m_new
            else:
                ot = acc[:d] / acc[d:d + 1]
                o = (ot[:, :tq] - lam * ot[:, tq:]).T
                o = o * _rms_scale(o) * sw_ref[...] * (1.0 - lambda_init)
                o_ref[:, hh * d:(hh + 1) * d] = o.astype(o_ref.dtype)

    for hh in range(hb):
        s_ref[0, hh] = scores(0, hh)

    def body(jj, carry):
        tile(2 * jj, 0, False)
        tile(2 * jj + 1, 1, False)
        return carry

    lax.fori_loop(0, i // 2, body, 0)

    @pl.when(i % 2 == 0)
    def _():
        tile(i, 0, True)

    @pl.when(i % 2 == 1)
    def _():
        tile(i - 1, 0, False)
        tile(i, 1, True)


def _diff_attn(proj3, qw2, kw2, lam_p, sw, lambda_init, q_col, k_col, v_col, cast_ws, layer, tq=256, hb=8):
    b, s, _ = proj3.shape
    d = DA_V_DIM
    nh, nq = DA_HEADS // hb, s // tq
    steps = b * nh * nq
    kern = functools.partial(_diff_attn_kernel, tq=tq, hb=hb, lambda_init=lambda_init, n_cast=len(cast_ws))
    vec = lambda shape: pl.BlockSpec(shape, lambda bi, h, i: (0, 0))
    cast_in, cast_out, cast_shapes = [], [], []
    for w in cast_ws:
        rows, cols = w.shape[1:]
        rb = rows // steps
        assert rows % steps == 0 and rb % (2 * SUBLANES) == 0
        cast_in.append(pl.BlockSpec((None, rb, cols), lambda bi, h, i: (layer, (bi * nh + h) * nq + i, 0)))
        cast_out.append(pl.BlockSpec((rb, cols), lambda bi, h, i: ((bi * nh + h) * nq + i, 0)))
        cast_shapes.append(jax.ShapeDtypeStruct((rows, cols), BF16))
    outs = pl.pallas_call(
        kern,
        out_shape=[jax.ShapeDtypeStruct((b, s, DA_HEADS * d), BF16)] + cast_shapes,
        grid=(b, nh, nq),
        in_specs=[
            pl.BlockSpec((None, tq, hb * d), lambda bi, h, i: (bi, i, q_col // hb + h)),
            pl.BlockSpec((None, s, hb * d), lambda bi, h, i: (bi, 0, k_col // hb + h)),
            pl.BlockSpec((None, s, hb * d), lambda bi, h, i: (bi, 0, v_col // hb + h)),
            vec((1, d)), vec((1, d)), vec((4, DA_HEAD_DIM)), vec((1, d)),
        ] + cast_in,
        out_specs=[pl.BlockSpec((None, tq, hb * d), lambda bi, h, i: (bi, i, h))] + cast_out,
        scratch_shapes=[
            pltpu.VMEM((hb, s, d), BF16),
            pltpu.VMEM((hb, s // tq, d + DEN_ROWS, tq), BF16),
            pltpu.VMEM((hb, d, 2 * tq), BF16),
            pltpu.VMEM((hb, 1, 2 * tq), F32),
            pltpu.VMEM((hb, d + DEN_ROWS, 2 * tq), F32),
            pltpu.VMEM((2, hb, tq, 2 * tq), F32),
        ],
        compiler_params=pltpu.CompilerParams(
            dimension_semantics=("parallel", "parallel", "arbitrary"), vmem_limit_bytes=BIG_VMEM_LIMIT),
        name="diff_attn",
    )(proj3, proj3, proj3, qw2, kw2, lam_p, sw, *cast_ws)
    return outs[0], outs[1:]


def _ssd_kernel(z_ref, xs_ref, bc_ref, xsn_ref, bcn_ref, dt_ref, cwx_ref, cbx_ref, cwbc_ref, cbbc_ref,
                dtb_ref, alog_ref, dx_ref, nw_ref, e_ref, *rest, n_cast):
    cast_in, o_ref, cast_out = rest[:n_cast], rest[n_cast], rest[n_cast + 1:2 * n_cast + 1]
    state_ref, cxa_ref, cba_ref, cxb_ref, cbb_ref, y_ref, dtx_ref, acsx_ref = rest[2 * n_cast + 1:]
    L, N, G = SSD_CHUNK, SSD_STATE, SSD_GROUPS
    w = o_ref.shape[-1]
    bcw = bc_ref.shape[-1]
    gw = w // G
    c = pl.program_id(1)

    ext = CONV_TAIL + L
    s_row = lax.broadcasted_iota(jnp.int32, (L, SSD_CONV * ext), 0)
    s_col = lax.broadcasted_iota(jnp.int32, (L, SSD_CONV * ext), 1)
    pick = jnp.zeros((L, SSD_CONV * ext), F32)
    for k in range(SSD_CONV):
        pick = jnp.where(s_col == s_row + (k * ext + CONV_TAIL - k), 0.5, pick)
    pick = pick.astype(BF16)

    def conv_silu(rows, w_ref, b_ref, cs, dst_ref):
        n_rt = rows.shape[0] // CONV_TAIL
        rows3 = rows.reshape(n_rt, CONV_TAIL, rows.shape[1])
        taps = jnp.concatenate(
            [(rows3 * w_ref[SSD_CONV - 1 - k, :, cs][None]).reshape(rows.shape) for k in range(SSD_CONV)], axis=0)
        h = _dot(pick, taps) + 0.5 * b_ref[:, cs]
        dst_ref[:, cs] = h + h * jnp.tanh(h)

    def conv_slice(g, rows_of, dst_x_ref, dst_bc_ref):
        xc = slice(g * (w // G), (g + 1) * (w // G))
        bc = slice(g * (bcw // G), (g + 1) * (bcw // G))
        conv_silu(rows_of(xs_ref, xsn_ref, xc), cwx_ref, cbx_ref, xc, dst_x_ref)
        conv_silu(rows_of(bc_ref, bcn_ref, bc), cwbc_ref, cbbc_ref, bc, dst_bc_ref)

    def rows_first(cur_ref, nxt_ref, cs):
        return jnp.concatenate([jnp.zeros((CONV_TAIL, cs.stop - cs.start), BF16), cur_ref[0:L, cs]], axis=0)

    def rows_second(cur_ref, nxt_ref, cs):
        return cur_ref[L - CONV_TAIL:2 * L, cs]

    def rows_next(cur_ref, nxt_ref, cs):
        return jnp.concatenate([cur_ref[2 * L - CONV_TAIL:2 * L, cs], nxt_ref[:, cs]], axis=0)

    @pl.when(c == 0)
    def _():
        state_ref[...] = jnp.zeros(state_ref.shape, F32)
        for g in range(G):
            conv_slice(g, rows_first, cxa_ref, cba_ref)

    r_i = lax.broadcasted_iota(jnp.int32, (L, L), 0)
    c_i = lax.broadcasted_iota(jnp.int32, (L, L), 1)
    causal = c_i <= r_i
    tril = jnp.where(causal, 1.0, 0.0).astype(BF16)
    lane = lax.broadcasted_iota(jnp.int32, (1, LANES), 1)
    first = lane < SSD_HEAD_DIM
    n_heads = w // SSD_HEAD_DIM
    heads_per_group = gw // SSD_HEAD_DIM

    def expand(v, parts):
        packed, r = None, jnp.where(lane < n_heads, v, 0.0)
        for t in range(parts):
            p = r.astype(BF16).astype(F32)
            packed = p if t == 0 else packed + pltpu.roll(p, t * n_heads, 1)
            r = r - p
        return _dot(packed.astype(BF16), e_ref[...])

    def mix_chunk(r0, cx_ref, cb_ref, side):
        rows = slice(r0, r0 + L)
        dtx = dt_ref[rows, :] + dtb_ref[...]
        dt = jnp.maximum(dtx, 0.0) + jnp.log1p(jnp.exp(-jnp.abs(dtx)))
        a = -jnp.exp(alog_ref[...]) * LOG2E
        acs = sum(_dot(tril, p) for p in _split_bf16(dt * a, 3))
        acs_t = acs.T
        dtx_ref[...] = expand(dt, 2)
        acsx_ref[...] = expand(acs, 3)
        for g in range(G):
            side(g)
            cols = slice(g * gw, (g + 1) * gw)
            xs_g = cx_ref[:, cols]
            acs_g = acsx_ref[:, cols]
            last_g = acsx_ref[L - 1:L, cols]
            xdt_g = xs_g * dtx_ref[:, cols]
            xds_g = (xdt_g * jnp.exp2(last_g - acs_g)).astype(BF16)
            b_g = cb_ref[:, g * N:(g + 1) * N]
            c_g = cb_ref[:, (G + g) * N:(G + g + 1) * N].astype(BF16)
            cb = _dot_nt(c_g, b_g.astype(BF16))
            state_old = state_ref[:, cols]
            y_off = _dot(c_g, state_old.astype(BF16)) * jnp.exp2(acs_g)
            state_ref[:, cols] = state_old * jnp.exp2(last_g) + _dot(b_g.T.astype(BF16), xds_g)
            for pair in range(heads_per_group // 2):
                h0 = g * heads_per_group + 2 * pair
                pc = slice(pair * LANES, (pair + 1) * LANES)
                xp = xdt_g[:, pc]
                x_pair = jnp.concatenate([jnp.where(first, xp, 0.0), jnp.where(first, 0.0, xp)], axis=0).astype(BF16)
                mats = []
                for h in (h0, h0 + 1):
                    seg = acs[:, h:h + 1] - acs_t[h:h + 1, :]
                    dec = jnp.exp2(jnp.where(causal, seg, -jnp.inf))
                    mats.append((cb * dec).astype(BF16))
                y_ref[:, pc] = _dot(jnp.concatenate(mats, axis=1), x_pair)
            y = y_ref[...] + y_off + xs_g * dx_ref[:, cols]
            y = y * _silu(z_ref[rows, cols].astype(F32))
            o_ref[rows, cols] = (y * _rms_scale(y) * nw_ref[:, cols]).astype(o_ref.dtype)

    mix_chunk(0, cxa_ref, cba_ref, lambda g: conv_slice(g, rows_second, cxb_ref, cbb_ref))
    for w_ref, wb_ref in zip(cast_in, cast_out):
        wb_ref[...] = w_ref[...].astype(BF16)
    mix_chunk(L, cxb_ref, cbb_ref, lambda g: conv_slice(g, rows_next, cxa_ref, cba_ref))


def _ssd(proj3, dt3, cwx, cbx, cwbc, cbbc, dtb, alog, dx, nw, e, z_col, xs_col, bc_col, cast_ws, layer):
    b, s, _ = proj3.shape
    w = e.shape[1]
    bcw = cwbc.shape[2]
    L = SSD_CHUNK
    nc = s // L
    assert nc % 2 == 0
    steps = b * (nc // 2)
    cast_in, cast_out, cast_shapes = [], [], []
    for cw in cast_ws:
        rows, cols = cw.shape[1:]
        rb = rows // steps
        assert rows % steps == 0 and rb % (2 * SUBLANES) == 0
        cast_in.append(pl.BlockSpec((None, rb, cols), lambda bi, c: (layer, bi * (nc // 2) + c, 0)))
        cast_out.append(pl.BlockSpec((rb, cols), lambda bi, c: (bi * (nc // 2) + c, 0)))
        cast_shapes.append(jax.ShapeDtypeStruct((rows, cols), BF16))
    full = lambda shape: pl.BlockSpec(shape, lambda bi, c: (0,) * len(shape))
    nxt = lambda c: jnp.minimum(2 * c + 2, nc - 1)
    outs = pl.pallas_call(
        functools.partial(_ssd_kernel, n_cast=len(cast_ws)),
        out_shape=[jax.ShapeDtypeStruct((b, s, w), BF16)] + cast_shapes,
        grid=(b, nc // 2),
        in_specs=[
            pl.BlockSpec((None, 2 * L, w), lambda bi, c: (bi, c, z_col)),
            pl.BlockSpec((None, 2 * L, w), lambda bi, c: (bi, c, xs_col)),
            pl.BlockSpec((None, 2 * L, bcw), lambda bi, c: (bi, c, bc_col)),
            pl.BlockSpec((None, L, w), lambda bi, c: (bi, nxt(c), xs_col)),
            pl.BlockSpec((None, L, bcw), lambda bi, c: (bi, nxt(c), bc_col)),
            pl.BlockSpec((None, 2 * L, LANES), lambda bi, c: (bi, c, 0)),
            full((SSD_CONV, CONV_TAIL, w)), full((1, w)), full((SSD_CONV, CONV_TAIL, bcw)), full((1, bcw)),
            full((1, LANES)), full((1, LANES)), full((1, w)), full((1, w)), full((LANES, w)),
        ] + cast_in,
        out_specs=[pl.BlockSpec((None, 2 * L, w), lambda bi, c: (bi, c, 0))] + cast_out,
        scratch_shapes=[
            pltpu.VMEM((SSD_STATE, w), F32),
            pltpu.VMEM((L, w), F32), pltpu.VMEM((L, bcw), F32),
            pltpu.VMEM((L, w), F32), pltpu.VMEM((L, bcw), F32),
            pltpu.VMEM((L, w // SSD_GROUPS), F32),
            pltpu.VMEM((L, w), F32), pltpu.VMEM((L, w), F32),
        ],
        compiler_params=pltpu.CompilerParams(
            dimension_semantics=("parallel", "arbitrary"), vmem_limit_bytes=VMEM_LIMIT),
        name="ssd",
    )(proj3, proj3, proj3, proj3, proj3, dt3, cwx, cbx, cwbc, cbbc, dtb, alog, dx, nw, e, *cast_ws)
    return outs[0], outs[1:]


def _mix_out_kernel(a_ref, s_ref, x_ref, w_ref, xnw_ref, wq_ref, kv_ref, qw_ref, wo_ref, o_ref):
    d = X_HEAD_DIM
    ka = a_ref.shape[1]
    h1 = x_ref[...] + _dot(a_ref[...], w_ref[0])
    for c in range(1, w_ref.shape[0]):
        h1 = h1 + _dot(s_ref[:, (c - 1) * ka:c * ka], w_ref[c])
    hn = (h1 * _rms_scale(h1) * xnw_ref[...]).astype(BF16)
    q = _dot(hn, wq_ref[...])
    outs = []
    for hh in range(X_HEADS):
        qf = q[:, hh * d:(hh + 1) * d]
        qn = (qf * _rms_scale(qf) * qw_ref[...] * (d ** -0.5 * LOG2E)).astype(BF16)
        s = _dot_nt(qn, kv_ref[:, hh * d:(hh + 1) * d])
        p = jnp.exp2(s - jnp.max(s, axis=-1, keepdims=True))
        v = kv_ref[:, (X_HEADS + hh) * d:(X_HEADS + hh + 1) * d]
        o = _dot(p.astype(BF16), v) / jnp.sum(p, axis=-1, keepdims=True)
        outs.append(o.astype(BF16))
    o_ref[...] = h1 + _dot(jnp.concatenate(outs, axis=-1), wo_ref[...])


def _mix_out(a2, s2, x2, w_out, xnw, wq, kv3, qw, wo, seq, tm=512):
    m, ka = a2.shape
    ks = s2.shape[1]
    dm = x2.shape[1]
    xw = wq.shape[1]
    mlen = kv3.shape[1]
    assert seq % tm == 0 and ks % ka == 0
    once = dict(pipeline_mode=pl.Buffered(1))
    return pl.pallas_call(
        _mix_out_kernel,
        out_shape=jax.ShapeDtypeStruct((m, dm), F32),
        grid=(m // tm,),
        in_specs=[
            pl.BlockSpec((tm, ka), lambda i: (i, 0)),
            pl.BlockSpec((tm, ks), lambda i: (i, 0)),
            pl.BlockSpec((tm, dm), lambda i: (i, 0)),
            pl.BlockSpec((1 + ks // ka, ka, dm), lambda i: (0, 0, 0), **once),
            pl.BlockSpec((1, dm), lambda i: (0, 0)),
            pl.BlockSpec((dm, xw), lambda i: (0, 0), **once),
            pl.BlockSpec((None, mlen, 2 * xw), lambda i: (i // (seq // tm), 0, 0)),
            pl.BlockSpec((1, X_HEAD_DIM), lambda i: (0, 0)),
            pl.BlockSpec((xw, dm), lambda i: (0, 0), **once),
        ],
        out_specs=pl.BlockSpec((tm, dm), lambda i: (i, 0)),
        compiler_params=pltpu.CompilerParams(
            dimension_semantics=("parallel",), vmem_limit_bytes=BIG_VMEM_LIMIT),
        name="mix_out",
    )(a2, s2, x2, w_out.reshape(1 + ks // ka, ka, dm), xnw, wq, kv3, qw, wo)


def _ffn_kernel(h_ref, nw_ref, wg_ref, wu_ref, wd_ref, o_ref, hn_ref):
    f = pl.program_id(1)

    @pl.when(f == 0)
    def _():
        hf = h_ref[...]
        hn_ref[...] = (hf * _rms_scale(hf) * nw_ref[...]).astype(BF16)
        o_ref[...] = hf

    hn = hn_ref[...]
    act = (_silu(_dot(hn, wg_ref[...])) * _dot(hn, wu_ref[...])).astype(BF16)
    o_ref[...] += _dot(act, wd_ref[...])


def _ffn(h2, nw, wg, wu, wd, tm=1024, tf=512):
    m, k = h2.shape
    dff = wg.shape[1]
    return pl.pallas_call(
        _ffn_kernel,
        out_shape=jax.ShapeDtypeStruct((m, k), F32),
        grid=(m // tm, dff // tf),
        in_specs=[
            pl.BlockSpec((tm, k), lambda i, f: (i, 0), pipeline_mode=pl.Buffered(1)),
            pl.BlockSpec((1, k), lambda i, f: (0, 0)),
            pl.BlockSpec((k, tf), lambda i, f: (0, f)),
            pl.BlockSpec((k, tf), lambda i, f: (0, f)),
            pl.BlockSpec((tf, k), lambda i, f: (f, 0)),
        ],
        out_specs=pl.BlockSpec((tm, k), lambda i, f: (i, 0)),
        scratch_shapes=[pltpu.VMEM((tm, k), BF16)],
        compiler_params=pltpu.CompilerParams(
            dimension_semantics=("parallel", "arbitrary"), vmem_limit_bytes=BIG_VMEM_LIMIT),
        name="ffn",
    )(h2, nw, wg, wu, wd)


def _pad_lanes(v, fill=0.0):
    return jnp.pad(v, ((0, 0), (0, LANES - v.shape[1])), constant_values=fill)


def kernel(x, mem, mix_norm_w, w_in, da_q_norm_w, da_k_norm_w, da_lambda_q1, da_lambda_k1, da_lambda_q2, da_lambda_k2, da_subln_w, ssd_conv_w, ssd_conv_b, ssd_dt_bias, ssd_a_log, ssd_d, ssd_norm_w, w_out, xattn_norm_w, mem_norm_w, xattn_w_q, xattn_w_kv, xattn_q_norm_w, xattn_k_norm_w, xattn_w_o, ffn_norm_w, ffn_w_gate, ffn_w_up, ffn_w_down):
    b, s, dm = x.shape
    mlen = mem.shape[1]
    depth = w_in.shape[0]
    da_w = DA_HEADS * DA_V_DIM
    ssd_w = ssd_norm_w.shape[1]
    ssd_heads = ssd_w // SSD_HEAD_DIM
    gn = SSD_GROUPS * SSD_STATE
    xw = X_HEADS * X_HEAD_DIM
    o_q, o_k, o_v, o_z = 0, da_w, 2 * da_w, 3 * da_w
    o_xs = o_z + ssd_w
    o_bc = o_xs + ssd_w
    o_dt = o_bc + 2 * gn
    z_col, xs_col = 0, 1
    bc_col = (2 * ssd_w) // (2 * gn)
    q_col = (2 * ssd_w + 2 * gn) // DA_V_DIM
    k_col = q_col + DA_HEADS
    v_col = k_col + DA_HEADS
    assert (2 * ssd_w) % (2 * gn) == 0 and ssd_heads <= LANES

    assert LANES % ssd_heads == 0 and LANES // ssd_heads >= 3
    expand = np.zeros((LANES, ssd_w), np.float32)
    for r in range(LANES):
        h = r % ssd_heads
        expand[r, h * SSD_HEAD_DIM:(h + 1) * SSD_HEAD_DIM] = 1.0
    expand = jnp.asarray(expand, BF16)

    h = x.reshape(b * s, dm)
    mem2 = mem.reshape(b * mlen, dm)
    for i in range(depth):
        lambda_init = 0.8 - 0.6 * math.exp(-0.3 * i)
        proj, dt_raw = _in_proj(h, mix_norm_w[i][None], jnp.swapaxes(w_in, 1, 2), i, o_dt, o_z)
        proj3 = proj.reshape(b, s, -1)

        lam_p = jnp.stack([da_lambda_q1[i], da_lambda_k1[i], da_lambda_q2[i], da_lambda_k2[i]])
        a_out, _ = _diff_attn(
            proj3, jnp.tile(da_q_norm_w[i], 2)[None], jnp.tile(da_k_norm_w[i], 2)[None],
            lam_p, da_subln_w[i][None], lambda_init, q_col, k_col, v_col, [], i)

        cw = jnp.broadcast_to(ssd_conv_w[i][:, None, :], (SSD_CONV, CONV_TAIL, ssd_conv_w.shape[2])).astype(BF16)
        cb = ssd_conv_b[i][None]
        s_out, (wo_b, wq_b, xwo_b, wg_b, wu_b, wd_b) = _ssd(
            proj3, dt_raw.reshape(b, s, LANES),
            cw[:, :, :ssd_w], cb[:, :ssd_w], cw[:, :, ssd_w:], cb[:, ssd_w:],
            _pad_lanes(ssd_dt_bias[i][None]), _pad_lanes(ssd_a_log[i][None]),
            jnp.repeat(ssd_d[i], SSD_HEAD_DIM)[None], ssd_norm_w[i][None], expand,
            z_col, xs_col, bc_col,
            [w_out, xattn_w_q, xattn_w_o, ffn_w_gate, ffn_w_up, ffn_w_down], i)

        xkv = _xkv_proj(mem2, mem_norm_w[i][None], xattn_w_kv, i, xattn_k_norm_w[i][None])
        h2 = _mix_out(a_out.reshape(b * s, da_w), s_out.reshape(b * s, ssd_w), h, wo_b,
                      xattn_norm_w[i][None], wq_b, xkv.reshape(b, mlen, 2 * xw), xattn_q_norm_w[i][None], xwo_b, s)

        h = _ffn(h2, ffn_norm_w[i][None], wg_b, wu_b, wd_b)
    return h.reshape(b, s, dm)
```

```python
import functools
import math

import jax
import jax.numpy as jnp
import numpy as np
from jax import lax
from jax.experimental import pallas as pl
from jax.experimental.pallas import tpu as pltpu

EPS = 1e-6
LANES = 128
SUBLANES = 8
VMEM_LIMIT = 48 * 1024 * 1024
BIG_VMEM_LIMIT = 56 * 1024 * 1024
LOG2E = math.log2(math.e)
DEN_ROWS = 16
CONV_TAIL = 16

DA_HEADS = 8
DA_HEAD_DIM = 64
DA_V_DIM = 128
SSD_HEAD_DIM = 64
SSD_GROUPS = 4
SSD_STATE = 128
SSD_CONV = 4
SSD_CHUNK = 128
X_HEADS = 4
X_HEAD_DIM = 128

F32 = jnp.float32
BF16 = jnp.bfloat16


def _rms_scale(xf, eps=EPS):
    return lax.rsqrt(jnp.mean(xf * xf, axis=-1, keepdims=True) + eps)


def _silu(x):
    h = 0.5 * x
    return h + h * jnp.tanh(h)


def _split_bf16(x, parts):
    out = []
    r = x
    for _ in range(parts):
        p = r.astype(BF16)
        out.append(p)
        r = r - p.astype(F32)
    return out


def _dot(a, b):
    return jnp.dot(a, b, preferred_element_type=F32)


def _dot_nt(a, b):
    return lax.dot_general(a, b, (((1,), (1,)), ((), ())), preferred_element_type=F32)


def _lane_group_mean(xf, avg):
    hi, lo = _split_bf16(xf, 2)
    return _dot(hi, avg) + _dot(lo, avg)


def _in_proj_kernel(x_ref, nw_ref, wt_ref, wdt_ref, o_ref, dt_ref, hn_ref):
    n, mi = pl.program_id(1), pl.program_id(2)

    @pl.when(n == 0)
    def _():
        xf = x_ref[...]
        hn = (xf * _rms_scale(xf) * nw_ref[...]).astype(BF16)
        hn_ref[mi] = hn
        wdt = wdt_ref[...].astype(BF16)
        wdt = jnp.concatenate([wdt, jnp.zeros((LANES - wdt.shape[0], wdt.shape[1]), BF16)], axis=0)
        dt_ref[...] = _dot_nt(hn, wdt)

    o_ref[...] = _dot_nt(hn_ref[mi], wt_ref[...].astype(BF16)).astype(o_ref.dtype)


def _in_proj(x2, nw, wt_all, layer, n, col_shift, tm=1024, tn=1024, m_inner=2):
    m, k = x2.shape
    n_dt = wt_all.shape[1] - n
    shift, nblk = col_shift // tn, n // tn
    assert col_shift % tn == 0 and n % tn == 0 and n % n_dt == 0 and m % (tm * m_inner) == 0
    row = lambda mo, j, mi: mo * m_inner + jnp.where(j == 0, mi, m_inner - 1)
    return pl.pallas_call(
        _in_proj_kernel,
        out_shape=(jax.ShapeDtypeStruct((m, n), BF16), jax.ShapeDtypeStruct((m, LANES), F32)),
        grid=(m // (tm * m_inner), nblk, m_inner),
        in_specs=[
            pl.BlockSpec((tm, k), lambda mo, j, mi: (row(mo, j, mi), 0)),
            pl.BlockSpec((1, k), lambda mo, j, mi: (0, 0)),
            pl.BlockSpec((None, tn, k), lambda mo, j, mi: (layer, (j + shift) % nblk, 0)),
            pl.BlockSpec((None, n_dt, k), lambda mo, j, mi: (layer, n // n_dt, 0)),
        ],
        out_specs=(
            pl.BlockSpec((tm, tn), lambda mo, j, mi: (mo * m_inner + mi, j)),
            pl.BlockSpec((tm, LANES), lambda mo, j, mi: (row(mo, j, mi), 0)),
        ),
        scratch_shapes=[pltpu.VMEM((m_inner, tm, k), BF16)],
        compiler_params=pltpu.CompilerParams(
            dimension_semantics=("parallel", "arbitrary", "arbitrary"), vmem_limit_bytes=BIG_VMEM_LIMIT),
        name="in_proj",
    )(x2, nw, wt_all, wt_all)


def _xkv_kernel(x_ref, nw_ref, w_ref, kw_ref, o_ref):
    d = X_HEAD_DIM
    xf = x_ref[...]
    hn = (xf * _rms_scale(xf) * nw_ref[...]).astype(BF16)
    kv = _dot(hn, w_ref[...].astype(BF16))
    kw = X_HEADS * d
    for hh in range(X_HEADS):
        kf = kv[:, hh * d:(hh + 1) * d]
        o_ref[:, hh * d:(hh + 1) * d] = (kf * _rms_scale(kf) * kw_ref[...]).astype(o_ref.dtype)
    o_ref[:, kw:] = kv[:, kw:].astype(o_ref.dtype)


def _xkv_proj(x2, nw, w_all, layer, kw, tm=512):
    m, k = x2.shape
    n = w_all.shape[2]
    return pl.pallas_call(
        _xkv_kernel,
        out_shape=jax.ShapeDtypeStruct((m, n), BF16),
        grid=(m // tm,),
        in_specs=[
            pl.BlockSpec((tm, k), lambda i: (i, 0)),
            pl.BlockSpec((1, k), lambda i: (0, 0)),
            pl.BlockSpec((None, k, n), lambda i: (layer, 0, 0)),
            pl.BlockSpec((1, X_HEAD_DIM), lambda i: (0, 0)),
        ],
        out_specs=pl.BlockSpec((tm, n), lambda i: (i, 0)),
        compiler_params=pltpu.CompilerParams(
            dimension_semantics=("parallel",), vmem_limit_bytes=VMEM_LIMIT),
        name="xkv_proj",
    )(x2, nw, w_all, kw)


def _diff_attn_kernel(q_ref, k_ref, v_ref, qw_ref, kw_ref, lam_ref, sw_ref, *rest,
                      tq, hb, lambda_init, n_cast):
    cast_in, o_ref, cast_out = rest[:n_cast], rest[n_cast], rest[n_cast + 1:2 * n_cast + 1]
    kn_ref, vt_ref, qcat_ref, m_ref, acc_ref, s_ref = rest[2 * n_cast + 1:]
    for w_ref, wb_ref in zip(cast_in, cast_out):
        wb_ref[...] = w_ref[...].astype(BF16)
    i = pl.program_id(2)
    n_tiles = vt_ref.shape[1]
    d = DA_V_DIM
    lane = lax.broadcasted_iota(jnp.int32, (1, d), 1)
    first = lane < DA_HEAD_DIM
    ones_row = lax.broadcasted_iota(jnp.int32, (DEN_ROWS, tq), 0) == 0

    half_r = lax.broadcasted_iota(jnp.int32, (d, d), 0) < DA_HEAD_DIM
    half_c = lax.broadcasted_iota(jnp.int32, (d, d), 1) < DA_HEAD_DIM
    sub_avg = jnp.where(half_r == half_c, 1.0 / DA_HEAD_DIM, 0.0).astype(BF16)

    def sub_head_norm(xf):
        return xf * lax.rsqrt(_lane_group_mean(xf * xf, sub_avg) + EPS)

    @pl.when(i == 0)
    def _():
        for hh in range(hb):
            hs = slice(hh * d, (hh + 1) * d)
            kn_ref[hh] = (sub_head_norm(k_ref[:, hs].astype(F32)) * kw_ref[...]).astype(BF16)
            for t in range(n_tiles):
                vt_ref[hh, t, :d] = v_ref[t * tq:(t + 1) * tq, hs].T
                vt_ref[hh, t, d:] = jnp.where(ones_row, 1.0, 0.0).astype(BF16)

    for hh in range(hb):
        qn = sub_head_norm(q_ref[:, hh * d:(hh + 1) * d].astype(F32)) * qw_ref[...] * (DA_HEAD_DIM ** -0.5 * LOG2E)
        qcat_ref[hh] = jnp.concatenate(
            [jnp.where(first, qn, 0.0).T, jnp.where(first, 0.0, qn).T], axis=1).astype(BF16)
    m_ref[...] = jnp.full(m_ref.shape, -jnp.inf, F32)
    acc_ref[...] = jnp.zeros(acc_ref.shape, F32)

    def scores(j, hh):
        start = pl.multiple_of(j * tq, tq)
        return _dot(kn_ref[hh, pl.ds(start, tq), :], qcat_ref[hh])

    lam_p = lam_ref[...]
    lam = (jnp.exp(jnp.sum(lam_p[0:1] * lam_p[1:2], axis=-1, keepdims=True))
           - jnp.exp(jnp.sum(lam_p[2:3] * lam_p[3:4], axis=-1, keepdims=True)) + lambda_init)

    def tile(j, cur, last):
        if last:
            key = lax.broadcasted_iota(jnp.int32, (tq, 2 * tq), 0)
            qry = lax.broadcasted_iota(jnp.int32, (tq, 2 * tq), 1)
            causal = key <= jnp.where(qry >= tq, qry - tq, qry)
        for hh in range(hb):
            if not last:
                s_ref[1 - cur, hh] = scores(j + 1, hh)
            s = s_ref[cur, hh]
            if last:
                s = jnp.where(causal, s, -jnp.inf)
            m_prev = m_ref[hh]
            m_new = jnp.maximum(m_prev, jnp.max(s, axis=0, keepdims=True))
            alpha = jnp.exp2(m_prev - m_new)
            p = jnp.exp2(s - m_new)
            acc = alpha * acc_ref[hh] + _dot(vt_ref[hh, j], p.astype(BF16))
            if not last:
                acc_ref[hh] = acc
                m_ref[hh] = m_new
            else:
                ot = acc[:d] / acc[d:d + 1]
                o = (ot[:, :tq] - lam * ot[:, tq:]).T
                o = o * _rms_scale(o) * sw_ref[...] * (1.0 - lambda_init)
                o_ref[:, hh * d:(hh + 1) * d] = o.astype(o_ref.dtype)

    for hh in range(hb):
        s_ref[0, hh] = scores(0, hh)

    def body(jj, carry):
        tile(2 * jj, 0, False)
        tile(2 * jj + 1, 1, False)
        return carry

    lax.fori_loop(0, i // 2, body, 0)

    @pl.when(i % 2 == 0)
    def _():
        tile(i, 0, True)

    @pl.when(i % 2 == 1)
    def _():
        tile(i - 1, 0, False)
        tile(i, 1, True)


def _diff_attn(proj3, qw2, kw2, lam_p, sw, lambda_init, q_col, k_col, v_col, cast_ws, layer, tq=256, hb=8):
    b, s, _ = proj3.shape
    d = DA_V_DIM
    nh, nq = DA_HEADS // hb, s // tq
    steps = b * nh * nq
    kern = functools.partial(_diff_attn_kernel, tq=tq, hb=hb, lambda_init=lambda_init, n_cast=len(cast_ws))
    vec = lambda shape: pl.BlockSpec(shape, lambda bi, h, i: (0, 0))
    cast_in, cast_out, cast_shapes = [], [], []
    for w in cast_ws:
        rows, cols = w.shape[1:]
        rb = rows // steps
        assert rows % steps == 0 and rb % (2 * SUBLANES) == 0
        cast_in.append(pl.BlockSpec((None, rb, cols), lambda bi, h, i: (layer, (bi * nh + h) * nq + i, 0)))
        cast_out.append(pl.BlockSpec((rb, cols), lambda bi, h, i: ((bi * nh + h) * nq + i, 0)))
        cast_shapes.append(jax.ShapeDtypeStruct((rows, cols), BF16))
    outs = pl.pallas_call(
        kern,
        out_shape=[jax.ShapeDtypeStruct((b, s, DA_HEADS * d), BF16)] + cast_shapes,
        grid=(b, nh, nq),
        in_specs=[
            pl.BlockSpec((None, tq, hb * d), lambda bi, h, i: (bi, i, q_col // hb + h)),
            pl.BlockSpec((None, s, hb * d), lambda bi, h, i: (bi, 0, k_col // hb + h)),
            pl.BlockSpec((None, s, hb * d), lambda bi, h, i: (bi, 0, v_col // hb + h)),
            vec((1, d)), vec((1, d)), vec((4, DA_HEAD_DIM)), vec((1, d)),
        ] + cast_in,
        out_specs=[pl.BlockSpec((None, tq, hb * d), lambda bi, h, i: (bi, i, h))] + cast_out,
        scratch_shapes=[
            pltpu.VMEM((hb, s, d), BF16),
            pltpu.VMEM((hb, s // tq, d + DEN_ROWS, tq), BF16),
            pltpu.VMEM((hb, d, 2 * tq), BF16),
            pltpu.VMEM((hb, 1, 2 * tq), F32),
            pltpu.VMEM((hb, d + DEN_ROWS, 2 * tq), F32),
            pltpu.VMEM((2, hb, tq, 2 * tq), F32),
        ],
        compiler_params=pltpu.CompilerParams(
            dimension_semantics=("parallel", "parallel", "arbitrary"), vmem_limit_bytes=BIG_VMEM_LIMIT),
        name="diff_attn",
    )(proj3, proj3, proj3, qw2, kw2, lam_p, sw, *cast_ws)
    return outs[0], outs[1:]


def _ssd_kernel(z_ref, xs_ref, bc_ref, xsn_ref, bcn_ref, dt_ref, cwx_ref, cbx_ref, cwbc_ref, cbbc_ref,
                dtb_ref, alog_ref, dx_ref, nw_ref, e_ref, *rest, n_cast):
    cast_in, o_ref, cast_out = rest[:n_cast], rest[n_cast], rest[n_cast + 1:2 * n_cast + 1]
    state_ref, cxa_ref, cba_ref, cxb_ref, cbb_ref, y_ref, dtx_ref, acsx_ref = rest[2 * n_cast + 1:]
    L, N, G = SSD_CHUNK, SSD_STATE, SSD_GROUPS
    w = o_ref.shape[-1]
    bcw = bc_ref.shape[-1]
    gw = w // G
    c = pl.program_id(1)

    ext = CONV_TAIL + L
    s_row = lax.broadcasted_iota(jnp.int32, (L, SSD_CONV * ext), 0)
    s_col = lax.broadcasted_iota(jnp.int32, (L, SSD_CONV * ext), 1)
    pick = jnp.zeros((L, SSD_CONV * ext), F32)
    for k in range(SSD_CONV):
        pick = jnp.where(s_col == s_row + (k * ext + CONV_TAIL - k), 0.5, pick)
    pick = pick.astype(BF16)

    def conv_silu(rows, w_ref, b_ref, cs, dst_ref):
        n_rt = rows.shape[0] // CONV_TAIL
        rows3 = rows.reshape(n_rt, CONV_TAIL, rows.shape[1])
        taps = jnp.concatenate(
            [(rows3 * w_ref[SSD_CONV - 1 - k, :, cs][None]).reshape(rows.shape) for k in range(SSD_CONV)], axis=0)
        h = _dot(pick, taps) + 0.5 * b_ref[:, cs]
        dst_ref[:, cs] = h + h * jnp.tanh(h)

    def conv_slice(g, rows_of, dst_x_ref, dst_bc_ref):
        xc = slice(g * (w // G), (g + 1) * (w // G))
        bc = slice(g * (bcw // G), (g + 1) * (bcw // G))
        conv_silu(rows_of(xs_ref, xsn_ref, xc), cwx_ref, cbx_ref, xc, dst_x_ref)
        conv_silu(rows_of(bc_ref, bcn_ref, bc), cwbc_ref, cbbc_ref, bc, dst_bc_ref)

    def rows_first(cur_ref, nxt_ref, cs):
        return jnp.concatenate([jnp.zeros((CONV_TAIL, cs.stop - cs.start), BF16), cur_ref[0:L, cs]], axis=0)

    def rows_second(cur_ref, nxt_ref, cs):
        return cur_ref[L - CONV_TAIL:2 * L, cs]

    def rows_next(cur_ref, nxt_ref, cs):
        return jnp.concatenate([cur_ref[2 * L - CONV_TAIL:2 * L, cs], nxt_ref[:, cs]], axis=0)

    @pl.when(c == 0)
    def _():
        state_ref[...] = jnp.zeros(state_ref.shape, F32)
        for g in range(G):
            conv_slice(g, rows_first, cxa_ref, cba_ref)

    r_i = lax.broadcasted_iota(jnp.int32, (L, L), 0)
    c_i = lax.broadcasted_iota(jnp.int32, (L, L), 1)
    causal = c_i <= r_i
    tril = jnp.where(causal, 1.0, 0.0).astype(BF16)
    lane = lax.broadcasted_iota(jnp.int32, (1, LANES), 1)
    first = lane < SSD_HEAD_DIM
    n_heads = w // SSD_HEAD_DIM
    heads_per_group = gw // SSD_HEAD_DIM

    def expand(v, parts):
        packed, r = None, jnp.where(lane < n_heads, v, 0.0)
        for t in range(parts):
            p = r.astype(BF16).astype(F32)
            packed = p if t == 0 else packed + pltpu.roll(p, t * n_heads, 1)
            r = r - p
        return _dot(packed.astype(BF16), e_ref[...])

    def mix_chunk(r0, cx_ref, cb_ref, side):
        rows = slice(r0, r0 + L)
        dtx = dt_ref[rows, :] + dtb_ref[...]
        dt = jnp.maximum(dtx, 0.0) + jnp.log1p(jnp.exp(-jnp.abs(dtx)))
        a = -jnp.exp(alog_ref[...]) * LOG2E
        acs = sum(_dot(tril, p) for p in _split_bf16(dt * a, 3))
        acs_t = acs.T
        dtx_ref[...] = expand(dt, 2)
        acsx_ref[...] = expand(acs, 3)
        for g in range(G):
            side(g)
            cols = slice(g * gw, (g + 1) * gw)
            xs_g = cx_ref[:, cols]
            acs_g = acsx_ref[:, cols]
            last_g = acsx_ref[L - 1:L, cols]
            xdt_g = xs_g * dtx_ref[:, cols]
            xds_g = (xdt_g * jnp.exp2(last_g - acs_g)).astype(BF16)
            b_g = cb_ref[:, g * N:(g + 1) * N]
            c_g = cb_ref[:, (G + g) * N:(G + g + 1) * N].astype(BF16)
            cb = _dot_nt(c_g, b_g.astype(BF16))
            state_old = state_ref[:, cols]
            y_off = _dot(c_g, state_old.astype(BF16)) * jnp.exp2(acs_g)
            state_ref[:, cols] = state_old * jnp.exp2(last_g) + _dot(b_g.T.astype(BF16), xds_g)
            for pair in range(heads_per_group // 2):
                h0 = g * heads_per_group + 2 * pair
                pc = slice(pair * LANES, (pair + 1) * LANES)
                xp = xdt_g[:, pc]
                x_pair = jnp.concatenate([jnp.where(first, xp, 0.0), jnp.where(first, 0.0, xp)], axis=0).astype(BF16)
                mats = []
                for h in (h0, h0 + 1):
                    seg = acs[:, h:h + 1] - acs_t[h:h + 1, :]
                    dec = jnp.exp2(jnp.where(causal, seg, -jnp.inf))
                    mats.append((cb * dec).astype(BF16))
                y_ref[:, pc] = _dot(jnp.concatenate(mats, axis=1), x_pair)
            y = y_ref[...] + y_off + xs_g * dx_ref[:, cols]
            y = y * _silu(z_ref[rows, cols].astype(F32))
            o_ref[rows, cols] = (y * _rms_scale(y) * nw_ref[:, cols]).astype(o_ref.dtype)

    mix_chunk(0, cxa_ref, cba_ref, lambda g: conv_slice(g, rows_second, cxb_ref, cbb_ref))
    for w_ref, wb_ref in zip(cast_in, cast_out):
        wb_ref[...] = w_ref[...].astype(BF16)
    mix_chunk(L, cxb_ref, cbb_ref, lambda g: conv_slice(g, rows_next, cxa_ref, cba_ref))


def _ssd(proj3, dt3, cwx, cbx, cwbc, cbbc, dtb, alog, dx, nw, e, z_col, xs_col, bc_col, cast_ws, layer):
    b, s, _ = proj3.shape
    w = e.shape[1]
    bcw = cwbc.shape[2]
    L = SSD_CHUNK
    nc = s // L
    assert nc % 2 == 0
    steps = b * (nc // 2)
    cast_in, cast_out, cast_shapes = [], [], []
    for cw in cast_ws:
        rows, cols = cw.shape[1:]
        rb = rows // steps
        assert rows % steps == 0 and rb % (2 * SUBLANES) == 0
        cast_in.append(pl.BlockSpec((None, rb, cols), lambda bi, c: (layer, bi * (nc // 2) + c, 0)))
        cast_out.append(pl.BlockSpec((rb, cols), lambda bi, c: (bi * (nc // 2) + c, 0)))
        cast_shapes.append(jax.ShapeDtypeStruct((rows, cols), BF16))
    full = lambda shape: pl.BlockSpec(shape, lambda bi, c: (0,) * len(shape))
    nxt = lambda c: jnp.minimum(2 * c + 2, nc - 1)
    outs = pl.pallas_call(
        functools.partial(_ssd_kernel, n_cast=len(cast_ws)),
        out_shape=[jax.ShapeDtypeStruct((b, s, w), BF16)] + cast_shapes,
        grid=(b, nc // 2),
        in_specs=[
            pl.BlockSpec((None, 2 * L, w), lambda bi, c: (bi, c, z_col)),
            pl.BlockSpec((None, 2 * L, w), lambda bi, c: (bi, c, xs_col)),
            pl.BlockSpec((None, 2 * L, bcw), lambda bi, c: (bi, c, bc_col)),
            pl.BlockSpec((None, L, w), lambda bi, c: (bi, nxt(c), xs_col)),
            pl.BlockSpec((None, L, bcw), lambda bi, c: (bi, nxt(c), bc_col)),
            pl.BlockSpec((None, 2 * L, LANES), lambda bi, c: (bi, c, 0)),
            full((SSD_CONV, CONV_TAIL, w)), full((1, w)), full((SSD_CONV, CONV_TAIL, bcw)), full((1, bcw)),
            full((1, LANES)), full((1, LANES)), full((1, w)), full((1, w)), full((LANES, w)),
        ] + cast_in,
        out_specs=[pl.BlockSpec((None, 2 * L, w), lambda bi, c: (bi, c, 0))] + cast_out,
        scratch_shapes=[
            pltpu.VMEM((SSD_STATE, w), F32),
            pltpu.VMEM((L, w), F32), pltpu.VMEM((L, bcw), F32),
            pltpu.VMEM((L, w), F32), pltpu.VMEM((L, bcw), F32),
            pltpu.VMEM((L, w // SSD_GROUPS), F32),
            pltpu.VMEM((L, w), F32), pltpu.VMEM((L, w), F32),
        ],
        compiler_params=pltpu.CompilerParams(
            dimension_semantics=("parallel", "arbitrary"), vmem_limit_bytes=VMEM_LIMIT),
        name="ssd",
    )(proj3, proj3, proj3, proj3, proj3, dt3, cwx, cbx, cwbc, cbbc, dtb, alog, dx, nw, e, *cast_ws)
    return outs[0], outs[1:]


def _mix_out_kernel(a_ref, s_ref, x_ref, w_ref, xnw_ref, wq_ref, kv_ref, qw_ref, wo_ref, o_ref):
    d = X_HEAD_DIM
    ka = a_ref.shape[1]
    h1 = x_ref[...] + _dot(a_ref[...], w_ref[0])
    for c in range(1, w_ref.shape[0]):
        h1 = h1 + _dot(s_ref[:, (c - 1) * ka:c * ka], w_ref[c])
    hn = (h1 * _rms_scale(h1) * xnw_ref[...]).astype(BF16)
    q = _dot(hn, wq_ref[...])
    outs = []
    for hh in range(X_HEADS):
        qf = q[:, hh * d:(hh + 1) * d]
        qn = (qf * _rms_scale(qf) * qw_ref[...] * (d ** -0.5 * LOG2E)).astype(BF16)
        s = _dot_nt(qn, kv_ref[:, hh * d:(hh + 1) * d])
        p = jnp.exp2(s - jnp.max(s, axis=-1, keepdims=True))
        v = kv_ref[:, (X_HEADS + hh) * d:(X_HEADS + hh + 1) * d]
        o = _dot(p.astype(BF16), v) / jnp.sum(p, axis=-1, keepdims=True)
        outs.append(o.astype(BF16))
    o_ref[...] = h1 + _dot(jnp.concatenate(outs, axis=-1), wo_ref[...])


def _mix_out(a2, s2, x2, w_out, xnw, wq, kv3, qw, wo, seq, tm=512):
    m, ka = a2.shape
    ks = s2.shape[1]
    dm = x2.shape[1]
    xw = wq.shape[1]
    mlen = kv3.shape[1]
    assert seq % tm == 0 and ks % ka == 0
    once = dict(pipeline_mode=pl.Buffered(1))
    return pl.pallas_call(
        _mix_out_kernel,
        out_shape=jax.ShapeDtypeStruct((m, dm), F32),
        grid=(m // tm,),
        in_specs=[
            pl.BlockSpec((tm, ka), lambda i: (i, 0)),
            pl.BlockSpec((tm, ks), lambda i: (i, 0)),
            pl.BlockSpec((tm, dm), lambda i: (i, 0)),
            pl.BlockSpec((1 + ks // ka, ka, dm), lambda i: (0, 0, 0), **once),
            pl.BlockSpec((1, dm), lambda i: (0, 0)),
            pl.BlockSpec((dm, xw), lambda i: (0, 0), **once),
            pl.BlockSpec((None, mlen, 2 * xw), lambda i: (i // (seq // tm), 0, 0)),
            pl.BlockSpec((1, X_HEAD_DIM), lambda i: (0, 0)),
            pl.BlockSpec((xw, dm), lambda i: (0, 0), **once),
        ],
        out_specs=pl.BlockSpec((tm, dm), lambda i: (i, 0)),
        compiler_params=pltpu.CompilerParams(
            dimension_semantics=("parallel",), vmem_limit_bytes=BIG_VMEM_LIMIT),
        name="mix_out",
    )(a2, s2, x2, w_out.reshape(1 + ks // ka, ka, dm), xnw, wq, kv3, qw, wo)


def _ffn_kernel(h0_ref, hcur_ref, hnext_ref, nw_ref, wg_ref, wu_ref, wd_ref, o_ref, hn_even_ref, hn_odd_ref, *, n_chunks):
    i = pl.program_id(0)
    f = pl.program_id(1)
    rc = hcur_ref.shape[0]
    norm = lambda h: (h * _rms_scale(h) * nw_ref[...]).astype(BF16)

    @pl.when((i == 0) & (f == 0))
    def _():
        hn_even_ref[...] = norm(h0_ref[...])

    @pl.when(f == 0)
    def _():
        o_ref[...] = jnp.zeros_like(o_ref)

    def step(hn_ref, hn_next_ref):
        rows = pl.ds(pl.multiple_of(jnp.minimum(f, n_chunks - 1) * rc, rc), rc)
        hn_next_ref[rows, :] = norm(hnext_ref[...])
        o_ref[rows, :] += jnp.where(f < n_chunks, hcur_ref[...], 0.0)
        hn = hn_ref[...]
        act = (_silu(_dot(hn, wg_ref[...])) * _dot(hn, wu_ref[...])).astype(BF16)
        o_ref[...] += _dot(act, wd_ref[...])

    pl.when(i % 2 == 0)(lambda: step(hn_even_ref, hn_odd_ref))
    pl.when(i % 2 == 1)(lambda: step(hn_odd_ref, hn_even_ref))


def _ffn(h2, nw, wg, wu, wd, tm=1024, tf=512, n_chunks=8):
    m, k = h2.shape
    dff = wg.shape[1]
    rc = tm // n_chunks
    assert dff // tf >= n_chunks and tm % n_chunks == 0 and m % tm == 0
    chunk = lambda i, f: jnp.minimum(f, n_chunks - 1)
    return pl.pallas_call(
        functools.partial(_ffn_kernel, n_chunks=n_chunks),
        out_shape=jax.ShapeDtypeStruct((m, k), F32),
        grid=(m // tm, dff // tf),
        in_specs=[
            pl.BlockSpec((tm, k), lambda i, f: (0, 0), pipeline_mode=pl.Buffered(1)),
            pl.BlockSpec((rc, k), lambda i, f: (i * n_chunks + chunk(i, f), 0)),
            pl.BlockSpec((rc, k), lambda i, f: (jnp.minimum((i + 1) * n_chunks + chunk(i, f), m // rc - 1), 0)),
            pl.BlockSpec((1, k), lambda i, f: (0, 0)),
            pl.BlockSpec((k, tf), lambda i, f: (0, f)),
            pl.BlockSpec((k, tf), lambda i, f: (0, f)),
            pl.BlockSpec((tf, k), lambda i, f: (f, 0)),
        ],
        out_specs=pl.BlockSpec((tm, k), lambda i, f: (i, 0)),
        scratch_shapes=[pltpu.VMEM((tm, k), BF16), pltpu.VMEM((tm, k), BF16)],
        compiler_params=pltpu.CompilerParams(
            dimension_semantics=("arbitrary", "arbitrary"), vmem_limit_bytes=BIG_VMEM_LIMIT),
        name="ffn",
    )(h2, h2, h2, nw, wg, wu, wd)


def _pad_lanes(v, fill=0.0):
    return jnp.pad(v, ((0, 0), (0, LANES - v.shape[1])), constant_values=fill)


def kernel(x, mem, mix_norm_w, w_in, da_q_norm_w, da_k_norm_w, da_lambda_q1, da_lambda_k1, da_lambda_q2, da_lambda_k2, da_subln_w, ssd_conv_w, ssd_conv_b, ssd_dt_bias, ssd_a_log, ssd_d, ssd_norm_w, w_out, xattn_norm_w, mem_norm_w, xattn_w_q, xattn_w_kv, xattn_q_norm_w, xattn_k_norm_w, xattn_w_o, ffn_norm_w, ffn_w_gate, ffn_w_up, ffn_w_down):
    b, s, dm = x.shape
    mlen = mem.shape[1]
    depth = w_in.shape[0]
    da_w = DA_HEADS * DA_V_DIM
    ssd_w = ssd_norm_w.shape[1]
    ssd_heads = ssd_w // SSD_HEAD_DIM
    gn = SSD_GROUPS * SSD_STATE
    xw = X_HEADS * X_HEAD_DIM
    o_q, o_k, o_v, o_z = 0, da_w, 2 * da_w, 3 * da_w
    o_xs = o_z + ssd_w
    o_bc = o_xs + ssd_w
    o_dt = o_bc + 2 * gn
    z_col, xs_col = 0, 1
    bc_col = (2 * ssd_w) // (2 * gn)
    q_col = (2 * ssd_w + 2 * gn) // DA_V_DIM
    k_col = q_col + DA_HEADS
    v_col = k_col + DA_HEADS
    assert (2 * ssd_w) % (2 * gn) == 0 and ssd_heads <= LANES

    assert LANES % ssd_heads == 0 and LANES // ssd_heads >= 3
    expand = np.zeros((LANES, ssd_w), np.float32)
    for r in range(LANES):
        h = r % ssd_heads
        expand[r, h * SSD_HEAD_DIM:(h + 1) * SSD_HEAD_DIM] = 1.0
    expand = jnp.asarray(expand, BF16)

    h = x.reshape(b * s, dm)
    mem2 = mem.reshape(b * mlen, dm)
    for i in range(depth):
        lambda_init = 0.8 - 0.6 * math.exp(-0.3 * i)
        proj, dt_raw = _in_proj(h, mix_norm_w[i][None], jnp.swapaxes(w_in, 1, 2), i, o_dt, o_z)
        proj3 = proj.reshape(b, s, -1)

        lam_p = jnp.stack([da_lambda_q1[i], da_lambda_k1[i], da_lambda_q2[i], da_lambda_k2[i]])
        a_out, _ = _diff_attn(
            proj3, jnp.tile(da_q_norm_w[i], 2)[None], jnp.tile(da_k_norm_w[i], 2)[None],
            lam_p, da_subln_w[i][None], lambda_init, q_col, k_col, v_col, [], i)

        cw = jnp.broadcast_to(ssd_conv_w[i][:, None, :], (SSD_CONV, CONV_TAIL, ssd_conv_w.shape[2])).astype(BF16)
        cb = ssd_conv_b[i][None]
        s_out, (wo_b, wq_b, xwo_b, wg_b, wu_b, wd_b) = _ssd(
            proj3, dt_raw.reshape(b, s, LANES),
            cw[:, :, :ssd_w], cb[:, :ssd_w], cw[:, :, ssd_w:], cb[:, ssd_w:],
            _pad_lanes(ssd_dt_bias[i][None]), _pad_lanes(ssd_a_log[i][None]),
            jnp.repeat(ssd_d[i], SSD_HEAD_DIM)[None], ssd_norm_w[i][None], expand,
            z_col, xs_col, bc_col,
            [w_out, xattn_w_q, xattn_w_o, ffn_w_gate, ffn_w_up, ffn_w_down], i)

        xkv = _xkv_proj(mem2, mem_norm_w[i][None], xattn_w_kv, i, xattn_k_norm_w[i][None])
        h2 = _mix_out(a_out.reshape(b * s, da_w), s_out.reshape(b * s, ssd_w), h, wo_b,
                      xattn_norm_w[i][None], wq_b, xkv.reshape(b, mlen, 2 * xw), xattn_q_norm_w[i][None], xwo_b, s)

        h = _ffn(h2, ffn_norm_w[i][None], wg_b, wu_b, wd_b)
    return h.reshape(b, s, dm)
```

```python
import functools
import math

import jax
import jax.numpy as jnp
import numpy as np
from jax import lax
from jax.experimental import pallas as pl
from jax.experimental.pallas import tpu as pltpu

EPS = 1e-6
LANES = 128
SUBLANES = 8
VMEM_LIMIT = 48 * 1024 * 1024
BIG_VMEM_LIMIT = 56 * 1024 * 1024
LOG2E = math.log2(math.e)
DEN_ROWS = 16
RING_SLOTS = 3
CONV_TAIL = 16

DA_HEADS = 8
DA_HEAD_DIM = 64
DA_V_DIM = 128
SSD_HEAD_DIM = 64
SSD_GROUPS = 4
SSD_STATE = 128
SSD_CONV = 4
SSD_CHUNK = 128
X_HEADS = 4
X_HEAD_DIM = 128

F32 = jnp.float32
BF16 = jnp.bfloat16


def _rms_scale(xf, eps=EPS):
    return lax.rsqrt(jnp.mean(xf * xf, axis=-1, keepdims=True) + eps)


def _silu(x):
    h = 0.5 * x
    return h + h * jnp.tanh(h)


def _split_bf16(x, parts):
    out = []
    r = x
    for _ in range(parts):
        p = r.astype(BF16)
        out.append(p)
        r = r - p.astype(F32)
    return out


def _dot(a, b):
    return jnp.dot(a, b, preferred_element_type=F32)


def _dot_nt(a, b):
    return lax.dot_general(a, b, (((1,), (1,)), ((), ())), preferred_element_type=F32)


def _lane_group_mean(xf, avg):
    hi, lo = _split_bf16(xf, 2)
    return _dot(hi, avg) + _dot(lo, avg)


def _in_proj_kernel(x_ref, nw_ref, wt_ref, wdt_ref, o_ref, dt_ref, hn_ref):
    n, mi = pl.program_id(1), pl.program_id(2)

    @pl.when(n == 0)
    def _():
        xf = x_ref[...]
        hn = (xf * _rms_scale(xf) * nw_ref[...]).astype(BF16)
        hn_ref[mi] = hn
        wdt = wdt_ref[...].astype(BF16)
        wdt = jnp.concatenate([wdt, jnp.zeros((LANES - wdt.shape[0], wdt.shape[1]), BF16)], axis=0)
        dt_ref[...] = _dot_nt(hn, wdt)

    o_ref[...] = _dot_nt(hn_ref[mi], wt_ref[...].astype(BF16)).astype(o_ref.dtype)


def _in_proj(x2, nw, wt_all, layer, n, col_shift, tm=1024, tn=1024, m_inner=2):
    m, k = x2.shape
    n_dt = wt_all.shape[1] - n
    shift, nblk = col_shift // tn, n // tn
    assert col_shift % tn == 0 and n % tn == 0 and n % n_dt == 0 and m % (tm * m_inner) == 0
    row = lambda mo, j, mi: mo * m_inner + jnp.where(j == 0, mi, m_inner - 1)
    return pl.pallas_call(
        _in_proj_kernel,
        out_shape=(jax.ShapeDtypeStruct((m, n), BF16), jax.ShapeDtypeStruct((m, LANES), F32)),
        grid=(m // (tm * m_inner), nblk, m_inner),
        in_specs=[
            pl.BlockSpec((tm, k), lambda mo, j, mi: (row(mo, j, mi), 0)),
            pl.BlockSpec((1, k), lambda mo, j, mi: (0, 0)),
            pl.BlockSpec((None, tn, k), lambda mo, j, mi: (layer, (j + shift) % nblk, 0)),
            pl.BlockSpec((None, n_dt, k), lambda mo, j, mi: (layer, n // n_dt, 0)),
        ],
        out_specs=(
            pl.BlockSpec((tm, tn), lambda mo, j, mi: (mo * m_inner + mi, j)),
            pl.BlockSpec((tm, LANES), lambda mo, j, mi: (row(mo, j, mi), 0)),
        ),
        scratch_shapes=[pltpu.VMEM((m_inner, tm, k), BF16)],
        compiler_params=pltpu.CompilerParams(
            dimension_semantics=("parallel", "arbitrary", "arbitrary"), vmem_limit_bytes=BIG_VMEM_LIMIT),
        name="in_proj",
    )(x2, nw, wt_all, wt_all)


def _xkv_kernel(x_ref, nw_ref, w_ref, kw_ref, o_ref):
    d = X_HEAD_DIM
    xf = x_ref[...]
    hn = (xf * _rms_scale(xf) * nw_ref[...]).astype(BF16)
    kv = _dot(hn, w_ref[...].astype(BF16))
    kw = X_HEADS * d
    for hh in range(X_HEADS):
        kf = kv[:, hh * d:(hh + 1) * d]
        o_ref[:, hh * d:(hh + 1) * d] = (kf * _rms_scale(kf) * kw_ref[...]).astype(o_ref.dtype)
    o_ref[:, kw:] = kv[:, kw:].astype(o_ref.dtype)


def _xkv_proj(x2, nw, w_all, layer, kw, tm=512):
    m, k = x2.shape
    n = w_all.shape[2]
    return pl.pallas_call(
        _xkv_kernel,
        out_shape=jax.ShapeDtypeStruct((m, n), BF16),
        grid=(m // tm,),
        in_specs=[
            pl.BlockSpec((tm, k), lambda i: (i, 0)),
            pl.BlockSpec((1, k), lambda i: (0, 0)),
            pl.BlockSpec((None, k, n), lambda i: (layer, 0, 0)),
            pl.BlockSpec((1, X_HEAD_DIM), lambda i: (0, 0)),
        ],
        out_specs=pl.BlockSpec((tm, n), lambda i: (i, 0)),
        compiler_params=pltpu.CompilerParams(
            dimension_semantics=("parallel",), vmem_limit_bytes=VMEM_LIMIT),
        name="xkv_proj",
    )(x2, nw, w_all, kw)


def _diff_attn_kernel(q_ref, k_ref, v_ref, qw_ref, kw_ref, lam_ref, sw_ref, *rest,
                      tq, hb, lambda_init, n_cast):
    cast_in, o_ref, cast_out = rest[:n_cast], rest[n_cast], rest[n_cast + 1:2 * n_cast + 1]
    kn_ref, vt_ref, qcat_ref, m_ref, acc_ref, s_ref = rest[2 * n_cast + 1:]
    for w_ref, wb_ref in zip(cast_in, cast_out):
        wb_ref[...] = w_ref[...].astype(BF16)
    i = pl.program_id(2)
    n_tiles = vt_ref.shape[1]
    d = DA_V_DIM
    lane = lax.broadcasted_iota(jnp.int32, (1, d), 1)
    first = lane < DA_HEAD_DIM
    ones_row = lax.broadcasted_iota(jnp.int32, (DEN_ROWS, tq), 0) == 0

    half_r = lax.broadcasted_iota(jnp.int32, (d, d), 0) < DA_HEAD_DIM
    half_c = lax.broadcasted_iota(jnp.int32, (d, d), 1) < DA_HEAD_DIM
    sub_avg = jnp.where(half_r == half_c, 1.0 / DA_HEAD_DIM, 0.0).astype(BF16)

    def sub_head_norm(xf):
        return xf * lax.rsqrt(_lane_group_mean(xf * xf, sub_avg) + EPS)

    @pl.when(i == 0)
    def _():
        for hh in range(hb):
            hs = slice(hh * d, (hh + 1) * d)
            kn_ref[hh] = (sub_head_norm(k_ref[:, hs].astype(F32)) * kw_ref[...]).astype(BF16)
            for t in range(n_tiles):
                vt_ref[hh, t, :d] = v_ref[t * tq:(t + 1) * tq, hs].T
                vt_ref[hh, t, d:] = jnp.where(ones_row, 1.0, 0.0).astype(BF16)

    for hh in range(hb):
        qn = sub_head_norm(q_ref[:, hh * d:(hh + 1) * d].astype(F32)) * qw_ref[...] * (DA_HEAD_DIM ** -0.5 * LOG2E)
        qcat_ref[hh] = jnp.concatenate(
            [jnp.where(first, qn, 0.0).T, jnp.where(first, 0.0, qn).T], axis=1).astype(BF16)
    m_ref[...] = jnp.full(m_ref.shape, -jnp.inf, F32)
    acc_ref[...] = jnp.zeros(acc_ref.shape, F32)

    def scores(j, hh):
        start = pl.multiple_of(j * tq, tq)
        return _dot(kn_ref[hh, pl.ds(start, tq), :], qcat_ref[hh])

    lam_p = lam_ref[...]
    lam = (jnp.exp(jnp.sum(lam_p[0:1] * lam_p[1:2], axis=-1, keepdims=True))
           - jnp.exp(jnp.sum(lam_p[2:3] * lam_p[3:4], axis=-1, keepdims=True)) + lambda_init)

    def tile(j, cur, last):
        if last:
            key = lax.broadcasted_iota(jnp.int32, (tq, 2 * tq), 0)
            qry = lax.broadcasted_iota(jnp.int32, (tq, 2 * tq), 1)
            causal = key <= jnp.where(qry >= tq, qry - tq, qry)
        for hh in range(hb):
            if not last:
                s_ref[1 - cur, hh] = scores(j + 1, hh)
            s = s_ref[cur, hh]
            if last:
                s = jnp.where(causal, s, -jnp.inf)
            m_prev = m_ref[hh]
            m_new = jnp.maximum(m_prev, jnp.max(s, axis=0, keepdims=True))
            alpha = jnp.exp2(m_prev - m_new)
            p = jnp.exp2(s - m_new)
            acc = alpha * acc_ref[hh] + _dot(vt_ref[hh, j], p.astype(BF16))
            if not last:
                acc_ref[hh] = acc
                m_ref[hh] = m_new
            else:
                ot = acc[:d] / acc[d:d + 1]
                o = (ot[:, :tq] - lam * ot[:, tq:]).T
                o = o * _rms_scale(o) * sw_ref[...] * (1.0 - lambda_init)
                o_ref[:, hh * d:(hh + 1) * d] = o.astype(o_ref.dtype)

    for hh in range(hb):
        s_ref[0, hh] = scores(0, hh)

    def body(jj, carry):
        tile(2 * jj, 0, False)
        tile(2 * jj + 1, 1, False)
        return carry

    lax.fori_loop(0, i // 2, body, 0)

    @pl.when(i % 2 == 0)
    def _():
        tile(i, 0, True)

    @pl.when(i % 2 == 1)
    def _():
        tile(i - 1, 0, False)
        tile(i, 1, True)


def _diff_attn(proj3, qw2, kw2, lam_p, sw, lambda_init, q_col, k_col, v_col, cast_ws, layer, tq=256, hb=8):
    b, s, _ = proj3.shape
    d = DA_V_DIM
    nh, nq = DA_HEADS // hb, s // tq
    steps = b * nh * nq
    kern = functools.partial(_diff_attn_kernel, tq=tq, hb=hb, lambda_init=lambda_init, n_cast=len(cast_ws))
    vec = lambda shape: pl.BlockSpec(shape, lambda bi, h, i: (0, 0))
    cast_in, cast_out, cast_shapes = [], [], []
    for w in cast_ws:
        rows, cols = w.shape[1:]
        rb = rows // steps
        assert rows % steps == 0 and rb % (2 * SUBLANES) == 0
        cast_in.append(pl.BlockSpec((None, rb, cols), lambda bi, h, i: (layer, (bi * nh + h) * nq + i, 0)))
        cast_out.append(pl.BlockSpec((rb, cols), lambda bi, h, i: ((bi * nh + h) * nq + i, 0)))
        cast_shapes.append(jax.ShapeDtypeStruct((rows, cols), BF16))
    outs = pl.pallas_call(
        kern,
        out_shape=[jax.ShapeDtypeStruct((b, s, DA_HEADS * d), BF16)] + cast_shapes,
        grid=(b, nh, nq),
        in_specs=[
            pl.BlockSpec((None, tq, hb * d), lambda bi, h, i: (bi, i, q_col // hb + h)),
            pl.BlockSpec((None, s, hb * d), lambda bi, h, i: (bi, 0, k_col // hb + h)),
            pl.BlockSpec((None, s, hb * d), lambda bi, h, i: (bi, 0, v_col // hb + h)),
            vec((1, d)), vec((1, d)), vec((4, DA_HEAD_DIM)), vec((1, d)),
        ] + cast_in,
        out_specs=[pl.BlockSpec((None, tq, hb * d), lambda bi, h, i: (bi, i, h))] + cast_out,
        scratch_shapes=[
            pltpu.VMEM((hb, s, d), BF16),
            pltpu.VMEM((hb, s // tq, d + DEN_ROWS, tq), BF16),
            pltpu.VMEM((hb, d, 2 * tq), BF16),
            pltpu.VMEM((hb, 1, 2 * tq), F32),
            pltpu.VMEM((hb, d + DEN_ROWS, 2 * tq), F32),
            pltpu.VMEM((2, hb, tq, 2 * tq), F32),
        ],
        compiler_params=pltpu.CompilerParams(
            dimension_semantics=("parallel", "parallel", "arbitrary"), vmem_limit_bytes=BIG_VMEM_LIMIT),
        name="diff_attn",
    )(proj3, proj3, proj3, qw2, kw2, lam_p, sw, *cast_ws)
    return outs[0], outs[1:]


def _ssd_kernel(z_ref, xs_ref, bc_ref, xsn_ref, bcn_ref, dt_ref, cwx_ref, cbx_ref, cwbc_ref, cbbc_ref,
                dtb_ref, alog_ref, dx_ref, nw_ref, e_ref, *rest, n_cast, layer):
    cast_in, o_ref, cast_out = rest[:n_cast], rest[n_cast], rest[n_cast + 1:2 * n_cast + 1]
    state_ref, cxa_ref, cba_ref, cxb_ref, cbb_ref, y_ref, dtx_ref, acsx_ref = rest[2 * n_cast + 1:2 * n_cast + 9]
    ring, sem = rest[2 * n_cast + 9:-1], rest[-1]
    L, N, G = SSD_CHUNK, SSD_STATE, SSD_GROUPS

    step = pl.program_id(0) * pl.num_programs(1) + pl.program_id(1)
    n_steps = pl.num_programs(0) * pl.num_programs(1)

    def weight_copy(k, s):
        rb = ring[k].shape[1]
        return pltpu.make_async_copy(
            cast_in[k].at[layer, pl.ds(s * rb, rb), :], ring[k].at[s % RING_SLOTS], sem.at[k, s % RING_SLOTS])

    @pl.when(step == 0)
    def _():
        for k in range(n_cast):
            for s in range(RING_SLOTS - 1):
                weight_copy(k, s).start()

    @pl.when(step + RING_SLOTS - 1 < n_steps)
    def _():
        for k in range(n_cast):
            weight_copy(k, step + RING_SLOTS - 1).start()

    for k in range(n_cast):
        weight_copy(k, step).wait()

    w = o_ref.shape[-1]
    bcw = bc_ref.shape[-1]
    gw = w // G
    c = pl.program_id(1)

    ext = CONV_TAIL + L
    s_row = lax.broadcasted_iota(jnp.int32, (L, SSD_CONV * ext), 0)
    s_col = lax.broadcasted_iota(jnp.int32, (L, SSD_CONV * ext), 1)
    pick = jnp.zeros((L, SSD_CONV * ext), F32)
    for k in range(SSD_CONV):
        pick = jnp.where(s_col == s_row + (k * ext + CONV_TAIL - k), 0.5, pick)
    pick = pick.astype(BF16)

    def conv_silu(rows, w_ref, b_ref, cs, dst_ref):
        n_rt = rows.shape[0] // CONV_TAIL
        rows3 = rows.reshape(n_rt, CONV_TAIL, rows.shape[1])
        taps = jnp.concatenate(
            [(rows3 * w_ref[SSD_CONV - 1 - k, :, cs][None]).reshape(rows.shape) for k in range(SSD_CONV)], axis=0)
        h = _dot(pick, taps) + 0.5 * b_ref[:, cs]
        dst_ref[:, cs] = h + h * jnp.tanh(h)

    def conv_slice(g, rows_of, dst_x_ref, dst_bc_ref):
        xc = slice(g * (w // G), (g + 1) * (w // G))
        bc = slice(g * (bcw // G), (g + 1) * (bcw // G))
        conv_silu(rows_of(xs_ref, xsn_ref, xc), cwx_ref, cbx_ref, xc, dst_x_ref)
        conv_silu(rows_of(bc_ref, bcn_ref, bc), cwbc_ref, cbbc_ref, bc, dst_bc_ref)

    def rows_first(cur_ref, nxt_ref, cs):
        return jnp.concatenate([jnp.zeros((CONV_TAIL, cs.stop - cs.start), BF16), cur_ref[0:L, cs]], axis=0)

    def rows_second(cur_ref, nxt_ref, cs):
        return cur_ref[L - CONV_TAIL:2 * L, cs]

    def rows_next(cur_ref, nxt_ref, cs):
        return jnp.concatenate([cur_ref[2 * L - CONV_TAIL:2 * L, cs], nxt_ref[:, cs]], axis=0)

    @pl.when(c == 0)
    def _():
        state_ref[...] = jnp.zeros(state_ref.shape, F32)
        for g in range(G):
            conv_slice(g, rows_first, cxa_ref, cba_ref)

    r_i = lax.broadcasted_iota(jnp.int32, (L, L), 0)
    c_i = lax.broadcasted_iota(jnp.int32, (L, L), 1)
    causal = c_i <= r_i
    tril = jnp.where(causal, 1.0, 0.0).astype(BF16)
    lane = lax.broadcasted_iota(jnp.int32, (1, LANES), 1)
    first = lane < SSD_HEAD_DIM
    n_heads = w // SSD_HEAD_DIM
    heads_per_group = gw // SSD_HEAD_DIM

    def expand(v, parts):
        packed, r = None, jnp.where(lane < n_heads, v, 0.0)
        for t in range(parts):
            p = r.astype(BF16).astype(F32)
            packed = p if t == 0 else packed + pltpu.roll(p, t * n_heads, 1)
            r = r - p
        return _dot(packed.astype(BF16), e_ref[...])

    def mix_chunk(r0, cx_ref, cb_ref, side):
        rows = slice(r0, r0 + L)
        dtx = dt_ref[rows, :] + dtb_ref[...]
        dt = jnp.maximum(dtx, 0.0) + jnp.log1p(jnp.exp(-jnp.abs(dtx)))
        a = -jnp.exp(alog_ref[...]) * LOG2E
        acs = sum(_dot(tril, p) for p in _split_bf16(dt * a, 3))
        acs_t = acs.T
        dtx_ref[...] = expand(dt, 2)
        acsx_ref[...] = expand(acs, 3)
        for g in range(G):
            side(g)
            cols = slice(g * gw, (g + 1) * gw)
            xs_g = cx_ref[:, cols]
            acs_g = acsx_ref[:, cols]
            last_g = acsx_ref[L - 1:L, cols]
            xdt_g = xs_g * dtx_ref[:, cols]
            xds_g = (xdt_g * jnp.exp2(last_g - acs_g)).astype(BF16)
            b_g = cb_ref[:, g * N:(g + 1) * N]
            c_g = cb_ref[:, (G + g) * N:(G + g + 1) * N].astype(BF16)
            cb = _dot_nt(c_g, b_g.astype(BF16))
            state_old = state_ref[:, cols]
            y_off = _dot(c_g, state_old.astype(BF16)) * jnp.exp2(acs_g)
            state_ref[:, cols] = state_old * jnp.exp2(last_g) + _dot(b_g.T.astype(BF16), xds_g)
            for pair in range(heads_per_group // 2):
                h0 = g * heads_per_group + 2 * pair
                pc = slice(pair * LANES, (pair + 1) * LANES)
                xp = xdt_g[:, pc]
                x_pair = jnp.concatenate([jnp.where(first, xp, 0.0), jnp.where(first, 0.0, xp)], axis=0).astype(BF16)
                mats = []
                for h in (h0, h0 + 1):
                    seg = acs[:, h:h + 1] - acs_t[h:h + 1, :]
                    dec = jnp.exp2(jnp.where(causal, seg, -jnp.inf))
                    mats.append((cb * dec).astype(BF16))
                y_ref[:, pc] = _dot(jnp.concatenate(mats, axis=1), x_pair)
            y = y_ref[...] + y_off + xs_g * dx_ref[:, cols]
            y = y * _silu(z_ref[rows, cols].astype(F32))
            o_ref[rows, cols] = (y * _rms_scale(y) * nw_ref[:, cols]).astype(o_ref.dtype)

    mix_chunk(0, cxa_ref, cba_ref, lambda g: conv_slice(g, rows_second, cxb_ref, cbb_ref))
    for k, wb_ref in enumerate(cast_out):
        wb_ref[...] = ring[k][step % RING_SLOTS].astype(BF16)
    mix_chunk(L, cxb_ref, cbb_ref, lambda g: conv_slice(g, rows_next, cxa_ref, cba_ref))


def _ssd(proj3, dt3, cwx, cbx, cwbc, cbbc, dtb, alog, dx, nw, e, z_col, xs_col, bc_col, cast_ws, layer):
    b, s, _ = proj3.shape
    w = e.shape[1]
    bcw = cwbc.shape[2]
    L = SSD_CHUNK
    nc = s // L
    assert nc % 2 == 0
    steps = b * (nc // 2)
    cast_in, cast_out, cast_shapes, ring = [], [], [], []
    for cw in cast_ws:
        rows, cols = cw.shape[1:]
        rb = rows // steps
        assert rows % steps == 0 and rb % (2 * SUBLANES) == 0 and steps >= RING_SLOTS
        cast_in.append(pl.BlockSpec(memory_space=pl.ANY))
        ring.append(pltpu.VMEM((RING_SLOTS, rb, cols), F32))
        cast_out.append(pl.BlockSpec((rb, cols), lambda bi, c: (bi * (nc // 2) + c, 0)))
        cast_shapes.append(jax.ShapeDtypeStruct((rows, cols), BF16))
    full = lambda shape: pl.BlockSpec(shape, lambda bi, c: (0,) * len(shape))
    nxt = lambda c: jnp.minimum(2 * c + 2, nc - 1)
    outs = pl.pallas_call(
        functools.partial(_ssd_kernel, n_cast=len(cast_ws), layer=layer),
        out_shape=[jax.ShapeDtypeStruct((b, s, w), BF16)] + cast_shapes,
        grid=(b, nc // 2),
        in_specs=[
            pl.BlockSpec((None, 2 * L, w), lambda bi, c: (bi, c, z_col)),
            pl.BlockSpec((None, 2 * L, w), lambda bi, c: (bi, c, xs_col)),
            pl.BlockSpec((None, 2 * L, bcw), lambda bi, c: (bi, c, bc_col)),
            pl.BlockSpec((None, L, w), lambda bi, c: (bi, nxt(c), xs_col)),
            pl.BlockSpec((None, L, bcw), lambda bi, c: (bi, nxt(c), bc_col)),
            pl.BlockSpec((None, 2 * L, LANES), lambda bi, c: (bi, c, 0)),
            full((SSD_CONV, CONV_TAIL, w)), full((1, w)), full((SSD_CONV, CONV_TAIL, bcw)), full((1, bcw)),
            full((1, LANES)), full((1, LANES)), full((1, w)), full((1, w)), full((LANES, w)),
        ] + cast_in,
        out_specs=[pl.BlockSpec((None, 2 * L, w), lambda bi, c: (bi, c, 0))] + cast_out,
        scratch_shapes=[
            pltpu.VMEM((SSD_STATE, w), F32),
            pltpu.VMEM((L, w), F32), pltpu.VMEM((L, bcw), F32),
            pltpu.VMEM((L, w), F32), pltpu.VMEM((L, bcw), F32),
            pltpu.VMEM((L, w // SSD_GROUPS), F32),
            pltpu.VMEM((L, w), F32), pltpu.VMEM((L, w), F32),
        ] + ring + [pltpu.SemaphoreType.DMA((len(cast_ws), RING_SLOTS))],
        compiler_params=pltpu.CompilerParams(
            dimension_semantics=("arbitrary", "arbitrary"), vmem_limit_bytes=BIG_VMEM_LIMIT),
        name="ssd",
    )(proj3, proj3, proj3, proj3, proj3, dt3, cwx, cbx, cwbc, cbbc, dtb, alog, dx, nw, e, *cast_ws)
    return outs[0], outs[1:]


def _mix_out_kernel(a_ref, s_ref, x_ref, w_ref, xnw_ref, wq_ref, kv_ref, qw_ref, wo_ref, o_ref):
    d = X_HEAD_DIM
    ka = a_ref.shape[1]
    h1 = x_ref[...] + _dot(a_ref[...], w_ref[0])
    for c in range(1, w_ref.shape[0]):
        h1 = h1 + _dot(s_ref[:, (c - 1) * ka:c * ka], w_ref[c])
    hn = (h1 * _rms_scale(h1) * xnw_ref[...]).astype(BF16)
    q = _dot(hn, wq_ref[...])
    outs = []
    for hh in range(X_HEADS):
        qf = q[:, hh * d:(hh + 1) * d]
        qn = (qf * _rms_scale(qf) * qw_ref[...] * (d ** -0.5 * LOG2E)).astype(BF16)
        s = _dot_nt(qn, kv_ref[:, hh * d:(hh + 1) * d])
        p = jnp.exp2(s - jnp.max(s, axis=-1, keepdims=True))
        v = kv_ref[:, (X_HEADS + hh) * d:(X_HEADS + hh + 1) * d]
        o = _dot(p.astype(BF16), v) / jnp.sum(p, axis=-1, keepdims=True)
        outs.append(o.astype(BF16))
    o_ref[...] = h1 + _dot(jnp.concatenate(outs, axis=-1), wo_ref[...])


def _mix_out(a2, s2, x2, w_out, xnw, wq, kv3, qw, wo, seq, tm=512):
    m, ka = a2.shape
    ks = s2.shape[1]
    dm = x2.shape[1]
    xw = wq.shape[1]
    mlen = kv3.shape[1]
    assert seq % tm == 0 and ks % ka == 0
    once = dict(pipeline_mode=pl.Buffered(1))
    return pl.pallas_call(
        _mix_out_kernel,
        out_shape=jax.ShapeDtypeStruct((m, dm), F32),
        grid=(m // tm,),
        in_specs=[
            pl.BlockSpec((tm, ka), lambda i: (i, 0)),
            pl.BlockSpec((tm, ks), lambda i: (i, 0)),
            pl.BlockSpec((tm, dm), lambda i: (i, 0)),
            pl.BlockSpec((1 + ks // ka, ka, dm), lambda i: (0, 0, 0), **once),
            pl.BlockSpec((1, dm), lambda i: (0, 0)),
            pl.BlockSpec((dm, xw), lambda i: (0, 0), **once),
            pl.BlockSpec((None, mlen, 2 * xw), lambda i: (i // (seq // tm), 0, 0)),
            pl.BlockSpec((1, X_HEAD_DIM), lambda i: (0, 0)),
            pl.BlockSpec((xw, dm), lambda i: (0, 0), **once),
        ],
        out_specs=pl.BlockSpec((tm, dm), lambda i: (i, 0)),
        compiler_params=pltpu.CompilerParams(
            dimension_semantics=("parallel",), vmem_limit_bytes=BIG_VMEM_LIMIT),
        name="mix_out",
    )(a2, s2, x2, w_out.reshape(1 + ks // ka, ka, dm), xnw, wq, kv3, qw, wo)


def _ffn_kernel(h0_ref, hcur_ref, hnext_ref, nw_ref, wg_ref, wu_ref, wd_ref, o_ref, hn_even_ref, hn_odd_ref, *, n_chunks):
    i = pl.program_id(0)
    f = pl.program_id(1)
    rc = hcur_ref.shape[0]
    norm = lambda h: (h * _rms_scale(h) * nw_ref[...]).astype(BF16)

    @pl.when((i == 0) & (f == 0))
    def _():
        hn_even_ref[...] = norm(h0_ref[...])

    @pl.when(f == 0)
    def _():
        o_ref[...] = jnp.zeros_like(o_ref)

    def step(hn_ref, hn_next_ref):
        rows = pl.ds(pl.multiple_of(jnp.minimum(f, n_chunks - 1) * rc, rc), rc)
        hn_next_ref[rows, :] = norm(hnext_ref[...])
        for c in range(n_chunks):
            o_ref[c * rc:(c + 1) * rc, :] += jnp.where(f == c, hcur_ref[...], 0.0)
        hn = hn_ref[...]
        act = (_silu(_dot(hn, wg_ref[...])) * _dot(hn, wu_ref[...])).astype(BF16)
        o_ref[...] += _dot(act, wd_ref[...])

    pl.when(i % 2 == 0)(lambda: step(hn_even_ref, hn_odd_ref))
    pl.when(i % 2 == 1)(lambda: step(hn_odd_ref, hn_even_ref))


def _ffn(h2, nw, wg, wu, wd, tm=1024, tf=512, n_chunks=8):
    m, k = h2.shape
    dff = wg.shape[1]
    rc = tm // n_chunks
    assert dff // tf >= n_chunks and tm % n_chunks == 0 and m % tm == 0
    chunk = lambda i, f: jnp.minimum(f, n_chunks - 1)
    return pl.pallas_call(
        functools.partial(_ffn_kernel, n_chunks=n_chunks),
        out_shape=jax.ShapeDtypeStruct((m, k), F32),
        grid=(m // tm, dff // tf),
        in_specs=[
            pl.BlockSpec((tm, k), lambda i, f: (0, 0), pipeline_mode=pl.Buffered(1)),
            pl.BlockSpec((rc, k), lambda i, f: (i * n_chunks + chunk(i, f), 0)),
            pl.BlockSpec((rc, k), lambda i, f: (jnp.minimum((i + 1) * n_chunks + chunk(i, f), m // rc - 1), 0)),
            pl.BlockSpec((1, k), lambda i, f: (0, 0)),
            pl.BlockSpec((k, tf), lambda i, f: (0, f)),
            pl.BlockSpec((k, tf), lambda i, f: (0, f)),
            pl.BlockSpec((tf, k), lambda i, f: (f, 0)),
        ],
        out_specs=pl.BlockSpec((tm, k), lambda i, f: (i, 0)),
        scratch_shapes=[pltpu.VMEM((tm, k), BF16), pltpu.VMEM((tm, k), BF16)],
        compiler_params=pltpu.CompilerParams(
            dimension_semantics=("arbitrary", "arbitrary"), vmem_limit_bytes=BIG_VMEM_LIMIT),
        name="ffn",
    )(h2, h2, h2, nw, wg, wu, wd)


def _pad_lanes(v, fill=0.0):
    return jnp.pad(v, ((0, 0), (0, LANES - v.shape[1])), constant_values=fill)


def kernel(x, mem, mix_norm_w, w_in, da_q_norm_w, da_k_norm_w, da_lambda_q1, da_lambda_k1, da_lambda_q2, da_lambda_k2, da_subln_w, ssd_conv_w, ssd_conv_b, ssd_dt_bias, ssd_a_log, ssd_d, ssd_norm_w, w_out, xattn_norm_w, mem_norm_w, xattn_w_q, xattn_w_kv, xattn_q_norm_w, xattn_k_norm_w, xattn_w_o, ffn_norm_w, ffn_w_gate, ffn_w_up, ffn_w_down):
    b, s, dm = x.shape
    mlen = mem.shape[1]
    depth = w_in.shape[0]
    da_w = DA_HEADS * DA_V_DIM
    ssd_w = ssd_norm_w.shape[1]
    ssd_heads = ssd_w // SSD_HEAD_DIM
    gn = SSD_GROUPS * SSD_STATE
    xw = X_HEADS * X_HEAD_DIM
    o_q, o_k, o_v, o_z = 0, da_w, 2 * da_w, 3 * da_w
    o_xs = o_z + ssd_w
    o_bc = o_xs + ssd_w
    o_dt = o_bc + 2 * gn
    z_col, xs_col = 0, 1
    bc_col = (2 * ssd_w) // (2 * gn)
    q_col = (2 * ssd_w + 2 * gn) // DA_V_DIM
    k_col = q_col + DA_HEADS
    v_col = k_col + DA_HEADS
    assert (2 * ssd_w) % (2 * gn) == 0 and ssd_heads <= LANES

    assert LANES % ssd_heads == 0 and LANES // ssd_heads >= 3
    expand = np.zeros((LANES, ssd_w), np.float32)
    for r in range(LANES):
        h = r % ssd_heads
        expand[r, h * SSD_HEAD_DIM:(h + 1) * SSD_HEAD_DIM] = 1.0
    expand = jnp.asarray(expand, BF16)

    h = x.reshape(b * s, dm)
    mem2 = mem.reshape(b * mlen, dm)
    for i in range(depth):
        lambda_init = 0.8 - 0.6 * math.exp(-0.3 * i)
        proj, dt_raw = _in_proj(h, mix_norm_w[i][None], jnp.swapaxes(w_in, 1, 2), i, o_dt, o_z)
        proj3 = proj.reshape(b, s, -1)

        lam_p = jnp.stack([da_lambda_q1[i], da_lambda_k1[i], da_lambda_q2[i], da_lambda_k2[i]])
        a_out, _ = _diff_attn(
            proj3, jnp.tile(da_q_norm_w[i], 2)[None], jnp.tile(da_k_norm_w[i], 2)[None],
            lam_p, da_subln_w[i][None], lambda_init, q_col, k_col, v_col, [], i)

        cw = jnp.broadcast_to(ssd_conv_w[i][:, None, :], (SSD_CONV, CONV_TAIL, ssd_conv_w.shape[2])).astype(BF16)
        cb = ssd_conv_b[i][None]
        s_out, (wo_b, wq_b, xwo_b, wg_b, wu_b, wd_b) = _ssd(
            proj3, dt_raw.reshape(b, s, LANES),
            cw[:, :, :ssd_w], cb[:, :ssd_w], cw[:, :, ssd_w:], cb[:, ssd_w:],
            _pad_lanes(ssd_dt_bias[i][None]), _pad_lanes(ssd_a_log[i][None]),
            jnp.repeat(ssd_d[i], SSD_HEAD_DIM)[None], ssd_norm_w[i][None], expand,
            z_col, xs_col, bc_col,
            [w_out, xattn_w_q, xattn_w_o, ffn_w_gate, ffn_w_up, ffn_w_down], i)

        xkv = _xkv_proj(mem2, mem_norm_w[i][None], xattn_w_kv, i, xattn_k_norm_w[i][None])
        h2 = _mix_out(a_out.reshape(b * s, da_w), s_out.reshape(b * s, ssd_w), h, wo_b,
                      xattn_norm_w[i][None], wq_b, xkv.reshape(b, mlen, 2 * xw), xattn_q_norm_w[i][None], xwo_b, s)

        h = _ffn(h2, ffn_norm_w[i][None], wg_b, wu_b, wd_b)
    return h.reshape(b, s, dm)
```

```python
import functools
import math

import jax
import jax.numpy as jnp
import numpy as np
from jax import lax
from jax.experimental import pallas as pl
from jax.experimental.pallas import tpu as pltpu

EPS = 1e-6
LANES = 128
SUBLANES = 8
VMEM_LIMIT = 48 * 1024 * 1024
BIG_VMEM_LIMIT = 56 * 1024 * 1024
LOG2E = math.log2(math.e)
DEN_ROWS = 16
CONV_TAIL = 16

DA_HEADS = 8
DA_HEAD_DIM = 64
DA_V_DIM = 128
SSD_HEAD_DIM = 64
SSD_GROUPS = 4
SSD_STATE = 128
SSD_CONV = 4
SSD_CHUNK = 128
X_HEADS = 4
X_HEAD_DIM = 128

F32 = jnp.float32
BF16 = jnp.bfloat16


def _rms_scale(xf, eps=EPS):
    return lax.rsqrt(jnp.mean(xf * xf, axis=-1, keepdims=True) + eps)


def _silu(x):
    h = 0.5 * x
    return h + h * jnp.tanh(h)


def _split_bf16(x, parts):
    out = []
    r = x
    for _ in range(parts):
        p = r.astype(BF16)
        out.append(p)
        r = r - p.astype(F32)
    return out


def _dot(a, b):
    return jnp.dot(a, b, preferred_element_type=F32)


def _dot_nt(a, b):
    return lax.dot_general(a, b, (((1,), (1,)), ((), ())), preferred_element_type=F32)


def _lane_group_mean(xf, avg):
    hi, lo = _split_bf16(xf, 2)
    return _dot(hi, avg) + _dot(lo, avg)


def _in_proj_kernel(x_ref, nw_ref, wt_ref, wdt_ref, o_ref, dt_ref, hn_ref):
    n, mi = pl.program_id(1), pl.program_id(2)

    @pl.when(n == 0)
    def _():
        xf = x_ref[...]
        hn = (xf * _rms_scale(xf) * nw_ref[...]).astype(BF16)
        hn_ref[mi] = hn
        wdt = wdt_ref[...].astype(BF16)
        wdt = jnp.concatenate([wdt, jnp.zeros((LANES - wdt.shape[0], wdt.shape[1]), BF16)], axis=0)
        dt_ref[...] = _dot_nt(hn, wdt)

    o_ref[...] = _dot_nt(hn_ref[mi], wt_ref[...].astype(BF16)).astype(o_ref.dtype)


def _in_proj(x2, nw, wt_all, layer, n, col_shift, tm=1024, tn=1024, m_inner=2):
    m, k = x2.shape
    n_dt = wt_all.shape[1] - n
    shift, nblk = col_shift // tn, n // tn
    assert col_shift % tn == 0 and n % tn == 0 and n % n_dt == 0 and m % (tm * m_inner) == 0
    row = lambda mo, j, mi: mo * m_inner + jnp.where(j == 0, mi, m_inner - 1)
    return pl.pallas_call(
        _in_proj_kernel,
        out_shape=(jax.ShapeDtypeStruct((m, n), BF16), jax.ShapeDtypeStruct((m, LANES), F32)),
        grid=(m // (tm * m_inner), nblk, m_inner),
        in_specs=[
            pl.BlockSpec((tm, k), lambda mo, j, mi: (row(mo, j, mi), 0)),
            pl.BlockSpec((1, k), lambda mo, j, mi: (0, 0)),
            pl.BlockSpec((None, tn, k), lambda mo, j, mi: (layer, (j + shift) % nblk, 0)),
            pl.BlockSpec((None, n_dt, k), lambda mo, j, mi: (layer, n // n_dt, 0)),
        ],
        out_specs=(
            pl.BlockSpec((tm, tn), lambda mo, j, mi: (mo * m_inner + mi, j)),
            pl.BlockSpec((tm, LANES), lambda mo, j, mi: (row(mo, j, mi), 0)),
        ),
        scratch_shapes=[pltpu.VMEM((m_inner, tm, k), BF16)],
        compiler_params=pltpu.CompilerParams(
            dimension_semantics=("parallel", "arbitrary", "arbitrary"), vmem_limit_bytes=BIG_VMEM_LIMIT),
        name="in_proj",
    )(x2, nw, wt_all, wt_all)


def _xkv_kernel(x_ref, nw_ref, w_ref, kw_ref, o_ref):
    d = X_HEAD_DIM
    xf = x_ref[...]
    hn = (xf * _rms_scale(xf) * nw_ref[...]).astype(BF16)
    kv = _dot(hn, w_ref[...].astype(BF16))
    kw = X_HEADS * d
    for hh in range(X_HEADS):
        kf = kv[:, hh * d:(hh + 1) * d]
        o_ref[:, hh * d:(hh + 1) * d] = (kf * _rms_scale(kf) * kw_ref[...]).astype(o_ref.dtype)
    o_ref[:, kw:] = kv[:, kw:].astype(o_ref.dtype)


def _xkv_proj(x2, nw, w_all, layer, kw, tm=512):
    m, k = x2.shape
    n = w_all.shape[2]
    return pl.pallas_call(
        _xkv_kernel,
        out_shape=jax.ShapeDtypeStruct((m, n), BF16),
        grid=(m // tm,),
        in_specs=[
            pl.BlockSpec((tm, k), lambda i: (i, 0)),
            pl.BlockSpec((1, k), lambda i: (0, 0)),
            pl.BlockSpec((None, k, n), lambda i: (layer, 0, 0)),
            pl.BlockSpec((1, X_HEAD_DIM), lambda i: (0, 0)),
        ],
        out_specs=pl.BlockSpec((tm, n), lambda i: (i, 0)),
        compiler_params=pltpu.CompilerParams(
            dimension_semantics=("parallel",), vmem_limit_bytes=VMEM_LIMIT),
        name="xkv_proj",
    )(x2, nw, w_all, kw)


def _diff_attn_kernel(q_ref, k_ref, v_ref, qw_ref, kw_ref, lam_ref, sw_ref, *rest,
                      tq, hb, lambda_init, n_cast):
    cast_in, o_ref, cast_out = rest[:n_cast], rest[n_cast], rest[n_cast + 1:2 * n_cast + 1]
    kn_ref, vt_ref, qcat_ref, m_ref, acc_ref, s_ref = rest[2 * n_cast + 1:]
    for w_ref, wb_ref in zip(cast_in, cast_out):
        wb_ref[...] = w_ref[...].astype(BF16)
    i = pl.program_id(2)
    n_tiles = vt_ref.shape[1]
    d = DA_V_DIM
    lane = lax.broadcasted_iota(jnp.int32, (1, d), 1)
    first = lane < DA_HEAD_DIM
    ones_row = lax.broadcasted_iota(jnp.int32, (DEN_ROWS, tq), 0) == 0

    half_r = lax.broadcasted_iota(jnp.int32, (d, d), 0) < DA_HEAD_DIM
    half_c = lax.broadcasted_iota(jnp.int32, (d, d), 1) < DA_HEAD_DIM
    sub_avg = jnp.where(half_r == half_c, 1.0 / DA_HEAD_DIM, 0.0).astype(BF16)

    def sub_head_norm(xf):
        return xf * lax.rsqrt(_lane_group_mean(xf * xf, sub_avg) + EPS)

    @pl.when(i == 0)
    def _():
        for hh in range(hb):
            hs = slice(hh * d, (hh + 1) * d)
            kn_ref[hh] = (sub_head_norm(k_ref[:, hs].astype(F32)) * kw_ref[...]).astype(BF16)
            for t in range(n_tiles):
                vt_ref[hh, t, :d] = v_ref[t * tq:(t + 1) * tq, hs].T
                vt_ref[hh, t, d:] = jnp.where(ones_row, 1.0, 0.0).astype(BF16)

    for hh in range(hb):
        qn = sub_head_norm(q_ref[:, hh * d:(hh + 1) * d].astype(F32)) * qw_ref[...] * (DA_HEAD_DIM ** -0.5 * LOG2E)
        qcat_ref[hh] = jnp.concatenate(
            [jnp.where(first, qn, 0.0).T, jnp.where(first, 0.0, qn).T], axis=1).astype(BF16)
    m_ref[...] = jnp.full(m_ref.shape, -jnp.inf, F32)
    acc_ref[...] = jnp.zeros(acc_ref.shape, F32)

    def scores(j, hh):
        start = pl.multiple_of(j * tq, tq)
        return _dot(kn_ref[hh, pl.ds(start, tq), :], qcat_ref[hh])

    lam_p = lam_ref[...]
    lam = (jnp.exp(jnp.sum(lam_p[0:1] * lam_p[1:2], axis=-1, keepdims=True))
           - jnp.exp(jnp.sum(lam_p[2:3] * lam_p[3:4], axis=-1, keepdims=True)) + lambda_init)

    def tile(j, cur, last):
        if last:
            key = lax.broadcasted_iota(jnp.int32, (tq, 2 * tq), 0)
            qry = lax.broadcasted_iota(jnp.int32, (tq, 2 * tq), 1)
            causal = key <= jnp.where(qry >= tq, qry - tq, qry)
        for hh in range(hb):
            if not last:
                s_ref[1 - cur, hh] = scores(j + 1, hh)
            s = s_ref[cur, hh]
            if last:
                s = jnp.where(causal, s, -jnp.inf)
            m_prev = m_ref[hh]
            m_new = jnp.maximum(m_prev, jnp.max(s, axis=0, keepdims=True))
            alpha = jnp.exp2(m_prev - m_new)
            p = jnp.exp2(s - m_new)
            acc = alpha * acc_ref[hh] + _dot(vt_ref[hh, j], p.astype(BF16))
            if not last:
                acc_ref[hh] = acc
                m_ref[hh] = m_new
            else:
                ot = acc[:d] / acc[d:d + 1]
                o = (ot[:, :tq] - lam * ot[:, tq:]).T
                o = o * _rms_scale(o) * sw_ref[...] * (1.0 - lambda_init)
                o_ref[:, hh * d:(hh + 1) * d] = o.astype(o_ref.dtype)

    for hh in range(hb):
        s_ref[0, hh] = scores(0, hh)

    def body(jj, carry):
        tile(2 * jj, 0, False)
        tile(2 * jj + 1, 1, False)
        return carry

    lax.fori_loop(0, i // 2, body, 0)

    @pl.when(i % 2 == 0)
    def _():
        tile(i, 0, True)

    @pl.when(i % 2 == 1)
    def _():
        tile(i - 1, 0, False)
        tile(i, 1, True)


def _diff_attn(proj3, qw2, kw2, lam_p, sw, lambda_init, q_col, k_col, v_col, cast_ws, layer, tq=256, hb=8):
    b, s, _ = proj3.shape
    d = DA_V_DIM
    nh, nq = DA_HEADS // hb, s // tq
    steps = b * nh * nq
    kern = functools.partial(_diff_attn_kernel, tq=tq, hb=hb, lambda_init=lambda_init, n_cast=len(cast_ws))
    vec = lambda shape: pl.BlockSpec(shape, lambda bi, h, i: (0, 0))
    cast_in, cast_out, cast_shapes = [], [], []
    for w in cast_ws:
        rows, cols = w.shape[1:]
        rb = rows // steps
        assert rows % steps == 0 and rb % (2 * SUBLANES) == 0
        cast_in.append(pl.BlockSpec((None, rb, cols), lambda bi, h, i: (layer, (bi * nh + h) * nq + i, 0)))
        cast_out.append(pl.BlockSpec((rb, cols), lambda bi, h, i: ((bi * nh + h) * nq + i, 0)))
        cast_shapes.append(jax.ShapeDtypeStruct((rows, cols), BF16))
    outs = pl.pallas_call(
        kern,
        out_shape=[jax.ShapeDtypeStruct((b, s, DA_HEADS * d), BF16)] + cast_shapes,
        grid=(b, nh, nq),
        in_specs=[
            pl.BlockSpec((None, tq, hb * d), lambda bi, h, i: (bi, i, q_col // hb + h)),
            pl.BlockSpec((None, s, hb * d), lambda bi, h, i: (bi, 0, k_col // hb + h)),
            pl.BlockSpec((None, s, hb * d), lambda bi, h, i: (bi, 0, v_col // hb + h)),
            vec((1, d)), vec((1, d)), vec((4, DA_HEAD_DIM)), vec((1, d)),
        ] + cast_in,
        out_specs=[pl.BlockSpec((None, tq, hb * d), lambda bi, h, i: (bi, i, h))] + cast_out,
        scratch_shapes=[
            pltpu.VMEM((hb, s, d), BF16),
            pltpu.VMEM((hb, s // tq, d + DEN_ROWS, tq), BF16),
            pltpu.VMEM((hb, d, 2 * tq), BF16),
            pltpu.VMEM((hb, 1, 2 * tq), F32),
            pltpu.VMEM((hb, d + DEN_ROWS, 2 * tq), F32),
            pltpu.VMEM((2, hb, tq, 2 * tq), F32),
        ],
        compiler_params=pltpu.CompilerParams(
            dimension_semantics=("parallel", "parallel", "arbitrary"), vmem_limit_bytes=BIG_VMEM_LIMIT),
        name="diff_attn",
    )(proj3, proj3, proj3, qw2, kw2, lam_p, sw, *cast_ws)
    return outs[0], outs[1:]


def _ssd_kernel(z_ref, xs_ref, bc_ref, xsn_ref, bcn_ref, dt_ref, cwx_ref, cbx_ref, cwbc_ref, cbbc_ref,
                dtb_ref, alog_ref, dx_ref, nw_ref, e_ref, *rest, n_cast):
    cast_in, o_ref, cast_out = rest[:n_cast], rest[n_cast], rest[n_cast + 1:2 * n_cast + 1]
    state_ref, cxa_ref, cba_ref, cxb_ref, cbb_ref, y_ref, dtx_ref, acsx_ref = rest[2 * n_cast + 1:]
    L, N, G = SSD_CHUNK, SSD_STATE, SSD_GROUPS
    w = o_ref.shape[-1]
    bcw = bc_ref.shape[-1]
    gw = w // G
    c = pl.program_id(1)

    ext = CONV_TAIL + L
    s_row = lax.broadcasted_iota(jnp.int32, (L, SSD_CONV * ext), 0)
    s_col = lax.broadcasted_iota(jnp.int32, (L, SSD_CONV * ext), 1)
    pick = jnp.zeros((L, SSD_CONV * ext), F32)
    for k in range(SSD_CONV):
        pick = jnp.where(s_col == s_row + (k * ext + CONV_TAIL - k), 0.5, pick)
    pick = pick.astype(BF16)

    def conv_silu(rows, w_ref, b_ref, cs, dst_ref):
        n_rt = rows.shape[0] // CONV_TAIL
        rows3 = rows.reshape(n_rt, CONV_TAIL, rows.shape[1])
        taps = jnp.concatenate(
            [(rows3 * w_ref[SSD_CONV - 1 - k, :, cs][None]).reshape(rows.shape) for k in range(SSD_CONV)], axis=0)
        h = _dot(pick, taps) + 0.5 * b_ref[:, cs]
        dst_ref[:, cs] = h + h * jnp.tanh(h)

    def conv_slice(g, rows_of, dst_x_ref, dst_bc_ref):
        xc = slice(g * (w // G), (g + 1) * (w // G))
        bc = slice(g * (bcw // G), (g + 1) * (bcw // G))
        conv_silu(rows_of(xs_ref, xsn_ref, xc), cwx_ref, cbx_ref, xc, dst_x_ref)
        conv_silu(rows_of(bc_ref, bcn_ref, bc), cwbc_ref, cbbc_ref, bc, dst_bc_ref)

    def rows_first(cur_ref, nxt_ref, cs):
        return jnp.concatenate([jnp.zeros((CONV_TAIL, cs.stop - cs.start), BF16), cur_ref[0:L, cs]], axis=0)

    def rows_second(cur_ref, nxt_ref, cs):
        return cur_ref[L - CONV_TAIL:2 * L, cs]

    def rows_next(cur_ref, nxt_ref, cs):
        return jnp.concatenate([cur_ref[2 * L - CONV_TAIL:2 * L, cs], nxt_ref[:, cs]], axis=0)

    @pl.when(c == 0)
    def _():
        state_ref[...] = jnp.zeros(state_ref.shape, F32)
        for g in range(G):
            conv_slice(g, rows_first, cxa_ref, cba_ref)

    r_i = lax.broadcasted_iota(jnp.int32, (L, L), 0)
    c_i = lax.broadcasted_iota(jnp.int32, (L, L), 1)
    causal = c_i <= r_i
    tril = jnp.where(causal, 1.0, 0.0).astype(BF16)
    lane = lax.broadcasted_iota(jnp.int32, (1, LANES), 1)
    first = lane < SSD_HEAD_DIM
    n_heads = w // SSD_HEAD_DIM
    heads_per_group = gw // SSD_HEAD_DIM

    def expand(v, parts):
        packed, r = None, jnp.where(lane < n_heads, v, 0.0)
        for t in range(parts):
            p = r.astype(BF16).astype(F32)
            packed = p if t == 0 else packed + pltpu.roll(p, t * n_heads, 1)
            r = r - p
        return _dot(packed.astype(BF16), e_ref[...])

    def mix_chunk(r0, cx_ref, cb_ref, side):
        rows = slice(r0, r0 + L)
        dtx = dt_ref[rows, :] + dtb_ref[...]
        dt = jnp.maximum(dtx, 0.0) + jnp.log1p(jnp.exp(-jnp.abs(dtx)))
        a = -jnp.exp(alog_ref[...]) * LOG2E
        acs = sum(_dot(tril, p) for p in _split_bf16(dt * a, 3))
        acs_t = acs.T
        dtx_ref[...] = expand(dt, 2)
        acsx_ref[...] = expand(acs, 3)
        for g in range(G):
            side(g)
            cols = slice(g * gw, (g + 1) * gw)
            xs_g = cx_ref[:, cols]
            acs_g = acsx_ref[:, cols]
            last_g = acsx_ref[L - 1:L, cols]
            xdt_g = xs_g * dtx_ref[:, cols]
            xds_g = (xdt_g * jnp.exp2(last_g - acs_g)).astype(BF16)
            b_g = cb_ref[:, g * N:(g + 1) * N]
            c_g = cb_ref[:, (G + g) * N:(G + g + 1) * N].astype(BF16)
            cb = _dot_nt(c_g, b_g.astype(BF16))
            state_old = state_ref[:, cols]
            y_off = _dot(c_g, state_old.astype(BF16)) * jnp.exp2(acs_g)
            state_ref[:, cols] = state_old * jnp.exp2(last_g) + _dot(b_g.T.astype(BF16), xds_g)
            for pair in range(heads_per_group // 2):
                h0 = g * heads_per_group + 2 * pair
                pc = slice(pair * LANES, (pair + 1) * LANES)
                xp = xdt_g[:, pc]
                x_pair = jnp.concatenate([jnp.where(first, xp, 0.0), jnp.where(first, 0.0, xp)], axis=0).astype(BF16)
                mats = []
                for h in (h0, h0 + 1):
                    seg = acs[:, h:h + 1] - acs_t[h:h + 1, :]
                    dec = jnp.exp2(jnp.where(causal, seg, -jnp.inf))
                    mats.append((cb * dec).astype(BF16))
                y_ref[:, pc] = _dot(jnp.concatenate(mats, axis=1), x_pair)
            y = y_ref[...] + y_off + xs_g * dx_ref[:, cols]
            y = y * _silu(z_ref[rows, cols].astype(F32))
            o_ref[rows, cols] = (y * _rms_scale(y) * nw_ref[:, cols]).astype(o_ref.dtype)

    mix_chunk(0, cxa_ref, cba_ref, lambda g: conv_slice(g, rows_second, cxb_ref, cbb_ref))
    for w_ref, wb_ref in zip(cast_in, cast_out):
        wb_ref[...] = w_ref[...].astype(BF16)
    mix_chunk(L, cxb_ref, cbb_ref, lambda g: conv_slice(g, rows_next, cxa_ref, cba_ref))


def _ssd(proj3, dt3, cwx, cbx, cwbc, cbbc, dtb, alog, dx, nw, e, z_col, xs_col, bc_col, cast_ws, layer):
    b, s, _ = proj3.shape
    w = e.shape[1]
    bcw = cwbc.shape[2]
    L = SSD_CHUNK
    nc = s // L
    assert nc % 2 == 0
    steps = b * (nc // 2)
    cast_in, cast_out, cast_shapes = [], [], []
    for cw in cast_ws:
        rows, cols = cw.shape[1:]
        rb = rows // steps
        assert rows % steps == 0 and rb % (2 * SUBLANES) == 0
        cast_in.append(pl.BlockSpec((None, rb, cols), lambda bi, c: (layer, bi * (nc // 2) + c, 0)))
        cast_out.append(pl.BlockSpec((rb, cols), lambda bi, c: (bi * (nc // 2) + c, 0)))
        cast_shapes.append(jax.ShapeDtypeStruct((rows, cols), BF16))
    full = lambda shape: pl.BlockSpec(shape, lambda bi, c: (0,) * len(shape))
    nxt = lambda c: jnp.minimum(2 * c + 2, nc - 1)
    outs = pl.pallas_call(
        functools.partial(_ssd_kernel, n_cast=len(cast_ws)),
        out_shape=[jax.ShapeDtypeStruct((b, s, w), BF16)] + cast_shapes,
        grid=(b, nc // 2),
        in_specs=[
            pl.BlockSpec((None, 2 * L, w), lambda bi, c: (bi, c, z_col)),
            pl.BlockSpec((None, 2 * L, w), lambda bi, c: (bi, c, xs_col)),
            pl.BlockSpec((None, 2 * L, bcw), lambda bi, c: (bi, c, bc_col)),
            pl.BlockSpec((None, L, w), lambda bi, c: (bi, nxt(c), xs_col)),
            pl.BlockSpec((None, L, bcw), lambda bi, c: (bi, nxt(c), bc_col)),
            pl.BlockSpec((None, 2 * L, LANES), lambda bi, c: (bi, c, 0)),
            full((SSD_CONV, CONV_TAIL, w)), full((1, w)), full((SSD_CONV, CONV_TAIL, bcw)), full((1, bcw)),
            full((1, LANES)), full((1, LANES)), full((1, w)), full((1, w)), full((LANES, w)),
        ] + cast_in,
        out_specs=[pl.BlockSpec((None, 2 * L, w), lambda bi, c: (bi, c, 0))] + cast_out,
        scratch_shapes=[
            pltpu.VMEM((SSD_STATE, w), F32),
            pltpu.VMEM((L, w), F32), pltpu.VMEM((L, bcw), F32),
            pltpu.VMEM((L, w), F32), pltpu.VMEM((L, bcw), F32),
            pltpu.VMEM((L, w // SSD_GROUPS), F32),
            pltpu.VMEM((L, w), F32), pltpu.VMEM((L, w), F32),
        ],
        compiler_params=pltpu.CompilerParams(
            dimension_semantics=("parallel", "arbitrary"), vmem_limit_bytes=VMEM_LIMIT),
        name="ssd",
    )(proj3, proj3, proj3, proj3, proj3, dt3, cwx, cbx, cwbc, cbbc, dtb, alog, dx, nw, e, *cast_ws)
    return outs[0], outs[1:]


def _mix_out_kernel(a_ref, s_ref, x_ref, w_ref, xnw_ref, wq_ref, kv_ref, qw_ref, wo_ref, o_ref):
    d = X_HEAD_DIM
    ka = a_ref.shape[1]
    h1 = x_ref[...] + _dot(a_ref[...], w_ref[0])
    for c in range(1, w_ref.shape[0]):
        h1 = h1 + _dot(s_ref[:, (c - 1) * ka:c * ka], w_ref[c])
    hn = (h1 * _rms_scale(h1) * xnw_ref[...]).astype(BF16)
    q = _dot(hn, wq_ref[...])
    outs = []
    for hh in range(X_HEADS):
        qf = q[:, hh * d:(hh + 1) * d]
        qn = (qf * _rms_scale(qf) * qw_ref[...] * (d ** -0.5 * LOG2E)).astype(BF16)
        s = _dot_nt(qn, kv_ref[:, hh * d:(hh + 1) * d])
        p = jnp.exp2(s - jnp.max(s, axis=-1, keepdims=True))
        v = kv_ref[:, (X_HEADS + hh) * d:(X_HEADS + hh + 1) * d]
        o = _dot(p.astype(BF16), v) / jnp.sum(p, axis=-1, keepdims=True)
        outs.append(o.astype(BF16))
    o_ref[...] = h1 + _dot(jnp.concatenate(outs, axis=-1), wo_ref[...])


def _mix_out(a2, s2, x2, w_out, xnw, wq, kv3, qw, wo, seq, tm=512):
    m, ka = a2.shape
    ks = s2.shape[1]
    dm = x2.shape[1]
    xw = wq.shape[1]
    mlen = kv3.shape[1]
    assert seq % tm == 0 and ks % ka == 0
    once = dict(pipeline_mode=pl.Buffered(1))
    return pl.pallas_call(
        _mix_out_kernel,
        out_shape=jax.ShapeDtypeStruct((m, dm), F32),
        grid=(m // tm,),
        in_specs=[
            pl.BlockSpec((tm, ka), lambda i: (i, 0)),
            pl.BlockSpec((tm, ks), lambda i: (i, 0)),
            pl.BlockSpec((tm, dm), lambda i: (i, 0)),
            pl.BlockSpec((1 + ks // ka, ka, dm), lambda i: (0, 0, 0), **once),
            pl.BlockSpec((1, dm), lambda i: (0, 0)),
            pl.BlockSpec((dm, xw), lambda i: (0, 0), **once),
            pl.BlockSpec((None, mlen, 2 * xw), lambda i: (i // (seq // tm), 0, 0)),
            pl.BlockSpec((1, X_HEAD_DIM), lambda i: (0, 0)),
            pl.BlockSpec((xw, dm), lambda i: (0, 0), **once),
        ],
        out_specs=pl.BlockSpec((tm, dm), lambda i: (i, 0)),
        compiler_params=pltpu.CompilerParams(
            dimension_semantics=("parallel",), vmem_limit_bytes=BIG_VMEM_LIMIT),
        name="mix_out",
    )(a2, s2, x2, w_out.reshape(1 + ks // ka, ka, dm), xnw, wq, kv3, qw, wo)


def _ffn_kernel(h0_ref, hcur_ref, hnext_ref, nw_ref, wg_ref, wu_ref, wd_ref, o_ref, hn_even_ref, hn_odd_ref, *, n_chunks):
    i = pl.program_id(0)
    f = pl.program_id(1)
    rc = hcur_ref.shape[0]
    norm = lambda h: (h * _rms_scale(h) * nw_ref[...]).astype(BF16)

    @pl.when((i == 0) & (f == 0))
    def _():
        hn_even_ref[...] = norm(h0_ref[...])

    def step(hn_ref, hn_next_ref, first):
        rows = pl.ds(pl.multiple_of(jnp.minimum(f, n_chunks - 1) * rc, rc), rc)
        hn_next_ref[rows, :] = norm(hnext_ref[...])
        if not first:
            for c in range(1, n_chunks):
                o_ref[c * rc:(c + 1) * rc, :] += jnp.where(f == c, hcur_ref[...], 0.0)
        hn = hn_ref[...]
        act = (_silu(_dot(hn, wg_ref[...])) * _dot(hn, wu_ref[...])).astype(BF16)
        if first:
            o_ref[...] = _dot(act, wd_ref[...])
            o_ref[0:rc, :] += hcur_ref[...]
        else:
            o_ref[...] += _dot(act, wd_ref[...])

    for parity, (cur, nxt) in enumerate(((hn_even_ref, hn_odd_ref), (hn_odd_ref, hn_even_ref))):
        pl.when((i % 2 == parity) & (f == 0))(functools.partial(step, cur, nxt, True))
        pl.when((i % 2 == parity) & (f > 0))(functools.partial(step, cur, nxt, False))


def _ffn(h2, nw, wg, wu, wd, tm=1024, tf=512, n_chunks=8):
    m, k = h2.shape
    dff = wg.shape[1]
    rc = tm // n_chunks
    assert dff // tf >= n_chunks and tm % n_chunks == 0 and m % tm == 0
    chunk = lambda i, f: jnp.minimum(f, n_chunks - 1)
    return pl.pallas_call(
        functools.partial(_ffn_kernel, n_chunks=n_chunks),
        out_shape=jax.ShapeDtypeStruct((m, k), F32),
        grid=(m // tm, dff // tf),
        in_specs=[
            pl.BlockSpec((tm, k), lambda i, f: (0, 0), pipeline_mode=pl.Buffered(1)),
            pl.BlockSpec((rc, k), lambda i, f: (i * n_chunks + chunk(i, f), 0)),
            pl.BlockSpec((rc, k), lambda i, f: (jnp.minimum((i + 1) * n_chunks + chunk(i, f), m // rc - 1), 0)),
            pl.BlockSpec((1, k), lambda i, f: (0, 0)),
            pl.BlockSpec((k, tf), lambda i, f: (0, f)),
            pl.BlockSpec((k, tf), lambda i, f: (0, f)),
            pl.BlockSpec((tf, k), lambda i, f: (f, 0)),
        ],
        out_specs=pl.BlockSpec((tm, k), lambda i, f: (i, 0)),
        scratch_shapes=[pltpu.VMEM((tm, k), BF16), pltpu.VMEM((tm, k), BF16)],
        compiler_params=pltpu.CompilerParams(
            dimension_semantics=("arbitrary", "arbitrary"), vmem_limit_bytes=BIG_VMEM_LIMIT),
        name="ffn",
    )(h2, h2, h2, nw, wg, wu, wd)


def _pad_lanes(v, fill=0.0):
    return jnp.pad(v, ((0, 0), (0, LANES - v.shape[1])), constant_values=fill)


def kernel(x, mem, mix_norm_w, w_in, da_q_norm_w, da_k_norm_w, da_lambda_q1, da_lambda_k1, da_lambda_q2, da_lambda_k2, da_subln_w, ssd_conv_w, ssd_conv_b, ssd_dt_bias, ssd_a_log, ssd_d, ssd_norm_w, w_out, xattn_norm_w, mem_norm_w, xattn_w_q, xattn_w_kv, xattn_q_norm_w, xattn_k_norm_w, xattn_w_o, ffn_norm_w, ffn_w_gate, ffn_w_up, ffn_w_down):
    b, s, dm = x.shape
    mlen = mem.shape[1]
    depth = w_in.shape[0]
    da_w = DA_HEADS * DA_V_DIM
    ssd_w = ssd_norm_w.shape[1]
    ssd_heads = ssd_w // SSD_HEAD_DIM
    gn = SSD_GROUPS * SSD_STATE
    xw = X_HEADS * X_HEAD_DIM
    o_q, o_k, o_v, o_z = 0, da_w, 2 * da_w, 3 * da_w
    o_xs = o_z + ssd_w
    o_bc = o_xs + ssd_w
    o_dt = o_bc + 2 * gn
    z_col, xs_col = 0, 1
    bc_col = (2 * ssd_w) // (2 * gn)
    q_col = (2 * ssd_w + 2 * gn) // DA_V_DIM
    k_col = q_col + DA_HEADS
    v_col = k_col + DA_HEADS
    assert (2 * ssd_w) % (2 * gn) == 0 and ssd_heads <= LANES

    assert LANES % ssd_heads == 0 and LANES // ssd_heads >= 3
    expand = np.zeros((LANES, ssd_w), np.float32)
    for r in range(LANES):
        h = r % ssd_heads
        expand[r, h * SSD_HEAD_DIM:(h + 1) * SSD_HEAD_DIM] = 1.0
    expand = jnp.asarray(expand, BF16)

    h = x.reshape(b * s, dm)
    mem2 = mem.reshape(b * mlen, dm)
    for i in range(depth):
        lambda_init = 0.8 - 0.6 * math.exp(-0.3 * i)
        proj, dt_raw = _in_proj(h, mix_norm_w[i][None], jnp.swapaxes(w_in, 1, 2), i, o_dt, o_z)
        proj3 = proj.reshape(b, s, -1)

        lam_p = jnp.stack([da_lambda_q1[i], da_lambda_k1[i], da_lambda_q2[i], da_lambda_k2[i]])
        a_out, _ = _diff_attn(
            proj3, jnp.tile(da_q_norm_w[i], 2)[None], jnp.tile(da_k_norm_w[i], 2)[None],
            lam_p, da_subln_w[i][None], lambda_init, q_col, k_col, v_col, [], i)

        cw = jnp.broadcast_to(ssd_conv_w[i][:, None, :], (SSD_CONV, CONV_TAIL, ssd_conv_w.shape[2])).astype(BF16)
        cb = ssd_conv_b[i][None]
        s_out, (wo_b, wq_b, xwo_b, wg_b, wu_b, wd_b) = _ssd(
            proj3, dt_raw.reshape(b, s, LANES),
            cw[:, :, :ssd_w], cb[:, :ssd_w], cw[:, :, ssd_w:], cb[:, ssd_w:],
            _pad_lanes(ssd_dt_bias[i][None]), _pad_lanes(ssd_a_log[i][None]),
            jnp.repeat(ssd_d[i], SSD_HEAD_DIM)[None], ssd_norm_w[i][None], expand,
            z_col, xs_col, bc_col,
            [w_out, xattn_w_q, xattn_w_o, ffn_w_gate, ffn_w_up, ffn_w_down], i)

        xkv = _xkv_proj(mem2, mem_norm_w[i][None], xattn_w_kv, i, xattn_k_norm_w[i][None])
        h2 = _mix_out(a_out.reshape(b * s, da_w), s_out.reshape(b * s, ssd_w), h, wo_b,
                      xattn_norm_w[i][None], wq_b, xkv.reshape(b, mlen, 2 * xw), xattn_q_norm_w[i][None], xwo_b, s)

        h = _ffn(h2, ffn_norm_w[i][None], wg_b, wu_b, wd_b)
    return h.reshape(b, s, dm)
```
